```python
import math
import jax, jax.numpy as jnp
from jax import lax
import numpy as np

D_MODEL = 1024
BATCH = 16
SEQ = 2048
DEPTH = 2

F32 = jnp.float32
D_MIX = D_MODEL
NORM_EPS = 1e-6
Q_BLOCK = 128
MLA_HEADS = 6
MLA_NOPE = 64
MLA_ROPE = 32
MLA_V = 64
MLA_Q_RANK = 256
MLA_KV_RANK = 128
ROPE_THETA = 10000.0
MLA_W = MLA_HEADS * MLA_V
DIFF_HEADS = 4
DIFF_DH = 32
DIFF_W = DIFF_HEADS * 2 * DIFF_DH
DIFF_SUBLN_EPS = 1e-5
RW_HEADS = 6
RW_N = 64
RW_W = RW_HEADS * RW_N
RW_DECAY_RANK = 64
RW_AAA_RANK = 64
RW_MV_RANK = 32
RW_GN_EPS = 64e-5
RW_SHIFT_BASE = 3 * RW_W + RW_DECAY_RANK + RW_AAA_RANK
C_BASE = MLA_Q_RANK + MLA_KV_RANK + MLA_ROPE + 3 * DIFF_W + D_MIX + RW_SHIFT_BASE

kernel_name = 'hymba_style_mla_diff_rwkv7_hybrid'


def _split(t, sizes):
    return jnp.split(t, np.cumsum(sizes).tolist(), axis=-1)


def _rmsnorm(x, g, eps=NORM_EPS):
    xf = x.astype(F32)
    y = xf * lax.rsqrt(jnp.mean(xf * xf, axis=-1, keepdims=True) + eps) * g.astype(F32)
    return y.astype(x.dtype)


def _rope(t, positions):
    half = t.shape[-1] // 2
    inv = ROPE_THETA ** (-jnp.arange(half, dtype=F32) / half)
    ang = positions.astype(F32)[:, None] * inv[None, :]
    cos = jnp.cos(ang)[None, :, None, :]
    sin = jnp.sin(ang)[None, :, None, :]
    t = t.astype(F32)
    t1, t2 = t[..., :half], t[..., half:]
    return jnp.concatenate([t1 * cos - t2 * sin, t1 * sin + t2 * cos], axis=-1)


def _causal_mask(q0, kend):
    q_idx = q0 + jnp.arange(Q_BLOCK)
    k_idx = jnp.arange(kend)
    return k_idx[None, :] <= q_idx[:, None]


def _alibi_slopes(n):
    return 2.0 ** (-8.0 * jnp.arange(1, n + 1, dtype=F32) / n)


def _mla_branch(cq, ckv, kpe, positions, g_q, g_kv, w_uq, w_ukv):
    B, S, _ = cq.shape
    dt = cq.dtype
    q = (_rmsnorm(cq, g_q) @ w_uq).reshape(B, S, MLA_HEADS, MLA_NOPE + MLA_ROPE)
    kv = (_rmsnorm(ckv, g_kv) @ w_ukv).reshape(B, S, MLA_HEADS, MLA_NOPE + MLA_V)
    q_nope, q_pe = q[..., :MLA_NOPE], q[..., MLA_NOPE:]
    k_nope, v = kv[..., :MLA_NOPE], kv[..., MLA_NOPE:]
    q_pe = _rope(q_pe, positions)
    k_pe = jnp.broadcast_to(_rope(kpe[:, :, None, :], positions), (B, S, MLA_HEADS, MLA_ROPE))
    q = jnp.concatenate([q_nope.astype(F32), q_pe], axis=-1).transpose(0, 2, 1, 3)
    k = jnp.concatenate([k_nope.astype(F32), k_pe], axis=-1).transpose(0, 2, 1, 3)
    v = v.astype(F32).transpose(0, 2, 1, 3)
    scale = (MLA_NOPE + MLA_ROPE) ** -0.5
    outs = []
    for q0 in range(0, S, Q_BLOCK):
        kend = q0 + Q_BLOCK
        s = jnp.einsum('bhqd,bhkd->bhqk', q[:, :, q0:kend], k[:, :, :kend]) * scale
        s = jnp.where(_causal_mask(q0, kend), s, -jnp.inf)
        p = jax.nn.softmax(s, axis=-1)
        outs.append(jnp.einsum('bhqk,bhkd->bhqd', p, v[:, :, :kend]))
    o = jnp.concatenate(outs, axis=2)
    return o.transpose(0, 2, 1, 3).reshape(B, S, MLA_W).astype(dt)


def _diff_branch(q, k, v, positions, lam, g_sub, layer):
    B, S, _ = q.shape
    dt = q.dtype
    q = q.astype(F32).reshape(B, S, DIFF_HEADS, 2, DIFF_DH).transpose(0, 2, 3, 1, 4)
    k = k.astype(F32).reshape(B, S, DIFF_HEADS, 2, DIFF_DH).transpose(0, 2, 3, 1, 4)
    v = v.astype(F32).reshape(B, S, DIFF_HEADS, 2 * DIFF_DH).transpose(0, 2, 1, 3)
    lam = lam.astype(F32)
    lam_init = 0.8 - 0.6 * math.exp(-0.3 * (layer + 1))
    lam_full = jnp.exp(jnp.sum(lam[0] * lam[1])) - jnp.exp(jnp.sum(lam[2] * lam[3])) + lam_init
    slopes = _alibi_slopes(DIFF_HEADS)[None, :, None, None, None]
    pos = positions.astype(F32)
    scale = DIFF_DH ** -0.5
    outs = []
    for q0 in range(0, S, Q_BLOCK):
        kend = q0 + Q_BLOCK
        s = jnp.einsum('bhmqd,bhmkd->bhmqk', q[:, :, :, q0:kend], k[:, :, :, :kend]) * scale
        dist = jnp.abs(pos[q0:kend, None] - pos[None, :kend])
        s = s - slopes * dist
        s = jnp.where(_causal_mask(q0, kend), s, -jnp.inf)
        p = jax.nn.softmax(s, axis=-1)
        a = p[:, :, 0] - lam_full * p[:, :, 1]
        outs.append(jnp.einsum('bhqk,bhkd->bhqd', a, v[:, :, :kend]))
    o = jnp.concatenate(outs, axis=2)
    o = _rmsnorm(o, g_sub, DIFF_SUBLN_EPS) * (1.0 - lam_init)
    return o.transpose(0, 2, 1, 3).reshape(B, S, DIFF_W).astype(dt)


def _wkv7_scan(r, w, k, v, a, b):
    B, S, H, N = r.shape

    def step(state, inp):
        r_t, w_t, k_t, v_t, a_t, b_t = inp
        sa = jnp.einsum('bhvk,bhk->bhv', state, a_t)
        state = (state * w_t[:, :, None, :] + sa[..., None] * b_t[:, :, None, :]
                 + v_t[..., None] * k_t[:, :, None, :])
        return state, jnp.einsum('bhvk,bhk->bhv', state, r_t)

    xs = tuple(jnp.moveaxis(t, 1, 0) for t in (r, w, k, v, a, b))
    _, ys = lax.scan(step, jnp.zeros((B, H, N, N), F32), xs)
    return jnp.moveaxis(ys, 0, 1)


def _rwkv7_branch(p, mu, w0, w2, a0, a2, k_k, k_a, r_k, ln_w, ln_b, v_first, v0, v2):
    B, S, _ = p.shape
    dt = p.dtype
    p = p.astype(F32)
    prev = jnp.pad(p, ((0, 0), (1, 0), (0, 0)))[:, :S]
    xs = p + (prev - p) * mu.astype(F32)
    r, k, v, hw, ha, hv = _split(xs, [RW_W, RW_W, RW_W, RW_DECAY_RANK, RW_AAA_RANK])
    w = -jax.nn.softplus(-(w0 + jnp.tanh(hw) @ w2)) - 0.5
    a = jax.nn.sigmoid(a0 + ha @ a2)
    if v_first is None:
        v_first = v
    else:
        v = v + (v_first - v) * jax.nn.sigmoid(v0 + hv @ v2)
    heads = lambda t: t.reshape(B, S, RW_HEADS, RW_N)
    kk = heads(k * k_k)
    kk = kk / jnp.maximum(jnp.sqrt(jnp.sum(kk * kk, axis=-1, keepdims=True)), 1e-12)
    k = k * (1.0 + (a - 1.0) * k_a)
    r, k, v, a = heads(r), heads(k), heads(v), heads(a)
    decay = jnp.exp(-jnp.exp(heads(w)))
    y = _wkv7_scan(r, decay, k, v, -kk, kk * a)
    mean = jnp.mean(y, axis=-1, keepdims=True)
    var = jnp.mean(jnp.square(y - mean), axis=-1, keepdims=True)
    y = ((y - mean) * lax.rsqrt(var + RW_GN_EPS) * ln_w.reshape(RW_HEADS, RW_N)
         + ln_b.reshape(RW_HEADS, RW_N))
    y = y + jnp.sum(r * k * r_k, axis=-1, keepdims=True) * v
    return y.reshape(B, S, RW_W).astype(dt), v_first


def setup_inputs(seed: int = 0) -> dict:
    key = jax.random.key(seed)
    ks = iter(jax.random.split(key, 32))
    normal = lambda shape, scale: jax.random.normal(next(ks), shape, F32) * scale
    gain = lambda shape: 1.0 + normal(shape, 0.02)
    uniform = lambda shape, lo, hi: jax.random.uniform(next(ks), shape, F32, lo, hi)
    L, Lv = DEPTH, DEPTH - 1
    x = normal((BATCH, SEQ, D_MODEL), 1.0)
    offset = jax.random.randint(next(ks), (), 0, 4096, dtype=jnp.int32)
    positions = offset + jnp.arange(SEQ, dtype=jnp.int32)
    return {
        'x': x,
        'positions': positions,
        'pre_g': gain((L, D_MODEL)),
        'w_in': normal((L, D_MODEL, C_BASE), D_MODEL ** -0.5),
        'w_in_vres': normal((Lv, D_MODEL, RW_MV_RANK), D_MODEL ** -0.5),
        'w_out': normal((L, D_MIX, D_MODEL), D_MIX ** -0.5),
        'mla_gq': gain((L, MLA_Q_RANK)),
        'mla_gkv': gain((L, MLA_KV_RANK)),
        'mla_wuq': normal((L, MLA_Q_RANK, MLA_HEADS * (MLA_NOPE + MLA_ROPE)), MLA_Q_RANK ** -0.5),
        'mla_wukv': normal((L, MLA_KV_RANK, MLA_HEADS * (MLA_NOPE + MLA_V)), MLA_KV_RANK ** -0.5),
        'diff_lam': normal((L, 4, DIFF_DH), 0.1),
        'diff_gsub': gain((L, 2 * DIFF_DH)),
        'rw_mu': uniform((L, RW_SHIFT_BASE), 0.0, 1.0),
        'rw_mu_vres': uniform((Lv, RW_MV_RANK), 0.0, 1.0),
        'rw_w0': uniform((L, RW_W), -6.0, -1.0),
        'rw_w2': normal((L, RW_DECAY_RANK, RW_W), RW_DECAY_RANK ** -0.5),
        'rw_a0': normal((L, RW_W), 0.1),
        'rw_a2': normal((L, RW_AAA_RANK, RW_W), RW_AAA_RANK ** -0.5),
        'rw_v0': 1.0 + normal((Lv, RW_W), 0.1),
        'rw_v2': normal((Lv, RW_MV_RANK, RW_W), RW_MV_RANK ** -0.5),
        'rw_kk': 0.85 + normal((L, RW_W), 0.02),
        'rw_ka': gain((L, RW_W)),
        'rw_rk': normal((L, RW_HEADS, RW_N), 0.1),
        'rw_lnw': gain((L, RW_W)),
        'rw_lnb': normal((L, RW_W), 0.02),
        'final_g': gain((D_MODEL,)),
    }


def reference(x, positions, pre_g, w_in, w_in_vres, w_out, mla_gq, mla_gkv, mla_wuq, mla_wukv,
              diff_lam, diff_gsub, rw_mu, rw_mu_vres, rw_w0, rw_w2, rw_a0, rw_a2, rw_v0, rw_v2,
              rw_kk, rw_ka, rw_rk, rw_lnw, rw_lnb, final_g):
    v_first = None
    for layer in range(DEPTH):
        h = _rmsnorm(x, pre_g[layer])
        if layer == 0:
            w_comb, mu = w_in[0], rw_mu[0]
            v0, v2 = None, None
        else:
            w_comb = jnp.concatenate([w_in[layer], w_in_vres[layer - 1]], axis=1)
            mu = jnp.concatenate([rw_mu[layer], rw_mu_vres[layer - 1]], axis=0)
            v0, v2 = rw_v0[layer - 1], rw_v2[layer - 1]
        proj = h @ w_comb
        cq, ckv, kpe, dq, dk, dv, gate, rw = _split(
            proj, [MLA_Q_RANK, MLA_KV_RANK, MLA_ROPE, DIFF_W, DIFF_W, DIFF_W, D_MIX])
        o_mla = _mla_branch(cq, ckv, kpe, positions, mla_gq[layer], mla_gkv[layer],
                            mla_wuq[layer], mla_wukv[layer])
        o_diff = _diff_branch(dq, dk, dv, positions, diff_lam[layer], diff_gsub[layer], layer)
        o_rw, v_first = _rwkv7_branch(rw, mu, rw_w0[layer], rw_w2[layer], rw_a0[layer], rw_a2[layer],
                                      rw_kk[layer], rw_ka[layer], rw_rk[layer], rw_lnw[layer],
                                      rw_lnb[layer], v_first, v0, v2)
        g_mla, g_diff, g_rw = _split(gate, [MLA_W, DIFF_W])
        mixed = jnp.concatenate([o_mla * jax.nn.silu(g_mla), o_diff * jax.nn.silu(g_diff),
                                 o_rw * jax.nn.silu(g_rw)], axis=-1)
        x = x + mixed @ w_out[layer]
    return _rmsnorm(x, final_g)
```

```python
import functools
import math

import jax
import jax.numpy as jnp
from jax import lax
from jax.experimental import pallas as pl
from jax.experimental.pallas import tpu as pltpu

F32 = jnp.float32
BF16 = jnp.bfloat16

D_MODEL = 1024
D_MIX = 1024
NORM_EPS = 1e-6
MLA_HEADS, MLA_NOPE, MLA_ROPE, MLA_V = 6, 64, 32, 64
MLA_Q_RANK, MLA_KV_RANK = 256, 128
MLA_QK = MLA_NOPE + MLA_ROPE
ROPE_THETA = 10000.0
MLA_W = MLA_HEADS * MLA_V
DIFF_HEADS, DIFF_DH = 4, 32
DIFF_W = DIFF_HEADS * 2 * DIFF_DH
DIFF_SUBLN_EPS = 1e-5
RW_HEADS, RW_N = 6, 64
RW_W = RW_HEADS * RW_N
RW_DECAY_RANK, RW_AAA_RANK, RW_MV_RANK = 64, 64, 32
RW_GN_EPS = 64e-5
RW_SHIFT_BASE = 3 * RW_W + RW_DECAY_RANK + RW_AAA_RANK

LANE = 128
HP = LANE
RW_CHUNK = 64

MLA_G = MLA_Q_RANK + MLA_KV_RANK + 2 * HP
DIFF_G = 3 * DIFF_W
GATE_G = D_MIX
RW_G0 = RW_SHIFT_BASE
RW_G1 = RW_SHIFT_BASE + LANE
OFF_DIFF = MLA_G
OFF_GATE = OFF_DIFF + DIFF_G
OFF_RW = OFF_GATE + GATE_G

VMEM_LIMIT = 52 * 1024 * 1024

_NT = (((1,), (1,)), ((), ()))
_TN = (((0,), (0,)), ((), ()))


def _dot(a, b):
    return jnp.dot(a, b, preferred_element_type=F32)


def _dot_nt(a, b):
    return lax.dot_general(a, b, _NT, preferred_element_type=F32)


def _split_hi_lo(x):
    hi = x.astype(BF16)
    return hi, (x - hi.astype(F32)).astype(BF16)


def _sigmoid(z):
    return 1.0 / (1.0 + jnp.exp(-z))


def _silu(g):
    return g * _sigmoid(g)


def _rms(x, g, eps):
    return x * lax.rsqrt(jnp.mean(x * x, axis=-1, keepdims=True) + eps) * g


def _rope_table_kernel(pos_ref, tab_ref):
    pos = pos_ref[...]
    lane = lax.broadcasted_iota(jnp.int32, (1, LANE), 1)
    half = MLA_ROPE // 2
    in_rope = (lane >= MLA_NOPE) & (lane < MLA_QK)
    first = lane < MLA_NOPE + half
    idx = jnp.where(first, lane - MLA_NOPE, lane - MLA_NOPE - half).astype(F32)
    inv = jnp.exp(idx * (-math.log(ROPE_THETA) / half))
    ang = pos * inv
    cos = jnp.where(in_rope, jnp.cos(ang), jnp.where(lane < MLA_NOPE, 1.0, 0.0))
    sin = jnp.sin(ang)
    sin = jnp.where(in_rope, jnp.where(first, -sin, sin), 0.0)
    scale = MLA_QK ** -0.5
    tab_ref[:, 0 * LANE:1 * LANE] = cos * scale
    tab_ref[:, 1 * LANE:2 * LANE] = sin * scale
    tab_ref[:, 2 * LANE:3 * LANE] = cos
    tab_ref[:, 3 * LANE:4 * LANE] = sin


def _rope_table(positions):
    s = positions.shape[0]
    return pl.pallas_call(
        _rope_table_kernel,
        out_shape=jax.ShapeDtypeStruct((s, 4 * LANE), F32),
        name="rope_table",
    )(positions.astype(F32).reshape(s, 1))


def _inproj_kernel(x_ref, g_ref, w_ref, tab_ref, gq_ref, gkv_ref, wq_ref, wqs_ref, wk_ref, wv_ref,
                   q_ref, k_ref, v_ref, d_ref, gate_ref, rw_ref):
    x = x_ref[...]
    h = _rms(x, g_ref[...], NORM_EPS).astype(BF16)

    mla = _dot(h, w_ref[:, 0:MLA_G])
    cqn = _rms(mla[:, 0:MLA_Q_RANK], gq_ref[...], NORM_EPS).astype(BF16)
    ckvn = _rms(mla[:, MLA_Q_RANK:MLA_Q_RANK + MLA_KV_RANK], gkv_ref[...], NORM_EPS).astype(BF16)
    o_kpe = MLA_Q_RANK + MLA_KV_RANK
    tab = tab_ref[...]
    cos_q, sin_q = tab[:, 0:LANE], tab[:, LANE:2 * LANE]
    cos_k, sin_k = tab[:, 2 * LANE:3 * LANE], tab[:, 3 * LANE:4 * LANE]
    kpe = mla[:, o_kpe:o_kpe + HP] * cos_k + mla[:, o_kpe + HP:o_kpe + 2 * HP] * sin_k
    qa = _dot(cqn, wq_ref[...])
    qs = _dot(cqn, wqs_ref[...])
    kn = _dot(ckvn, wk_ref[...])
    for hd in range(MLA_HEADS):
        sl = slice(hd * HP, (hd + 1) * HP)
        q_ref[:, sl] = (qa[:, sl] * cos_q + qs[:, sl] * sin_q).astype(BF16)
        k_ref[:, sl] = (kn[:, sl] + kpe).astype(BF16)
    v_ref[...] = _dot(ckvn, wv_ref[...]).astype(BF16)

    dd = _dot(h, w_ref[:, OFF_DIFF:OFF_GATE])
    d_ref[:, 0:DIFF_W] = (dd[:, 0:DIFF_W] * (DIFF_DH ** -0.5)).astype(BF16)
    d_ref[:, DIFF_W:DIFF_G] = dd[:, DIFF_W:DIFF_G].astype(BF16)
    gate_ref[...] = _dot(h, w_ref[:, OFF_GATE:OFF_RW]).astype(BF16)
    rw_ref[...] = _dot(h, w_ref[:, OFF_RW:])


def _inproj(x2, g, w, tab, gq, gkv, wq, wqs, wk, wv, *, seq, tm):
    t = x2.shape[0]
    ctot = w.shape[1]
    rw_g = ctot - OFF_RW
    n_seq_tiles = seq // tm
    row = lambda i: (i, 0)
    const = lambda i: (0, 0)
    full = lambda a: pl.BlockSpec(a.shape, const)
    return pl.pallas_call(
        _inproj_kernel,
        grid=(t // tm,),
        in_specs=[
            pl.BlockSpec((tm, D_MODEL), row),
            full(g), full(w),
            pl.BlockSpec((tm, 4 * LANE), lambda i: (i % n_seq_tiles, 0)),
            full(gq), full(gkv), full(wq), full(wqs), full(wk), full(wv),
        ],
        out_specs=[
            pl.BlockSpec((tm, MLA_HEADS * HP), row),
            pl.BlockSpec((tm, MLA_HEADS * HP), row),
            pl.BlockSpec((tm, MLA_W), row),
            pl.BlockSpec((tm, DIFF_G), row),
            pl.BlockSpec((tm, GATE_G), row),
            pl.BlockSpec((tm, rw_g), row),
        ],
        out_shape=[
            jax.ShapeDtypeStruct((t, MLA_HEADS * HP), BF16),
            jax.ShapeDtypeStruct((t, MLA_HEADS * HP), BF16),
            jax.ShapeDtypeStruct((t, MLA_W), BF16),
            jax.ShapeDtypeStruct((t, DIFF_G), BF16),
            jax.ShapeDtypeStruct((t, GATE_G), BF16),
            jax.ShapeDtypeStruct((t, rw_g), F32),
        ],
        compiler_params=pltpu.CompilerParams(
            dimension_semantics=("parallel",), vmem_limit_bytes=VMEM_LIMIT),
        name="inproj",
    )(x2, g, w, tab, gq, gkv, wq, wqs, wk, wv)


def _softmax_step(s, m, l):
    m_new = jnp.maximum(m, jnp.max(s, axis=-1, keepdims=True))
    alpha = jnp.exp(m - m_new)
    p = jnp.exp(s - m_new)
    return m_new, alpha, p, alpha * l + jnp.sum(p, axis=-1, keepdims=True)


def _mla_flash_kernel(q_ref, k_ref, v_ref, gate_ref, o_ref, *, tq):
    i = pl.program_id(2)
    tri = (lax.broadcasted_iota(jnp.int32, (tq, tq), 1)
           <= lax.broadcasted_iota(jnp.int32, (tq, tq), 0))
    outs = []
    for hh in range(2):
        hs = slice(hh * HP, (hh + 1) * HP)
        q = q_ref[:, hs]

        def tile(j, carry, masked, q=q, hs=hs):
            m, l, acc = carry
            rows = pl.ds(pl.multiple_of(j * tq, tq), tq)
            s = _dot_nt(q, k_ref[rows, hs])
            if masked:
                s = jnp.where(tri, s, -jnp.inf)
            m, alpha, p, l = _softmax_step(s, m, l)
            return m, l, alpha * acc + _dot(p.astype(BF16), v_ref[rows, :])

        init = (jnp.full((tq, 1), -jnp.inf, F32), jnp.zeros((tq, 1), F32), jnp.zeros((tq, LANE), F32))
        carry = lax.fori_loop(0, i, functools.partial(tile, masked=False), init)
        _, l, acc = tile(i, carry, True)
        outs.append(acc / l)
    lane = lax.broadcasted_iota(jnp.int32, (tq, LANE), 1)
    o = jnp.where(lane < MLA_V, outs[0], outs[1])
    o_ref[...] = (o * _silu(gate_ref[...].astype(F32))).astype(BF16)


def _mla_flash(q, k, v, gate, *, batch, seq, tq):
    nq = seq // tq
    gate_blk0 = RW_W // LANE
    return pl.pallas_call(
        functools.partial(_mla_flash_kernel, tq=tq),
        grid=(batch, MLA_HEADS // 2, nq),
        in_specs=[
            pl.BlockSpec((tq, 2 * HP), lambda b, p, i: (b * nq + i, p)),
            pl.BlockSpec((seq, 2 * HP), lambda b, p, i: (b, p)),
            pl.BlockSpec((seq, LANE), lambda b, p, i: (b, p)),
            pl.BlockSpec((tq, LANE), lambda b, p, i: (b * nq + i, gate_blk0 + p)),
        ],
        out_specs=pl.BlockSpec((tq, LANE), lambda b, p, i: (b * nq + i, p)),
        out_shape=jax.ShapeDtypeStruct((batch * seq, MLA_W), BF16),
        compiler_params=pltpu.CompilerParams(
            dimension_semantics=("parallel", "parallel", "parallel"), vmem_limit_bytes=VMEM_LIMIT),
        name="mla_flash",
    )(q, k, v, gate)


def _diff_flash_kernel(posc_ref, posr_ref, lam_ref, gsub_ref, q_ref, k_ref, v_ref, gate_ref, o_ref, qm_ref,
                       *, tq, lam_init):
    i = pl.program_id(1)
    n_maps = 2 * DIFF_HEADS
    q = q_ref[...]
    lane_q = lax.broadcasted_iota(jnp.int32, (tq, DIFF_W), 1)
    for hm in range(n_maps):
        own = (lane_q >= hm * DIFF_DH) & (lane_q < (hm + 1) * DIFF_DH)
        qm_ref[hm * tq:(hm + 1) * tq, :] = jnp.where(own, q, jnp.zeros_like(q))
    pos_q = posc_ref[...]
    tri = (lax.broadcasted_iota(jnp.int32, (tq, tq), 1)
           <= lax.broadcasted_iota(jnp.int32, (tq, tq), 0))
    slopes = [2.0 ** (-8.0 * (h + 1) / DIFF_HEADS) for h in range(DIFF_HEADS)]

    def tile(j, carry, masked):
        rows = pl.ds(pl.multiple_of(j * tq, tq), tq)
        s_all = _dot_nt(qm_ref[...], k_ref[rows, :])
        dist = jnp.abs(pos_q - posr_ref[j])
        new = []
        for hm in range(n_maps):
            h = hm // 2
            m, l, acc = carry[hm]
            s = s_all[hm * tq:(hm + 1) * tq] - slopes[h] * dist
            if masked:
                s = jnp.where(tri, s, -jnp.inf)
            m, alpha, p, l = _softmax_step(s, m, l)
            vt = v_ref[rows, (h // 2) * LANE:(h // 2 + 1) * LANE]
            new.append((m, l, alpha * acc + _dot(p.astype(BF16), vt)))
        return tuple(new)

    one = (jnp.full((tq, 1), -jnp.inf, F32), jnp.zeros((tq, 1), F32), jnp.zeros((tq, LANE), F32))
    carry = lax.fori_loop(0, i, functools.partial(tile, masked=False), (one,) * n_maps)
    carry = tile(i, carry, True)

    lam = lam_ref[...]
    lam_full = (jnp.exp(jnp.sum(lam[0:1] * lam[1:2], axis=-1, keepdims=True))
                - jnp.exp(jnp.sum(lam[2:3] * lam[3:4], axis=-1, keepdims=True)) + lam_init)
    lane = lax.broadcasted_iota(jnp.int32, (tq, LANE), 1)
    low = lane < 2 * DIFF_DH
    heads = []
    for h in range(DIFF_HEADS):
        (_, l0, a0), (_, l1, a1) = carry[2 * h], carry[2 * h + 1]
        o = a0 / l0 - lam_full * (a1 / l1)
        valid = low if h % 2 == 0 else jnp.logical_not(low)
        ms = jnp.sum(jnp.where(valid, o * o, 0.0), axis=-1, keepdims=True) * (1.0 / (2 * DIFF_DH))
        heads.append(o * lax.rsqrt(ms + DIFF_SUBLN_EPS) * gsub_ref[...] * (1.0 - lam_init))
    for pr in range(DIFF_HEADS // 2):
        o = jnp.where(low, heads[2 * pr], heads[2 * pr + 1])
        cs = slice(pr * LANE, (pr + 1) * LANE)
        o_ref[:, cs] = (o * _silu(gate_ref[:, cs].astype(F32))).astype(BF16)


def _diff_flash(d, gate, posc, posr, lam, gsub2, *, batch, seq, tq, lam_init):
    nq = seq // tq
    gate_blk = (RW_W + MLA_W) // DIFF_W
    return pl.pallas_call(
        functools.partial(_diff_flash_kernel, tq=tq, lam_init=lam_init),
        grid=(batch, nq),
        in_specs=[
            pl.BlockSpec((tq, 1), lambda b, i: (i, 0)),
            pl.BlockSpec(posr.shape, lambda b, i: (0, 0, 0)),
            pl.BlockSpec(lam.shape, lambda b, i: (0, 0)),
            pl.BlockSpec(gsub2.shape, lambda b, i: (0, 0)),
            pl.BlockSpec((tq, DIFF_W), lambda b, i: (b * nq + i, 0)),
            pl.BlockSpec((seq, DIFF_W), lambda b, i: (b, 1)),
            pl.BlockSpec((seq, DIFF_W), lambda b, i: (b, 2)),
            pl.BlockSpec((tq, DIFF_W), lambda b, i: (b * nq + i, gate_blk)),
        ],
        out_specs=pl.BlockSpec((tq, DIFF_W), lambda b, i: (b * nq + i, 0)),
        out_shape=jax.ShapeDtypeStruct((batch * seq, DIFF_W), BF16),
        scratch_shapes=[pltpu.VMEM((2 * DIFF_HEADS * tq, DIFF_W), BF16)],
        compiler_params=pltpu.CompilerParams(
            dimension_semantics=("parallel", "parallel"), vmem_limit_bytes=VMEM_LIMIT),
        name="diff_flash",
    )(posc, posr, lam, gsub2, d, d, d, gate)


def _rwkv_kernel(*refs, tb, has_vres):
    if has_vres:
        (p_ref, vf_ref, gate_ref, tri_ref, blk_ref, seg_ref, mu_ref, w0_ref, w2_ref, a0_ref, a2_ref,
         v0_ref, v2_ref, kk_ref, ka_ref, rk_ref, lnw_ref, lnb_ref, y_ref,
         state_ref, carry_ref, at_ref, rt_ref, bt_ref, kt_ref, bw_ref, kw_ref, vv_ref, wl_ref, yy_ref) = refs
    else:
        (p_ref, gate_ref, tri_ref, blk_ref, seg_ref, mu_ref, w0_ref, w2_ref, a0_ref, a2_ref,
         kk_ref, ka_ref, rk_ref, lnw_ref, lnb_ref, y_ref, vfo_ref,
         state_ref, carry_ref, at_ref, rt_ref, bt_ref, kt_ref, bw_ref, kw_ref, vv_ref, wl_ref, yy_ref) = refs

    @pl.when(pl.program_id(1) == 0)
    def _():
        state_ref[...] = jnp.zeros_like(state_ref)
        carry_ref[...] = jnp.zeros_like(carry_ref)

    seg = seg_ref[...]

    def segsum(x):
        hi, lo = _split_hi_lo(x)
        return _dot(hi, seg) + _dot(lo, seg)

    p = p_ref[...]
    row = lax.broadcasted_iota(jnp.int32, (tb, 1), 0)
    prev = jnp.where(row == 0, carry_ref[...], pltpu.roll(p, 1, 0))
    carry_ref[...] = p[tb - 1:tb, :]
    xs = p + (prev - p) * mu_ref[...]
    r = xs[:, 0:RW_W]
    k = xs[:, RW_W:2 * RW_W]
    v = xs[:, 2 * RW_W:3 * RW_W]
    hwa = xs[:, 3 * RW_W:3 * RW_W + LANE]
    z = -(w0_ref[...] + _dot(jnp.tanh(hwa).astype(BF16), w2_ref[...]))
    softplus = jnp.maximum(z, 0.0) + jnp.log(1.0 + jnp.exp(-jnp.abs(z)))
    lw = -jnp.exp(-softplus - 0.5)
    a = _sigmoid(a0_ref[...] + _dot(hwa.astype(BF16), a2_ref[...]))
    if has_vres:
        hv = xs[:, RW_SHIFT_BASE:RW_SHIFT_BASE + LANE]
        v = v + (vf_ref[...] - v) * _sigmoid(v0_ref[...] + _dot(hv.astype(BF16), v2_ref[...]))
    else:
        vfo_ref[...] = v
    kk = k * kk_ref[...]
    kk = kk / jnp.maximum(jnp.sqrt(segsum(kk * kk)), 1e-12)
    k = k * (1.0 + (a - 1.0) * ka_ref[...])

    lw_hi, lw_lo = _split_hi_lo(lw)
    cum = _dot(tri_ref[...], lw_hi) + _dot(tri_ref[...], lw_lo)
    tot = _dot(blk_ref[...], lw_hi) + _dot(blk_ref[...], lw_lo)
    w_inv = jnp.exp(-cum)
    w_end = jnp.exp(tot - cum)
    kka = kk * a
    rt_ref[...] = (r * jnp.exp(cum)).astype(BF16)
    at_ref[...] = (-kk * jnp.exp(cum - lw)).astype(BF16)
    bt_ref[...] = (kka * w_inv).astype(BF16)
    kt_ref[...] = (k * w_inv).astype(BF16)
    bw_ref[...] = (kka * w_end).astype(BF16)
    kw_ref[...] = (k * w_end).astype(BF16)
    vv_ref[...] = v.astype(BF16)
    wl_ref[...] = jnp.exp(tot)

    c = RW_CHUNK
    ri = lax.broadcasted_iota(jnp.int32, (2 * c, 2 * c), 0)
    ci = lax.broadcasted_iota(jnp.int32, (2 * c, 2 * c), 1)
    ci = jnp.where(ci >= c, ci - c, ci)
    keep = ((ri >= c) & (ci <= ri - c)) | ((ri < c) & (ci < ri))
    eye = (lax.broadcasted_iota(jnp.int32, (c, c), 0) == lax.broadcasted_iota(jnp.int32, (c, c), 1)).astype(F32)

    def chunk(ic, _):
        rows = pl.ds(pl.multiple_of(ic * c, c), c)
        for h in range(RW_HEADS):
            ls = slice(h * RW_N, (h + 1) * RW_N)
            vh = vv_ref[rows, ls]
            s0 = state_ref[h]
            xar = jnp.concatenate([at_ref[rows, ls], rt_ref[rows, ls]], axis=0)
            ybk = jnp.concatenate([bt_ref[rows, ls], kt_ref[rows, ls]], axis=0)
            aa = jnp.where(keep, _dot_nt(xar, ybk), 0.0)
            n = aa[0:c, 0:c]
            x = eye + n
            pw = n.astype(BF16)
            for _ in range(5):
                pw = _dot(pw, pw).astype(BF16)
                x = x + _dot(x.astype(BF16), pw)
            rhs = _dot_nt(xar, s0.astype(BF16))
            u = _dot(x.astype(BF16), (rhs[0:c] + _dot(aa[0:c, c:2 * c].astype(BF16), vh)).astype(BF16))
            uv = jnp.concatenate([u.astype(BF16), vh], axis=0)
            yy_ref[rows, ls] = rhs[c:2 * c] + _dot(aa[c:2 * c, :].astype(BF16), uv)
            zz = jnp.concatenate([bw_ref[rows, ls], kw_ref[rows, ls]], axis=0)
            state_ref[h] = (s0 * wl_ref[pl.ds(ic * c, 1), ls]
                            + lax.dot_general(uv, zz, _TN, preferred_element_type=F32))
        return 0

    lax.fori_loop(0, tb // c, chunk, 0)

    y = yy_ref[...]
    inv_n = 1.0 / RW_N
    dlt = y - segsum(y) * inv_n
    var = segsum(dlt * dlt) * inv_n
    y = dlt * lax.rsqrt(var + RW_GN_EPS) * lnw_ref[...] + lnb_ref[...]
    y = y + segsum(r * k * rk_ref[...]) * v
    y_ref[...] = (y * _silu(gate_ref[...].astype(F32))).astype(BF16)


def _rwkv(p, vfirst, gate, consts, params, *, batch, seq, tb):
    has_vres = vfirst is not None
    nb = seq // tb
    cw = p.shape[1]
    row = lambda b, i: (b * nb + i, 0)
    full = lambda a: pl.BlockSpec(a.shape, lambda b, i: (0,) * a.ndim)
    ins = [p] + ([vfirst] if has_vres else []) + [gate] + list(consts) + list(params)
    in_specs = ([pl.BlockSpec((tb, cw), row)]
                + ([pl.BlockSpec((tb, RW_W), row)] if has_vres else [])
                + [pl.BlockSpec((tb, RW_W), row)]
                + [full(a) for a in consts] + [full(a) for a in params])
    out_specs = [pl.BlockSpec((tb, RW_W), row)]
    out_shape = [jax.ShapeDtypeStruct((batch * seq, RW_W), BF16)]
    if not has_vres:
        out_specs.append(pl.BlockSpec((tb, RW_W), row))
        out_shape.append(jax.ShapeDtypeStruct((batch * seq, RW_W), F32))
    bf = lambda: pltpu.VMEM((tb, RW_W), BF16)
    scratch = [pltpu.VMEM((RW_HEADS, RW_N, RW_N), F32), pltpu.VMEM((1, cw), F32),
               bf(), bf(), bf(), bf(), bf(), bf(), bf(),
               pltpu.VMEM((tb, RW_W), F32), pltpu.VMEM((tb, RW_W), F32)]
    outs = pl.pallas_call(
        functools.partial(_rwkv_kernel, tb=tb, has_vres=has_vres),
        grid=(batch, nb),
        in_specs=in_specs, out_specs=out_specs, out_shape=out_shape,
        scratch_shapes=scratch,
        compiler_params=pltpu.CompilerParams(
            dimension_semantics=("parallel", "arbitrary"), vmem_limit_bytes=VMEM_LIMIT),
        name="rwkv7",
    )(*ins)
    return outs if not has_vres else (outs[0], None)


def _outproj_kernel(x_ref, om_ref, od_ref, orw_ref, w_ref, fg_ref, o_ref, *, final):
    acc = (_dot(om_ref[...], w_ref[0:MLA_W, :])
           + _dot(od_ref[...], w_ref[MLA_W:MLA_W + DIFF_W, :])
           + _dot(orw_ref[...], w_ref[MLA_W + DIFF_W:, :]))
    y = x_ref[...] + acc
    if final:
        y = _rms(y, fg_ref[...], NORM_EPS)
    o_ref[...] = y


def _outproj(x2, om, od, orw, w, fg, *, tm, final):
    t = x2.shape[0]
    row = lambda i: (i, 0)
    const = lambda i: (0, 0)
    return pl.pallas_call(
        functools.partial(_outproj_kernel, final=final),
        grid=(t // tm,),
        in_specs=[pl.BlockSpec((tm, D_MODEL), row), pl.BlockSpec((tm, MLA_W), row),
                  pl.BlockSpec((tm, DIFF_W), row), pl.BlockSpec((tm, RW_W), row),
                  pl.BlockSpec(w.shape, const), pl.BlockSpec(fg.shape, const)],
        out_specs=pl.BlockSpec((tm, D_MODEL), row),
        out_shape=jax.ShapeDtypeStruct((t, D_MODEL), F32),
        compiler_params=pltpu.CompilerParams(
            dimension_semantics=("parallel",), vmem_limit_bytes=VMEM_LIMIT),
        name="outproj",
    )(x2, om, od, orw, w, fg)


def _pack_inproj(w_in, w_vres):
    d = w_in.shape[0]
    o = 0
    cq = w_in[:, o:o + MLA_Q_RANK]; o += MLA_Q_RANK
    ckv = w_in[:, o:o + MLA_KV_RANK]; o += MLA_KV_RANK
    kpe = w_in[:, o:o + MLA_ROPE]; o += MLA_ROPE
    dqkv = w_in[:, o:o + DIFF_G]; o += DIFF_G
    gate = w_in[:, o:o + D_MIX]; o += D_MIX
    rw = w_in[:, o:]
    half = MLA_ROPE // 2
    z = lambda n: jnp.zeros((d, n), w_in.dtype)
    kp = jnp.concatenate([z(MLA_NOPE), kpe, z(HP - MLA_QK)], axis=1)
    kps = jnp.concatenate([z(MLA_NOPE), kpe[:, half:], kpe[:, :half], z(HP - MLA_QK)], axis=1)
    g_mla, g_diff, g_rw = gate[:, :MLA_W], gate[:, MLA_W:MLA_W + DIFF_W], gate[:, MLA_W + DIFF_W:]
    cols = [cq, ckv, kp, kps, dqkv, g_rw, g_mla, g_diff, rw]
    if w_vres is not None:
        cols += [w_vres, z(LANE - RW_MV_RANK)]
    return jnp.concatenate(cols, axis=1).astype(BF16)


def _pack_mla(w_uq, w_ukv):
    rq, rkv = w_uq.shape[0], w_ukv.shape[0]
    half = MLA_ROPE // 2
    q3 = w_uq.reshape(rq, MLA_HEADS, MLA_QK)
    zq = lambda n: jnp.zeros((rq, MLA_HEADS, n), w_uq.dtype)
    wq = jnp.concatenate([q3, zq(HP - MLA_QK)], axis=2)
    wqs = jnp.concatenate([zq(MLA_NOPE), q3[:, :, MLA_NOPE + half:], q3[:, :, MLA_NOPE:MLA_NOPE + half],
                           zq(HP - MLA_QK)], axis=2)
    kv3 = w_ukv.reshape(rkv, MLA_HEADS, MLA_NOPE + MLA_V)
    wk = jnp.concatenate([kv3[:, :, :MLA_NOPE], jnp.zeros((rkv, MLA_HEADS, HP - MLA_NOPE), w_ukv.dtype)], axis=2)
    wv = kv3[:, :, MLA_NOPE:]
    flat = lambda a: a.reshape(a.shape[0], -1).astype(BF16)
    return flat(wq), flat(wqs), flat(wk), flat(wv)


def _pad_rows(w, top, total):
    return jnp.concatenate([jnp.zeros((top, w.shape[1]), w.dtype), w,
                            jnp.zeros((total - top - w.shape[0], w.shape[1]), w.dtype)], axis=0).astype(BF16)


def _rwkv_consts(tb):
    t = jnp.arange(tb)
    same = (t[:, None] // RW_CHUNK) == (t[None, :] // RW_CHUNK)
    tri = (same & (t[None, :] <= t[:, None])).astype(BF16)
    blk = same.astype(BF16)
    hl = jnp.arange(RW_W) // RW_N
    seg = (hl[:, None] == hl[None, :]).astype(BF16)
    return tri, blk, seg


def kernel(x, positions, pre_g, w_in, w_in_vres, w_out, mla_gq, mla_gkv, mla_wuq, mla_wukv, diff_lam, diff_gsub, rw_mu, rw_mu_vres, rw_w0, rw_w2, rw_a0, rw_a2, rw_v0, rw_v2, rw_kk, rw_ka, rw_rk, rw_lnw, rw_lnb, final_g):
    batch, seq, _ = x.shape
    depth = pre_g.shape[0]
    tm = min(512, seq)
    tq = min(256, seq)
    tb = min(256, seq)
    assert seq % tm == 0 and seq % tq == 0 and seq % tb == 0 and tb % RW_CHUNK == 0

    x2 = x.reshape(batch * seq, D_MODEL)
    tab = _rope_table(positions)
    pos_f = positions.astype(F32)
    posc = pos_f.reshape(seq, 1)
    posr = pos_f.reshape(seq // tq, 1, tq)
    consts = _rwkv_consts(tb)
    r1 = lambda a: a.reshape(1, -1).astype(F32)

    vfirst = None
    for layer in range(depth):
        vres = layer > 0
        w = _pack_inproj(w_in[layer], w_in_vres[layer - 1] if vres else None)
        wq, wqs, wk, wv = _pack_mla(mla_wuq[layer], mla_wukv[layer])
        q, k, v, d, gate, rw = _inproj(x2, r1(pre_g[layer]), w, tab, r1(mla_gq[layer]), r1(mla_gkv[layer]),
                                       wq, wqs, wk, wv, seq=seq, tm=tm)
        o_mla = _mla_flash(q, k, v, gate, batch=batch, seq=seq, tq=tq)
        lam_init = 0.8 - 0.6 * math.exp(-0.3 * (layer + 1))
        gsub2 = jnp.concatenate([diff_gsub[layer], diff_gsub[layer]]).reshape(1, LANE).astype(F32)
        o_diff = _diff_flash(d, gate, posc, posr, diff_lam[layer].astype(F32), gsub2,
                             batch=batch, seq=seq, tq=tq, lam_init=lam_init)
        mu = rw_mu[layer]
        if vres:
            mu = jnp.concatenate([mu, rw_mu_vres[layer - 1], jnp.zeros((LANE - RW_MV_RANK,), mu.dtype)])
        params = [r1(mu), r1(rw_w0[layer]), _pad_rows(rw_w2[layer], 0, LANE), r1(rw_a0[layer]),
                  _pad_rows(rw_a2[layer], RW_DECAY_RANK, LANE)]
        if vres:
            params += [r1(rw_v0[layer - 1]), _pad_rows(rw_v2[layer - 1], 0, LANE)]
        params += [r1(rw_kk[layer]), r1(rw_ka[layer]), r1(rw_rk[layer]), r1(rw_lnw[layer]), r1(rw_lnb[layer])]
        o_rw, vf = _rwkv(rw, vfirst, gate, consts, params, batch=batch, seq=seq, tb=tb)
        if not vres:
            vfirst = vf
        x2 = _outproj(x2, o_mla, o_diff, o_rw, w_out[layer].astype(BF16), r1(final_g),
                      tm=tm, final=(layer == depth - 1))
    return x2.reshape(batch, seq, D_MODEL)
```

```python
import functools
import math

import jax
import jax.numpy as jnp
from jax import lax
from jax.experimental import pallas as pl
from jax.experimental.pallas import tpu as pltpu

F32 = jnp.float32
BF16 = jnp.bfloat16

D_MODEL = 1024
D_MIX = 1024
NORM_EPS = 1e-6
MLA_HEADS, MLA_NOPE, MLA_ROPE, MLA_V = 6, 64, 32, 64
MLA_Q_RANK, MLA_KV_RANK = 256, 128
MLA_QK = MLA_NOPE + MLA_ROPE
ROPE_THETA = 10000.0
MLA_W = MLA_HEADS * MLA_V
DIFF_HEADS, DIFF_DH = 4, 32
DIFF_W = DIFF_HEADS * 2 * DIFF_DH
DIFF_SUBLN_EPS = 1e-5
RW_HEADS, RW_N = 6, 64
RW_W = RW_HEADS * RW_N
RW_DECAY_RANK, RW_AAA_RANK, RW_MV_RANK = 64, 64, 32
RW_GN_EPS = 64e-5
RW_SHIFT_BASE = 3 * RW_W + RW_DECAY_RANK + RW_AAA_RANK

LANE = 128
HP = LANE
RW_CHUNK = 64

MLA_G = MLA_Q_RANK + MLA_KV_RANK + 2 * HP
DIFF_G = 3 * DIFF_W
GATE_G = D_MIX
RW_G0 = RW_SHIFT_BASE
RW_G1 = RW_SHIFT_BASE + LANE
OFF_DIFF = MLA_G
OFF_GATE = OFF_DIFF + DIFF_G
OFF_RW = OFF_GATE + GATE_G

VMEM_LIMIT = 52 * 1024 * 1024

_NT = (((1,), (1,)), ((), ()))
_TN = (((0,), (0,)), ((), ()))


def _dot(a, b):
    return jnp.dot(a, b, preferred_element_type=F32)


def _dot_nt(a, b):
    return lax.dot_general(a, b, _NT, preferred_element_type=F32)


def _split_hi_lo(x):
    hi = x.astype(BF16)
    return hi, (x - hi.astype(F32)).astype(BF16)


def _sigmoid(z):
    return 1.0 / (1.0 + jnp.exp(-z))


def _silu(g):
    return g * _sigmoid(g)


def _rms(x, g, eps):
    return x * lax.rsqrt(jnp.mean(x * x, axis=-1, keepdims=True) + eps) * g


def _rope_table_kernel(pos_ref, tab_ref):
    pos = pos_ref[...]
    lane = lax.broadcasted_iota(jnp.int32, (1, LANE), 1)
    half = MLA_ROPE // 2
    in_rope = (lane >= MLA_NOPE) & (lane < MLA_QK)
    first = lane < MLA_NOPE + half
    idx = jnp.where(first, lane - MLA_NOPE, lane - MLA_NOPE - half).astype(F32)
    inv = jnp.exp(idx * (-math.log(ROPE_THETA) / half))
    ang = pos * inv
    cos = jnp.where(in_rope, jnp.cos(ang), jnp.where(lane < MLA_NOPE, 1.0, 0.0))
    sin = jnp.sin(ang)
    sin = jnp.where(in_rope, jnp.where(first, -sin, sin), 0.0)
    scale = MLA_QK ** -0.5
    tab_ref[:, 0 * LANE:1 * LANE] = cos * scale
    tab_ref[:, 1 * LANE:2 * LANE] = sin * scale
    tab_ref[:, 2 * LANE:3 * LANE] = cos
    tab_ref[:, 3 * LANE:4 * LANE] = sin


def _rope_table(positions):
    s = positions.shape[0]
    return pl.pallas_call(
        _rope_table_kernel,
        out_shape=jax.ShapeDtypeStruct((s, 4 * LANE), F32),
        name="rope_table",
    )(positions.astype(F32).reshape(s, 1))


def _inproj_kernel(x_ref, g_ref, w_ref, tab_ref, gq_ref, gkv_ref, wq_ref, wqs_ref, wk_ref, wv_ref,
                   q_ref, k_ref, v_ref, d_ref, gate_ref, rw_ref):
    x = x_ref[...]
    h = _rms(x, g_ref[...], NORM_EPS).astype(BF16)

    mla = _dot(h, w_ref[:, 0:MLA_G])
    cqn = _rms(mla[:, 0:MLA_Q_RANK], gq_ref[...], NORM_EPS).astype(BF16)
    ckvn = _rms(mla[:, MLA_Q_RANK:MLA_Q_RANK + MLA_KV_RANK], gkv_ref[...], NORM_EPS).astype(BF16)
    o_kpe = MLA_Q_RANK + MLA_KV_RANK
    tab = tab_ref[...]
    cos_q, sin_q = tab[:, 0:LANE], tab[:, LANE:2 * LANE]
    cos_k, sin_k = tab[:, 2 * LANE:3 * LANE], tab[:, 3 * LANE:4 * LANE]
    kpe = mla[:, o_kpe:o_kpe + HP] * cos_k + mla[:, o_kpe + HP:o_kpe + 2 * HP] * sin_k
    qa = _dot(cqn, wq_ref[...])
    qs = _dot(cqn, wqs_ref[...])
    kn = _dot(ckvn, wk_ref[...])
    for hd in range(MLA_HEADS):
        sl = slice(hd * HP, (hd + 1) * HP)
        q_ref[:, sl] = (qa[:, sl] * cos_q + qs[:, sl] * sin_q).astype(BF16)
        k_ref[:, sl] = (kn[:, sl] + kpe).astype(BF16)
    v_ref[...] = _dot(ckvn, wv_ref[...]).astype(BF16)

    dd = _dot(h, w_ref[:, OFF_DIFF:OFF_GATE])
    d_ref[:, 0:DIFF_W] = (dd[:, 0:DIFF_W] * (DIFF_DH ** -0.5)).astype(BF16)
    d_ref[:, DIFF_W:DIFF_G] = dd[:, DIFF_W:DIFF_G].astype(BF16)
    gate_ref[...] = _dot(h, w_ref[:, OFF_GATE:OFF_RW]).astype(BF16)
    rw_ref[...] = _dot(h, w_ref[:, OFF_RW:])


def _inproj(x2, g, w, tab, gq, gkv, wq, wqs, wk, wv, *, seq, tm):
    t = x2.shape[0]
    ctot = w.shape[1]
    rw_g = ctot - OFF_RW
    n_seq_tiles = seq // tm
    row = lambda i: (i, 0)
    const = lambda i: (0, 0)
    full = lambda a: pl.BlockSpec(a.shape, const)
    return pl.pallas_call(
        _inproj_kernel,
        grid=(t // tm,),
        in_specs=[
            pl.BlockSpec((tm, D_MODEL), row),
            full(g), full(w),
            pl.BlockSpec((tm, 4 * LANE), lambda i: (i % n_seq_tiles, 0)),
            full(gq), full(gkv), full(wq), full(wqs), full(wk), full(wv),
        ],
        out_specs=[
            pl.BlockSpec((tm, MLA_HEADS * HP), row),
            pl.BlockSpec((tm, MLA_HEADS * HP), row),
            pl.BlockSpec((tm, MLA_W), row),
            pl.BlockSpec((tm, DIFF_G), row),
            pl.BlockSpec((tm, GATE_G), row),
            pl.BlockSpec((tm, rw_g), row),
        ],
        out_shape=[
            jax.ShapeDtypeStruct((t, MLA_HEADS * HP), BF16),
            jax.ShapeDtypeStruct((t, MLA_HEADS * HP), BF16),
            jax.ShapeDtypeStruct((t, MLA_W), BF16),
            jax.ShapeDtypeStruct((t, DIFF_G), BF16),
            jax.ShapeDtypeStruct((t, GATE_G), BF16),
            jax.ShapeDtypeStruct((t, rw_g), F32),
        ],
        compiler_params=pltpu.CompilerParams(
            dimension_semantics=("parallel",), vmem_limit_bytes=VMEM_LIMIT),
        name="inproj",
    )(x2, g, w, tab, gq, gkv, wq, wqs, wk, wv)


def _softmax_step(s, m, l):
    m_new = jnp.maximum(m, jnp.max(s, axis=-1, keepdims=True))
    alpha = jnp.exp(m - m_new)
    p = jnp.exp(s - m_new)
    return m_new, alpha, p, alpha * l + jnp.sum(p, axis=-1, keepdims=True)


def _mla_flash_kernel(q_ref, k_ref, v_ref, gate_ref, o_ref, *, tq):
    i = pl.program_id(2)
    tri = (lax.broadcasted_iota(jnp.int32, (tq, tq), 1)
           <= lax.broadcasted_iota(jnp.int32, (tq, tq), 0))
    outs = []
    for hh in range(2):
        hs = slice(hh * HP, (hh + 1) * HP)
        q = q_ref[:, hs]

        def tile(j, carry, masked, q=q, hs=hs):
            m, l, acc = carry
            rows = pl.ds(pl.multiple_of(j * tq, tq), tq)
            s = _dot_nt(q, k_ref[rows, hs])
            if masked:
                s = jnp.where(tri, s, -jnp.inf)
            m, alpha, p, l = _softmax_step(s, m, l)
            return m, l, alpha * acc + _dot(p.astype(BF16), v_ref[rows, :])

        init = (jnp.full((tq, 1), -jnp.inf, F32), jnp.zeros((tq, 1), F32), jnp.zeros((tq, LANE), F32))
        carry = lax.fori_loop(0, i, functools.partial(tile, masked=False), init)
        _, l, acc = tile(i, carry, True)
        outs.append(acc / l)
    lane = lax.broadcasted_iota(jnp.int32, (tq, LANE), 1)
    o = jnp.where(lane < MLA_V, outs[0], outs[1])
    o_ref[...] = (o * _silu(gate_ref[...].astype(F32))).astype(BF16)


def _mla_flash(q, k, v, gate, *, batch, seq, tq):
    nq = seq // tq
    gate_blk0 = RW_W // LANE
    return pl.pallas_call(
        functools.partial(_mla_flash_kernel, tq=tq),
        grid=(batch, MLA_HEADS // 2, nq),
        in_specs=[
            pl.BlockSpec((tq, 2 * HP), lambda b, p, i: (b * nq + i, p)),
            pl.BlockSpec((seq, 2 * HP), lambda b, p, i: (b, p)),
            pl.BlockSpec((seq, LANE), lambda b, p, i: (b, p)),
            pl.BlockSpec((tq, LANE), lambda b, p, i: (b * nq + i, gate_blk0 + p)),
        ],
        out_specs=pl.BlockSpec((tq, LANE), lambda b, p, i: (b * nq + i, p)),
        out_shape=jax.ShapeDtypeStruct((batch * seq, MLA_W), BF16),
        compiler_params=pltpu.CompilerParams(
            dimension_semantics=("parallel", "parallel", "parallel"), vmem_limit_bytes=VMEM_LIMIT),
        name="mla_flash",
    )(q, k, v, gate)


def _diff_flash_kernel(posc_ref, posr_ref, lam_ref, gsub_ref, q_ref, k_ref, v_ref, gate_ref, o_ref, qm_ref,
                       *, tq, lam_init):
    i = pl.program_id(1)
    n_maps = 2 * DIFF_HEADS
    q = q_ref[...]
    lane_q = lax.broadcasted_iota(jnp.int32, (tq, DIFF_W), 1)
    for hm in range(n_maps):
        own = (lane_q >= hm * DIFF_DH) & (lane_q < (hm + 1) * DIFF_DH)
        qm_ref[hm * tq:(hm + 1) * tq, :] = jnp.where(own, q, jnp.zeros_like(q))
    pos_q = posc_ref[...]
    tri = (lax.broadcasted_iota(jnp.int32, (tq, tq), 1)
           <= lax.broadcasted_iota(jnp.int32, (tq, tq), 0))
    slopes = [2.0 ** (-8.0 * (h + 1) / DIFF_HEADS) for h in range(DIFF_HEADS)]

    def tile(j, carry, masked):
        rows = pl.ds(pl.multiple_of(j * tq, tq), tq)
        s_all = _dot_nt(qm_ref[...], k_ref[rows, :])
        dist = jnp.abs(pos_q - posr_ref[j])
        new = []
        for hm in range(n_maps):
            h = hm // 2
            m, l, acc = carry[hm]
            s = s_all[hm * tq:(hm + 1) * tq] - slopes[h] * dist
            if masked:
                s = jnp.where(tri, s, -jnp.inf)
            m, alpha, p, l = _softmax_step(s, m, l)
            vt = v_ref[rows, (h // 2) * LANE:(h // 2 + 1) * LANE]
            new.append((m, l, alpha * acc + _dot(p.astype(BF16), vt)))
        return tuple(new)

    one = (jnp.full((tq, 1), -jnp.inf, F32), jnp.zeros((tq, 1), F32), jnp.zeros((tq, LANE), F32))
    carry = lax.fori_loop(0, i, functools.partial(tile, masked=False), (one,) * n_maps)
    carry = tile(i, carry, True)

    lam = lam_ref[...]
    lam_full = (jnp.exp(jnp.sum(lam[0:1] * lam[1:2], axis=-1, keepdims=True))
                - jnp.exp(jnp.sum(lam[2:3] * lam[3:4], axis=-1, keepdims=True)) + lam_init)
    lane = lax.broadcasted_iota(jnp.int32, (tq, LANE), 1)
    low = lane < 2 * DIFF_DH
    heads = []
    for h in range(DIFF_HEADS):
        (_, l0, a0), (_, l1, a1) = carry[2 * h], carry[2 * h + 1]
        o = a0 / l0 - lam_full * (a1 / l1)
        valid = low if h % 2 == 0 else jnp.logical_not(low)
        ms = jnp.sum(jnp.where(valid, o * o, 0.0), axis=-1, keepdims=True) * (1.0 / (2 * DIFF_DH))
        heads.append(o * lax.rsqrt(ms + DIFF_SUBLN_EPS) * gsub_ref[...] * (1.0 - lam_init))
    for pr in range(DIFF_HEADS // 2):
        o = jnp.where(low, heads[2 * pr], heads[2 * pr + 1])
        cs = slice(pr * LANE, (pr + 1) * LANE)
        o_ref[:, cs] = (o * _silu(gate_ref[:, cs].astype(F32))).astype(BF16)


def _diff_flash(d, gate, posc, posr, lam, gsub2, *, batch, seq, tq, lam_init):
    nq = seq // tq
    gate_blk = (RW_W + MLA_W) // DIFF_W
    return pl.pallas_call(
        functools.partial(_diff_flash_kernel, tq=tq, lam_init=lam_init),
        grid=(batch, nq),
        in_specs=[
            pl.BlockSpec((tq, 1), lambda b, i: (i, 0)),
            pl.BlockSpec(posr.shape, lambda b, i: (0, 0, 0)),
            pl.BlockSpec(lam.shape, lambda b, i: (0, 0)),
            pl.BlockSpec(gsub2.shape, lambda b, i: (0, 0)),
            pl.BlockSpec((tq, DIFF_W), lambda b, i: (b * nq + i, 0)),
            pl.BlockSpec((seq, DIFF_W), lambda b, i: (b, 1)),
            pl.BlockSpec((seq, DIFF_W), lambda b, i: (b, 2)),
            pl.BlockSpec((tq, DIFF_W), lambda b, i: (b * nq + i, gate_blk)),
        ],
        out_specs=pl.BlockSpec((tq, DIFF_W), lambda b, i: (b * nq + i, 0)),
        out_shape=jax.ShapeDtypeStruct((batch * seq, DIFF_W), BF16),
        scratch_shapes=[pltpu.VMEM((2 * DIFF_HEADS * tq, DIFF_W), BF16)],
        compiler_params=pltpu.CompilerParams(
            dimension_semantics=("parallel", "parallel"), vmem_limit_bytes=VMEM_LIMIT),
        name="diff_flash",
    )(posc, posr, lam, gsub2, d, d, d, gate)


def _rwkv_kernel(*refs, tb, has_vres, cpi):
    n_sc = 16
    if has_vres:
        (p_ref, vf_ref, gate_ref, tri_ref, blk_ref, seg_ref, mu_ref, w0_ref, w2_ref, a0_ref, a2_ref,
         v0_ref, v2_ref, kk_ref, ka_ref, rk_ref, lnw_ref, lnb_ref, y_ref) = refs[:-n_sc]
    else:
        (p_ref, gate_ref, tri_ref, blk_ref, seg_ref, mu_ref, w0_ref, w2_ref, a0_ref, a2_ref,
         kk_ref, ka_ref, rk_ref, lnw_ref, lnb_ref, y_ref, vfo_ref) = refs[:-n_sc]
    (state_ref, carry_ref, at_ref, rt_ref, bt_ref, kt_ref, vv_ref, bw_ref, kw_ref, tot_ref, yy_ref,
     lhs_ref, xav_ref, ay_ref, zt_ref, wlt_ref) = refs[-n_sc:]

    @pl.when(pl.program_id(1) == 0)
    def _():
        state_ref[...] = jnp.zeros_like(state_ref)
        carry_ref[...] = jnp.zeros_like(carry_ref)

    seg = seg_ref[...]

    def segsum(x):
        hi, lo = _split_hi_lo(x)
        return _dot(hi, seg) + _dot(lo, seg)

    p = p_ref[...]
    row = lax.broadcasted_iota(jnp.int32, (tb, 1), 0)
    prev = jnp.where(row == 0, carry_ref[...], pltpu.roll(p, 1, 0))
    carry_ref[...] = p[tb - 1:tb, :]
    xs = p + (prev - p) * mu_ref[...]
    r = xs[:, 0:RW_W]
    k = xs[:, RW_W:2 * RW_W]
    v = xs[:, 2 * RW_W:3 * RW_W]
    hwa = xs[:, 3 * RW_W:3 * RW_W + LANE]
    z = -(w0_ref[...] + _dot(jnp.tanh(hwa).astype(BF16), w2_ref[...]))
    softplus = jnp.maximum(z, 0.0) + jnp.log(1.0 + jnp.exp(-jnp.abs(z)))
    lw = -jnp.exp(-softplus - 0.5)
    a = _sigmoid(a0_ref[...] + _dot(hwa.astype(BF16), a2_ref[...]))
    if has_vres:
        hv = xs[:, RW_SHIFT_BASE:RW_SHIFT_BASE + LANE]
        v = v + (vf_ref[...] - v) * _sigmoid(v0_ref[...] + _dot(hv.astype(BF16), v2_ref[...]))
    else:
        vfo_ref[...] = v
    kk = k * kk_ref[...]
    kk = kk / jnp.maximum(jnp.sqrt(segsum(kk * kk)), 1e-12)
    k = k * (1.0 + (a - 1.0) * ka_ref[...])

    lw_hi, lw_lo = _split_hi_lo(lw)
    cum = _dot(tri_ref[...], lw_hi) + _dot(tri_ref[...], lw_lo)
    tot = _dot(blk_ref[...], lw_hi) + _dot(blk_ref[...], lw_lo)
    w_inv = jnp.exp(-cum)
    w_end = jnp.exp(tot - cum)
    kka = kk * a
    rt_ref[...] = (r * jnp.exp(cum)).astype(BF16)
    at_ref[...] = (-kk * jnp.exp(cum - lw)).astype(BF16)
    bt_ref[...] = (kka * w_inv).astype(BF16)
    kt_ref[...] = (k * w_inv).astype(BF16)
    vv_ref[...] = v.astype(BF16)
    bw_ref[...] = kka * w_end
    kw_ref[...] = k * w_end
    tot_ref[...] = tot

    c = RW_CHUNK
    n_pairs = RW_HEADS // 2
    lane_c = lax.broadcasted_iota(jnp.int32, (c, LANE), 1)
    even_c = lane_c < RW_N
    even_2c = lax.broadcasted_iota(jnp.int32, (2 * c, LANE), 1) < RW_N
    ri = lax.broadcasted_iota(jnp.int32, (2 * c, 2 * c), 0)
    ci = lax.broadcasted_iota(jnp.int32, (2 * c, 2 * c), 1)
    cm = jnp.where(ci >= c, ci - c, ci)
    keep = ((ri >= c) & (cm <= ri - c)) | ((ri < c) & (cm < ri))
    same_head = (ri < c) == (ci < c)
    eye2 = (lax.broadcasted_iota(jnp.int32, (c, LANE), 0)
            == jnp.where(lane_c >= c, lane_c - c, lane_c)).astype(F32)

    def bdiag(m):
        zero = jnp.zeros_like(m)
        return jnp.concatenate([jnp.where(even_c, m, zero), jnp.where(even_c, zero, m)], axis=0)

    def adiag(m):
        zero = jnp.zeros_like(m)
        return jnp.concatenate([jnp.where(even_c, zero, m), jnp.where(even_c, m, zero)], axis=0)

    def phase_a(i, _):
        probs = [(i * cpi + cc, pr) for cc in range(cpi) for pr in range(n_pairs)]
        rows = [pl.ds(pl.multiple_of(ic * c, c), c) for ic, _ in probs]
        rows2 = [pl.ds(pl.multiple_of(ic * 2 * c, 2 * c), 2 * c) for ic, _ in probs]
        ls = [slice(pr * LANE, (pr + 1) * LANE) for _, pr in probs]
        rng = range(len(probs))
        at = [at_ref[rows[j], ls[j]] for j in rng]
        rt = [rt_ref[rows[j], ls[j]] for j in rng]
        bt = [bt_ref[rows[j], ls[j]] for j in rng]
        kt = [kt_ref[rows[j], ls[j]] for j in rng]
        vh = [vv_ref[rows[j], ls[j]] for j in rng]
        xar = [jnp.concatenate([at[j], rt[j]], axis=0) for j in rng]
        zero_b = jnp.zeros((2 * c, LANE), BF16)
        aa_e = [jnp.where(keep, _dot_nt(jnp.where(even_2c, xar[j], zero_b),
                                        jnp.concatenate([bt[j], kt[j]], axis=0)), 0.0) for j in rng]
        aa_o = [jnp.where(keep, _dot_nt(jnp.where(even_2c, zero_b, xar[j]),
                                        jnp.concatenate([kt[j], bt[j]], axis=0)), 0.0) for j in rng]
        nn = [jnp.where(even_c, aa_e[j][0:c], aa_o[j][0:c]) for j in rng]
        aak = [jnp.where(even_c, aa_o[j][0:c], aa_e[j][0:c]) for j in rng]
        for j in rng:
            pr = probs[j][1]
            ay_ref[rows[j], 2 * pr * LANE:(2 * pr + 1) * LANE] = jnp.where(
                even_c, aa_e[j][c:2 * c], aa_o[j][c:2 * c]).astype(BF16)
            ay_ref[rows[j], (2 * pr + 1) * LANE:(2 * pr + 2) * LANE] = jnp.where(
                even_c, aa_o[j][c:2 * c], aa_e[j][c:2 * c]).astype(BF16)
        av = [_dot(aak[j].astype(BF16), adiag(vh[j])) for j in rng]
        x = [eye2 + nn[j] for j in rng]
        pw = [nn[j].astype(BF16) for j in rng]
        pw = [_dot(pw[j], bdiag(pw[j])).astype(BF16) for j in rng]
        n_sq = int(math.log2(c)) - 1
        for it in range(n_sq):
            last = it == n_sq - 1
            lhs = [x[j].astype(BF16) if last else jnp.concatenate([x[j].astype(BF16), pw[j]], axis=0) for j in rng]
            out = [_dot(lhs[j], bdiag(pw[j])) for j in rng]
            x = [x[j] + out[j][0:c] for j in rng]
            if not last:
                pw = [out[j][c:2 * c].astype(BF16) for j in rng]
        fin = [_dot(x[j].astype(BF16), jnp.concatenate([bdiag(at[j]), bdiag(av[j].astype(BF16))], axis=1))
               for j in rng]
        for j in rng:
            lhs_ref[rows2[j], ls[j]] = jnp.concatenate([fin[j][:, 0:LANE].astype(BF16), rt[j]], axis=0)
            xav_ref[rows[j], ls[j]] = fin[j][:, LANE:2 * LANE]
            zz = jnp.concatenate([bw_ref[rows[j], ls[j]], kw_ref[rows[j], ls[j]]], axis=0)
            zt_ref[rows2[j], ls[j]] = zz.T.astype(BF16)
            tot_row = tot_ref[pl.ds(probs[j][0] * c, 1), ls[j]]
            wlt_ref[rows2[j], ls[j]] = jnp.exp(jnp.broadcast_to(tot_row, (2 * c, LANE)).T)
        return 0

    lax.fori_loop(0, tb // (c * cpi), phase_a, 0)

    def phase_b(ic, _):
        rows = pl.ds(pl.multiple_of(ic * c, c), c)
        rows2 = pl.ds(pl.multiple_of(ic * 2 * c, 2 * c), 2 * c)
        ls = [slice(pr * LANE, (pr + 1) * LANE) for pr in range(n_pairs)]
        rng = range(n_pairs)
        mt = [state_ref[pr] for pr in rng]
        out = [_dot(lhs_ref[rows2, ls[pr]], mt[pr].astype(BF16)) for pr in rng]
        ub = [(out[pr][0:c] + xav_ref[rows, ls[pr]]).astype(BF16) for pr in rng]
        vh = [vv_ref[rows, ls[pr]] for pr in rng]
        ft = [_dot(zt_ref[rows2, ls[pr]], jnp.concatenate([ub[pr], vh[pr]], axis=0)) for pr in rng]
        for pr in rng:
            state_ref[pr] = mt[pr] * wlt_ref[rows2, ls[pr]] + jnp.where(same_head, ft[pr], 0.0)
        for pr in rng:
            uv = jnp.concatenate([bdiag(ub[pr]), adiag(vh[pr])], axis=0)
            yy_ref[rows, ls[pr]] = out[pr][c:2 * c] + _dot(ay_ref[rows, 2 * pr * LANE:(2 * pr + 2) * LANE], uv)
        return 0

    lax.fori_loop(0, tb // c, phase_b, 0)

    y = yy_ref[...]
    inv_n = 1.0 / RW_N
    dlt = y - segsum(y) * inv_n
    var = segsum(dlt * dlt) * inv_n
    y = dlt * lax.rsqrt(var + RW_GN_EPS) * lnw_ref[...] + lnb_ref[...]
    y = y + segsum(r * k * rk_ref[...]) * v
    y_ref[...] = (y * _silu(gate_ref[...].astype(F32))).astype(BF16)


def _rwkv(p, vfirst, gate, consts, params, *, batch, seq, tb, cpi):
    has_vres = vfirst is not None
    nb = seq // tb
    cw = p.shape[1]
    row = lambda b, i: (b * nb + i, 0)
    full = lambda a: pl.BlockSpec(a.shape, lambda b, i: (0,) * a.ndim)
    ins = [p] + ([vfirst] if has_vres else []) + [gate] + list(consts) + list(params)
    in_specs = ([pl.BlockSpec((tb, cw), row)]
                + ([pl.BlockSpec((tb, RW_W), row)] if has_vres else [])
                + [pl.BlockSpec((tb, RW_W), row)]
                + [full(a) for a in consts] + [full(a) for a in params])
    out_specs = [pl.BlockSpec((tb, RW_W), row)]
    out_shape = [jax.ShapeDtypeStruct((batch * seq, RW_W), BF16)]
    if not has_vres:
        out_specs.append(pl.BlockSpec((tb, RW_W), row))
        out_shape.append(jax.ShapeDtypeStruct((batch * seq, RW_W), F32))
    vm = lambda rows, cols, dt: pltpu.VMEM((rows, cols), dt)
    scratch = [pltpu.VMEM((RW_HEADS // 2, LANE, LANE), F32), vm(1, cw, F32),
               vm(tb, RW_W, BF16), vm(tb, RW_W, BF16), vm(tb, RW_W, BF16), vm(tb, RW_W, BF16), vm(tb, RW_W, BF16),
               vm(tb, RW_W, F32), vm(tb, RW_W, F32), vm(tb, RW_W, F32), vm(tb, RW_W, F32),
               vm(2 * tb, RW_W, BF16), vm(tb, RW_W, F32), vm(tb, 2 * RW_W, BF16),
               vm(2 * tb, RW_W, BF16), vm(2 * tb, RW_W, F32)]
    outs = pl.pallas_call(
        functools.partial(_rwkv_kernel, tb=tb, has_vres=has_vres, cpi=cpi),
        grid=(batch, nb),
        in_specs=in_specs, out_specs=out_specs, out_shape=out_shape,
        scratch_shapes=scratch,
        compiler_params=pltpu.CompilerParams(
            dimension_semantics=("parallel", "arbitrary"), vmem_limit_bytes=VMEM_LIMIT),
        name="rwkv7",
    )(*ins)
    return outs if not has_vres else (outs[0], None)


def _outproj_kernel(x_ref, om_ref, od_ref, orw_ref, w_ref, fg_ref, o_ref, *, final):
    acc = (_dot(om_ref[...], w_ref[0:MLA_W, :])
           + _dot(od_ref[...], w_ref[MLA_W:MLA_W + DIFF_W, :])
           + _dot(orw_ref[...], w_ref[MLA_W + DIFF_W:, :]))
    y = x_ref[...] + acc
    if final:
        y = _rms(y, fg_ref[...], NORM_EPS)
    o_ref[...] = y


def _outproj(x2, om, od, orw, w, fg, *, tm, final):
    t = x2.shape[0]
    row = lambda i: (i, 0)
    const = lambda i: (0, 0)
    return pl.pallas_call(
        functools.partial(_outproj_kernel, final=final),
        grid=(t // tm,),
        in_specs=[pl.BlockSpec((tm, D_MODEL), row), pl.BlockSpec((tm, MLA_W), row),
                  pl.BlockSpec((tm, DIFF_W), row), pl.BlockSpec((tm, RW_W), row),
                  pl.BlockSpec(w.shape, const), pl.BlockSpec(fg.shape, const)],
        out_specs=pl.BlockSpec((tm, D_MODEL), row),
        out_shape=jax.ShapeDtypeStruct((t, D_MODEL), F32),
        compiler_params=pltpu.CompilerParams(
            dimension_semantics=("parallel",), vmem_limit_bytes=VMEM_LIMIT),
        name="outproj",
    )(x2, om, od, orw, w, fg)


def _pack_inproj(w_in, w_vres):
    d = w_in.shape[0]
    o = 0
    cq = w_in[:, o:o + MLA_Q_RANK]; o += MLA_Q_RANK
    ckv = w_in[:, o:o + MLA_KV_RANK]; o += MLA_KV_RANK
    kpe = w_in[:, o:o + MLA_ROPE]; o += MLA_ROPE
    dqkv = w_in[:, o:o + DIFF_G]; o += DIFF_G
    gate = w_in[:, o:o + D_MIX]; o += D_MIX
    rw = w_in[:, o:]
    half = MLA_ROPE // 2
    z = lambda n: jnp.zeros((d, n), w_in.dtype)
    kp = jnp.concatenate([z(MLA_NOPE), kpe, z(HP - MLA_QK)], axis=1)
    kps = jnp.concatenate([z(MLA_NOPE), kpe[:, half:], kpe[:, :half], z(HP - MLA_QK)], axis=1)
    g_mla, g_diff, g_rw = gate[:, :MLA_W], gate[:, MLA_W:MLA_W + DIFF_W], gate[:, MLA_W + DIFF_W:]
    cols = [cq, ckv, kp, kps, dqkv, g_rw, g_mla, g_diff, rw]
    if w_vres is not None:
        cols += [w_vres, z(LANE - RW_MV_RANK)]
    return jnp.concatenate(cols, axis=1).astype(BF16)


def _pack_mla(w_uq, w_ukv):
    rq, rkv = w_uq.shape[0], w_ukv.shape[0]
    half = MLA_ROPE // 2
    q3 = w_uq.reshape(rq, MLA_HEADS, MLA_QK)
    zq = lambda n: jnp.zeros((rq, MLA_HEADS, n), w_uq.dtype)
    wq = jnp.concatenate([q3, zq(HP - MLA_QK)], axis=2)
    wqs = jnp.concatenate([zq(MLA_NOPE), q3[:, :, MLA_NOPE + half:], q3[:, :, MLA_NOPE:MLA_NOPE + half],
                           zq(HP - MLA_QK)], axis=2)
    kv3 = w_ukv.reshape(rkv, MLA_HEADS, MLA_NOPE + MLA_V)
    wk = jnp.concatenate([kv3[:, :, :MLA_NOPE], jnp.zeros((rkv, MLA_HEADS, HP - MLA_NOPE), w_ukv.dtype)], axis=2)
    wv = kv3[:, :, MLA_NOPE:]
    flat = lambda a: a.reshape(a.shape[0], -1).astype(BF16)
    return flat(wq), flat(wqs), flat(wk), flat(wv)


def _pad_rows(w, top, total):
    return jnp.concatenate([jnp.zeros((top, w.shape[1]), w.dtype), w,
                            jnp.zeros((total - top - w.shape[0], w.shape[1]), w.dtype)], axis=0).astype(BF16)


def _rwkv_consts(tb):
    t = jnp.arange(tb)
    same = (t[:, None] // RW_CHUNK) == (t[None, :] // RW_CHUNK)
    tri = (same & (t[None, :] <= t[:, None])).astype(BF16)
    blk = same.astype(BF16)
    hl = jnp.arange(RW_W) // RW_N
    seg = (hl[:, None] == hl[None, :]).astype(BF16)
    return tri, blk, seg


def kernel(x, positions, pre_g, w_in, w_in_vres, w_out, mla_gq, mla_gkv, mla_wuq, mla_wukv, diff_lam, diff_gsub, rw_mu, rw_mu_vres, rw_w0, rw_w2, rw_a0, rw_a2, rw_v0, rw_v2, rw_kk, rw_ka, rw_rk, rw_lnw, rw_lnb, final_g):
    batch, seq, _ = x.shape
    depth = pre_g.shape[0]
    tm = min(512, seq)
    tq = min(256, seq)
    tb = min(256, seq)
    assert seq % tm == 0 and seq % tq == 0 and seq % tb == 0 and tb % RW_CHUNK == 0

    x2 = x.reshape(batch * seq, D_MODEL)
    tab = _rope_table(positions)
    pos_f = positions.astype(F32)
    posc = pos_f.reshape(seq, 1)
    posr = pos_f.reshape(seq // tq, 1, tq)
    consts = _rwkv_consts(tb)
    r1 = lambda a: a.reshape(1, -1).astype(F32)

    vfirst = None
    for layer in range(depth):
        vres = layer > 0
        w = _pack_inproj(w_in[layer], w_in_vres[layer - 1] if vres else None)
        wq, wqs, wk, wv = _pack_mla(mla_wuq[layer], mla_wukv[layer])
        q, k, v, d, gate, rw = _inproj(x2, r1(pre_g[layer]), w, tab, r1(mla_gq[layer]), r1(mla_gkv[layer]),
                                       wq, wqs, wk, wv, seq=seq, tm=tm)
        o_mla = _mla_flash(q, k, v, gate, batch=batch, seq=seq, tq=tq)
        lam_init = 0.8 - 0.6 * math.exp(-0.3 * (layer + 1))
        gsub2 = jnp.concatenate([diff_gsub[layer], diff_gsub[layer]]).reshape(1, LANE).astype(F32)
        o_diff = _diff_flash(d, gate, posc, posr, diff_lam[layer].astype(F32), gsub2,
                             batch=batch, seq=seq, tq=tq, lam_init=lam_init)
        mu = rw_mu[layer]
        if vres:
            mu = jnp.concatenate([mu, rw_mu_vres[layer - 1], jnp.zeros((LANE - RW_MV_RANK,), mu.dtype)])
        params = [r1(mu), r1(rw_w0[layer]), _pad_rows(rw_w2[layer], 0, LANE), r1(rw_a0[layer]),
                  _pad_rows(rw_a2[layer], RW_DECAY_RANK, LANE)]
        if vres:
            params += [r1(rw_v0[layer - 1]), _pad_rows(rw_v2[layer - 1], 0, LANE)]
        params += [r1(rw_kk[layer]), r1(rw_ka[layer]), r1(rw_rk[layer]), r1(rw_lnw[layer]), r1(rw_lnb[layer])]
        o_rw, vf = _rwkv(rw, vfirst, gate, consts, params, batch=batch, seq=seq, tb=tb, cpi=2)
        if not vres:
            vfirst = vf
        x2 = _outproj(x2, o_mla, o_diff, o_rw, w_out[layer].astype(BF16), r1(final_g),
                      tm=tm, final=(layer == depth - 1))
    return x2.reshape(batch, seq, D_MODEL)
```

```python
import functools
import math

import jax
import jax.numpy as jnp
from jax import lax
from jax.experimental import pallas as pl
from jax.experimental.pallas import tpu as pltpu

F32 = jnp.float32
BF16 = jnp.bfloat16

D_MODEL = 1024
D_MIX = 1024
NORM_EPS = 1e-6
MLA_HEADS, MLA_NOPE, MLA_ROPE, MLA_V = 6, 64, 32, 64
MLA_Q_RANK, MLA_KV_RANK = 256, 128
MLA_QK = MLA_NOPE + MLA_ROPE
ROPE_THETA = 10000.0
MLA_W = MLA_HEADS * MLA_V
DIFF_HEADS, DIFF_DH = 4, 32
DIFF_W = DIFF_HEADS * 2 * DIFF_DH
DIFF_SUBLN_EPS = 1e-5
RW_HEADS, RW_N = 6, 64
RW_W = RW_HEADS * RW_N
RW_DECAY_RANK, RW_AAA_RANK, RW_MV_RANK = 64, 64, 32
RW_GN_EPS = 64e-5
RW_SHIFT_BASE = 3 * RW_W + RW_DECAY_RANK + RW_AAA_RANK

LANE = 128
LOG2E = math.log2(math.e)
HP = LANE
RW_CHUNK = 64

MLA_G = MLA_Q_RANK + MLA_KV_RANK + 2 * HP
DIFF_G = 2 * DIFF_W
GATE_G = D_MIX
RW_G0 = RW_SHIFT_BASE
RW_G1 = RW_SHIFT_BASE + LANE
OFF_DIFF = MLA_G
OFF_GATE = OFF_DIFF + DIFF_G
OFF_RW = OFF_GATE + GATE_G

VMEM_LIMIT = 52 * 1024 * 1024

_NT = (((1,), (1,)), ((), ()))
_TN = (((0,), (0,)), ((), ()))


def _dot(a, b):
    return jnp.dot(a, b, preferred_element_type=F32)


def _dot_nt(a, b):
    return lax.dot_general(a, b, _NT, preferred_element_type=F32)


def _split_hi_lo(x):
    hi = x.astype(BF16)
    return hi, (x - hi.astype(F32)).astype(BF16)


def _sigmoid(z):
    return 1.0 / (1.0 + jnp.exp(-z))


def _silu(g):
    return g * _sigmoid(g)


def _rms(x, g, eps):
    return x * lax.rsqrt(jnp.mean(x * x, axis=-1, keepdims=True) + eps) * g


def _rope_table_kernel(pos_ref, tab_ref):
    pos = pos_ref[...]
    lane = lax.broadcasted_iota(jnp.int32, (1, LANE), 1)
    half = MLA_ROPE // 2
    in_rope = (lane >= MLA_NOPE) & (lane < MLA_QK)
    first = lane < MLA_NOPE + half
    idx = jnp.where(first, lane - MLA_NOPE, lane - MLA_NOPE - half).astype(F32)
    inv = jnp.exp(idx * (-math.log(ROPE_THETA) / half))
    ang = pos * inv
    cos = jnp.where(in_rope, jnp.cos(ang), jnp.where(lane < MLA_NOPE, 1.0, 0.0))
    sin = jnp.sin(ang)
    sin = jnp.where(in_rope, jnp.where(first, -sin, sin), 0.0)
    scale = MLA_QK ** -0.5 * LOG2E
    tab_ref[:, 0 * LANE:1 * LANE] = cos * scale
    tab_ref[:, 1 * LANE:2 * LANE] = sin * scale
    tab_ref[:, 2 * LANE:3 * LANE] = cos
    tab_ref[:, 3 * LANE:4 * LANE] = sin


def _rope_table(positions):
    s = positions.shape[0]
    return pl.pallas_call(
        _rope_table_kernel,
        out_shape=jax.ShapeDtypeStruct((s, 4 * LANE), F32),
        name="rope_table",
    )(positions.astype(F32).reshape(s, 1))


def _inproj_kernel(x_ref, g_ref, w_ref, tab_ref, gq_ref, gkv_ref, wq_ref, wqs_ref, wk_ref, wvt_ref, wdvt_ref,
                   q_ref, k_ref, vt_ref, d_ref, dvt_ref, gate_ref, rw_ref, *, tkv):
    x = x_ref[...]
    h = _rms(x, g_ref[...], NORM_EPS).astype(BF16)

    mla = _dot(h, w_ref[:, 0:MLA_G])
    cqn = _rms(mla[:, 0:MLA_Q_RANK], gq_ref[...], NORM_EPS).astype(BF16)
    ckvn = _rms(mla[:, MLA_Q_RANK:MLA_Q_RANK + MLA_KV_RANK], gkv_ref[...], NORM_EPS).astype(BF16)
    o_kpe = MLA_Q_RANK + MLA_KV_RANK
    tab = tab_ref[...]
    cos_q, sin_q = tab[:, 0:LANE], tab[:, LANE:2 * LANE]
    cos_k, sin_k = tab[:, 2 * LANE:3 * LANE], tab[:, 3 * LANE:4 * LANE]
    kpe = mla[:, o_kpe:o_kpe + HP] * cos_k + mla[:, o_kpe + HP:o_kpe + 2 * HP] * sin_k
    qa = _dot(cqn, wq_ref[...])
    qs = _dot(cqn, wqs_ref[...])
    kn = _dot(ckvn, wk_ref[...])
    for hd in range(MLA_HEADS):
        sl = slice(hd * HP, (hd + 1) * HP)
        q_ref[:, sl] = (qa[:, sl] * cos_q + qs[:, sl] * sin_q).astype(BF16)
        k_ref[:, sl] = (kn[:, sl] + kpe).astype(BF16)
    vt = _dot_nt(wvt_ref[...], ckvn).astype(BF16)
    dvt = _dot_nt(wdvt_ref[...], h).astype(BF16)
    for t in range(vt_ref.shape[0]):
        vt_ref[t] = vt[:, t * tkv:(t + 1) * tkv]
        dvt_ref[t] = dvt[:, t * tkv:(t + 1) * tkv]

    dd = _dot(h, w_ref[:, OFF_DIFF:OFF_GATE])
    d_ref[:, 0:DIFF_W] = (dd[:, 0:DIFF_W] * (DIFF_DH ** -0.5 * LOG2E)).astype(BF16)
    d_ref[:, DIFF_W:DIFF_G] = dd[:, DIFF_W:DIFF_G].astype(BF16)
    gate_ref[...] = _dot(h, w_ref[:, OFF_GATE:OFF_RW]).astype(BF16)
    rw_ref[...] = _dot(h, w_ref[:, OFF_RW:])


def _inproj(x2, g, w, tab, gq, gkv, wq, wqs, wk, wvt, wdvt, *, seq, tm, tkv):
    t = x2.shape[0]
    ctot = w.shape[1]
    rw_g = ctot - OFF_RW
    n_seq_tiles = seq // tm
    kv_per_tile = tm // tkv
    row = lambda i: (i, 0)
    slab = lambda i: (i, 0, 0)
    const = lambda i: (0, 0)
    full = lambda a: pl.BlockSpec(a.shape, const)
    return pl.pallas_call(
        functools.partial(_inproj_kernel, tkv=tkv),
        grid=(t // tm,),
        in_specs=[
            pl.BlockSpec((tm, D_MODEL), row),
            full(g), full(w),
            pl.BlockSpec((tm, 4 * LANE), lambda i: (i % n_seq_tiles, 0)),
            full(gq), full(gkv), full(wq), full(wqs), full(wk), full(wvt), full(wdvt),
        ],
        out_specs=[
            pl.BlockSpec((tm, MLA_HEADS * HP), row),
            pl.BlockSpec((tm, MLA_HEADS * HP), row),
            pl.BlockSpec((kv_per_tile, MLA_W, tkv), slab),
            pl.BlockSpec((tm, DIFF_G), row),
            pl.BlockSpec((kv_per_tile, DIFF_W, tkv), slab),
            pl.BlockSpec((tm, GATE_G), row),
            pl.BlockSpec((tm, rw_g), row),
        ],
        out_shape=[
            jax.ShapeDtypeStruct((t, MLA_HEADS * HP), BF16),
            jax.ShapeDtypeStruct((t, MLA_HEADS * HP), BF16),
            jax.ShapeDtypeStruct((t // tkv, MLA_W, tkv), BF16),
            jax.ShapeDtypeStruct((t, DIFF_G), BF16),
            jax.ShapeDtypeStruct((t // tkv, DIFF_W, tkv), BF16),
            jax.ShapeDtypeStruct((t, GATE_G), BF16),
            jax.ShapeDtypeStruct((t, rw_g), F32),
        ],
        compiler_params=pltpu.CompilerParams(
            dimension_semantics=("parallel",), vmem_limit_bytes=VMEM_LIMIT),
        name="inproj",
    )(x2, g, w, tab, gq, gkv, wq, wqs, wk, wvt, wdvt)


def _softmax_step(s, m, l):
    m_new = jnp.maximum(m, jnp.max(s, axis=0, keepdims=True))
    alpha = jnp.exp2(m - m_new)
    p = jnp.exp2(s - m_new)
    return m_new, alpha, p, alpha * l + jnp.sum(p, axis=0, keepdims=True)


def _causal_t(n_keys, n_queries):
    return (lax.broadcasted_iota(jnp.int32, (n_keys, n_queries), 0)
            <= lax.broadcasted_iota(jnp.int32, (n_keys, n_queries), 1))


def _mla_flash_kernel(q_ref, k_ref, vt_ref, gate_ref, o_ref, *, tq, tk):
    i = pl.program_id(2)
    n_h = 2
    half = tq // 2
    hs = [slice(hh * HP, (hh + 1) * HP) for hh in range(n_h)]

    def tile(j, carry, qs, mask):
        rows = pl.ds(pl.multiple_of(j * tk, tk), tk)
        s = [_dot_nt(k_ref[rows, hs[hh]], qs[hh]) for hh in range(n_h)]
        if mask is not None:
            s = [jnp.where(mask, x, -jnp.inf) for x in s]
        st = [_softmax_step(s[hh], carry[hh][0], carry[hh][1]) for hh in range(n_h)]
        vt = vt_ref[j]
        pv = [_dot(vt, st[hh][2].astype(BF16)) for hh in range(n_h)]
        return tuple((st[hh][0], st[hh][3], st[hh][1] * carry[hh][2] + pv[hh]) for hh in range(n_h))

    qs = [q_ref[:, hs[hh]] for hh in range(n_h)]
    one = (jnp.full((1, tq), -jnp.inf, F32), jnp.zeros((1, tq), F32), jnp.zeros((2 * MLA_V, tq), F32))
    carry = lax.fori_loop(0, 2 * i, lambda j, c: tile(j, c, qs, None), (one,) * n_h)
    carry = tile(2 * i, carry, qs, _causal_t(tk, tq))
    late = tile(2 * i + 1, tuple(tuple(a[:, half:] for a in c) for c in carry),
                [q[half:] for q in qs], _causal_t(tk, half))
    outs = [jnp.concatenate([carry[hh][2][:, :half] / carry[hh][1][:, :half], late[hh][2] / late[hh][1]], axis=1)
            for hh in range(n_h)]
    row = lax.broadcasted_iota(jnp.int32, (2 * MLA_V, tq), 0)
    o = jnp.where(row < MLA_V, outs[0], outs[1]).T
    o_ref[...] = (o * _silu(gate_ref[...].astype(F32))).astype(BF16)


def _mla_flash(q, k, vt, gate, *, batch, seq, tq):
    nq = seq // tq
    tk = tq // 2
    nkv = seq // tk
    gate_blk0 = RW_W // LANE
    return pl.pallas_call(
        functools.partial(_mla_flash_kernel, tq=tq, tk=tk),
        grid=(batch, MLA_HEADS // 2, nq),
        in_specs=[
            pl.BlockSpec((tq, 2 * HP), lambda b, p, i: (b * nq + i, p)),
            pl.BlockSpec((seq, 2 * HP), lambda b, p, i: (b, p)),
            pl.BlockSpec((nkv, 2 * MLA_V, tk), lambda b, p, i: (b, p, 0)),
            pl.BlockSpec((tq, LANE), lambda b, p, i: (b * nq + i, gate_blk0 + p)),
        ],
        out_specs=pl.BlockSpec((tq, LANE), lambda b, p, i: (b * nq + i, p)),
        out_shape=jax.ShapeDtypeStruct((batch * seq, MLA_W), BF16),
        compiler_params=pltpu.CompilerParams(
            dimension_semantics=("parallel", "parallel", "parallel"), vmem_limit_bytes=VMEM_LIMIT),
        name="mla_flash",
    )(q, k, vt, gate)


def _diff_flash_kernel(posc_ref, posr_ref, lam_ref, gsub_ref, q_ref, k_ref, vt_ref, gate_ref, o_ref, qm_ref, p_ref,
                       *, tq, lam_init):
    i = pl.program_id(1)
    n_maps = 2 * DIFF_HEADS
    q = q_ref[...]
    lane_q = lax.broadcasted_iota(jnp.int32, (tq, DIFF_W), 1)
    for hm in range(n_maps):
        own = (lane_q >= hm * DIFF_DH) & (lane_q < (hm + 1) * DIFF_DH)
        qm_ref[hm * tq:(hm + 1) * tq, :] = jnp.where(own, q, jnp.zeros_like(q))
    pos_q = posr_ref[i]
    slopes = [LOG2E * 2.0 ** (-8.0 * (h + 1) / DIFF_HEADS) for h in range(DIFF_HEADS)]
    dv = 2 * DIFF_DH

    def tile(j, carry, mask):
        rows = pl.ds(pl.multiple_of(j * tq, tq), tq)
        s_all = _dot_nt(k_ref[rows, :], qm_ref[...])
        dist = jnp.abs(posc_ref[rows, :] - pos_q)
        bias = [slope * dist for slope in slopes]
        st = []
        for hm in range(n_maps):
            s = s_all[:, hm * tq:(hm + 1) * tq] - bias[hm // 2]
            if mask is not None:
                s = jnp.where(mask, s, -jnp.inf)
            m, alpha, p, l = _softmax_step(s, carry[hm][0], carry[hm][1])
            p_ref[:, hm * tq:(hm + 1) * tq] = p.astype(BF16)
            st.append((m, l, alpha))
        pv = _dot(vt_ref[j], p_ref[...])
        new = []
        for hm in range(n_maps):
            h = hm // 2
            new.append((st[hm][0], st[hm][1],
                        st[hm][2] * carry[hm][2] + pv[h * dv:(h + 1) * dv, hm * tq:(hm + 1) * tq]))
        return tuple(new)

    one = (jnp.full((1, tq), -jnp.inf, F32), jnp.zeros((1, tq), F32), jnp.zeros((dv, tq), F32))
    carry = lax.fori_loop(0, i, lambda j, c: tile(j, c, None), (one,) * n_maps)
    carry = tile(i, carry, _causal_t(tq, tq))

    lam = lam_ref[...]
    lam_full = (jnp.exp(jnp.sum(lam[0:1] * lam[1:2], axis=-1, keepdims=True))
                - jnp.exp(jnp.sum(lam[2:3] * lam[3:4], axis=-1, keepdims=True)) + lam_init)
    heads = []
    for h in range(DIFF_HEADS):
        (_, l0, a0), (_, l1, a1) = carry[2 * h], carry[2 * h + 1]
        o = a0 / l0 - lam_full * (a1 / l1)
        ms = jnp.mean(o * o, axis=0, keepdims=True)
        heads.append(o * lax.rsqrt(ms + DIFF_SUBLN_EPS) * gsub_ref[...] * (1.0 - lam_init))
    o = jnp.concatenate(heads, axis=0).T
    o_ref[...] = (o * _silu(gate_ref[...].astype(F32))).astype(BF16)


def _diff_flash(d, dvt, gate, posc, posr, lam, gsub_col, *, batch, seq, tq, lam_init):
    nq = seq // tq
    gate_blk = (RW_W + MLA_W) // DIFF_W
    n_maps = 2 * DIFF_HEADS
    return pl.pallas_call(
        functools.partial(_diff_flash_kernel, tq=tq, lam_init=lam_init),
        grid=(batch, nq),
        in_specs=[
            pl.BlockSpec(posc.shape, lambda b, i: (0, 0)),
            pl.BlockSpec(posr.shape, lambda b, i: (0, 0, 0)),
            pl.BlockSpec(lam.shape, lambda b, i: (0, 0)),
            pl.BlockSpec(gsub_col.shape, lambda b, i: (0, 0)),
            pl.BlockSpec((tq, DIFF_W), lambda b, i: (b * nq + i, 0)),
            pl.BlockSpec((seq, DIFF_W), lambda b, i: (b, 1)),
            pl.BlockSpec((nq, DIFF_W, tq), lambda b, i: (b, 0, 0)),
            pl.BlockSpec((tq, DIFF_W), lambda b, i: (b * nq + i, gate_blk)),
        ],
        out_specs=pl.BlockSpec((tq, DIFF_W), lambda b, i: (b * nq + i, 0)),
        out_shape=jax.ShapeDtypeStruct((batch * seq, DIFF_W), BF16),
        scratch_shapes=[pltpu.VMEM((n_maps * tq, DIFF_W), BF16), pltpu.VMEM((tq, n_maps * tq), BF16)],
        compiler_params=pltpu.CompilerParams(
            dimension_semantics=("parallel", "parallel"), vmem_limit_bytes=VMEM_LIMIT),
        name="diff_flash",
    )(posc, posr, lam, gsub_col, d, d, dvt, gate)


def _rwkv_kernel(*refs, tb, has_vres, cpi):
    n_sc = 16
    if has_vres:
        (p_ref, vf_ref, gate_ref, tri_ref, blk_ref, seg_ref, mu_ref, w0_ref, w2_ref, a0_ref, a2_ref,
         v0_ref, v2_ref, kk_ref, ka_ref, rk_ref, lnw_ref, lnb_ref, y_ref) = refs[:-n_sc]
    else:
        (p_ref, gate_ref, tri_ref, blk_ref, seg_ref, mu_ref, w0_ref, w2_ref, a0_ref, a2_ref,
         kk_ref, ka_ref, rk_ref, lnw_ref, lnb_ref, y_ref, vfo_ref) = refs[:-n_sc]
    (state_ref, carry_ref, at_ref, rt_ref, bt_ref, kt_ref, vv_ref, bw_ref, kw_ref, tot_ref, yy_ref,
     lhs_ref, xav_ref, ay_ref, zt_ref, wlt_ref) = refs[-n_sc:]

    @pl.when(pl.program_id(1) == 0)
    def _():
        state_ref[...] = jnp.zeros_like(state_ref)
        carry_ref[...] = jnp.zeros_like(carry_ref)

    seg = seg_ref[...]

    def segsum(x):
        hi, lo = _split_hi_lo(x)
        return _dot(hi, seg) + _dot(lo, seg)

    p = p_ref[...]
    row = lax.broadcasted_iota(jnp.int32, (tb, 1), 0)
    prev = jnp.where(row == 0, carry_ref[...], pltpu.roll(p, 1, 0))
    carry_ref[...] = p[tb - 1:tb, :]
    xs = p + (prev - p) * mu_ref[...]
    r = xs[:, 0:RW_W]
    k = xs[:, RW_W:2 * RW_W]
    v = xs[:, 2 * RW_W:3 * RW_W]
    hwa = xs[:, 3 * RW_W:3 * RW_W + LANE]
    z = -(w0_ref[...] + _dot(jnp.tanh(hwa).astype(BF16), w2_ref[...]))
    softplus = jnp.maximum(z, 0.0) + jnp.log(1.0 + jnp.exp(-jnp.abs(z)))
    lw = -jnp.exp(-softplus - 0.5)
    a = _sigmoid(a0_ref[...] + _dot(hwa.astype(BF16), a2_ref[...]))
    if has_vres:
        hv = xs[:, RW_SHIFT_BASE:RW_SHIFT_BASE + LANE]
        v = v + (vf_ref[...] - v) * _sigmoid(v0_ref[...] + _dot(hv.astype(BF16), v2_ref[...]))
    else:
        vfo_ref[...] = v
    kk = k * kk_ref[...]
    kk = kk / jnp.maximum(jnp.sqrt(segsum(kk * kk)), 1e-12)
    k = k * (1.0 + (a - 1.0) * ka_ref[...])

    lw_hi, lw_lo = _split_hi_lo(lw)
    cum = _dot(tri_ref[...], lw_hi) + _dot(tri_ref[...], lw_lo)
    tot = _dot(blk_ref[...], lw_hi) + _dot(blk_ref[...], lw_lo)
    w_inv = jnp.exp(-cum)
    w_end = jnp.exp(tot - cum)
    kka = kk * a
    rt_ref[...] = (r * jnp.exp(cum)).astype(BF16)
    at_ref[...] = (-kk * jnp.exp(cum - lw)).astype(BF16)
    bt_ref[...] = (kka * w_inv).astype(BF16)
    kt_ref[...] = (k * w_inv).astype(BF16)
    vv_ref[...] = v.astype(BF16)
    bw_ref[...] = kka * w_end
    kw_ref[...] = k * w_end
    tot_ref[...] = tot

    c = RW_CHUNK
    n_pairs = RW_HEADS // 2
    lane_c = lax.broadcasted_iota(jnp.int32, (c, LANE), 1)
    even_c = lane_c < RW_N
    even_2c = lax.broadcasted_iota(jnp.int32, (2 * c, LANE), 1) < RW_N
    ri = lax.broadcasted_iota(jnp.int32, (2 * c, 2 * c), 0)
    ci = lax.broadcasted_iota(jnp.int32, (2 * c, 2 * c), 1)
    cm = jnp.where(ci >= c, ci - c, ci)
    keep = ((ri >= c) & (cm <= ri - c)) | ((ri < c) & (cm < ri))
    same_head = (ri < c) == (ci < c)
    eye2 = (lax.broadcasted_iota(jnp.int32, (c, LANE), 0)
            == jnp.where(lane_c >= c, lane_c - c, lane_c)).astype(F32)

    def bdiag(m):
        zero = jnp.zeros_like(m)
        return jnp.concatenate([jnp.where(even_c, m, zero), jnp.where(even_c, zero, m)], axis=0)

    def adiag(m):
        zero = jnp.zeros_like(m)
        return jnp.concatenate([jnp.where(even_c, zero, m), jnp.where(even_c, m, zero)], axis=0)

    def phase_a(i, _):
        probs = [(i * cpi + cc, pr) for cc in range(cpi) for pr in range(n_pairs)]
        rows = [pl.ds(pl.multiple_of(ic * c, c), c) for ic, _ in probs]
        rows2 = [pl.ds(pl.multiple_of(ic * 2 * c, 2 * c), 2 * c) for ic, _ in probs]
        ls = [slice(pr * LANE, (pr + 1) * LANE) for _, pr in probs]
        rng = range(len(probs))
        at = [at_ref[rows[j], ls[j]] for j in rng]
        rt = [rt_ref[rows[j], ls[j]] for j in rng]
        bt = [bt_ref[rows[j], ls[j]] for j in rng]
        kt = [kt_ref[rows[j], ls[j]] for j in rng]
        vh = [vv_ref[rows[j], ls[j]] for j in rng]
        xar = [jnp.concatenate([at[j], rt[j]], axis=0) for j in rng]
        zero_b = jnp.zeros((2 * c, LANE), BF16)
        aa_e = [jnp.where(keep, _dot_nt(jnp.where(even_2c, xar[j], zero_b),
                                        jnp.concatenate([bt[j], kt[j]], axis=0)), 0.0) for j in rng]
        aa_o = [jnp.where(keep, _dot_nt(jnp.where(even_2c, zero_b, xar[j]),
                                        jnp.concatenate([kt[j], bt[j]], axis=0)), 0.0) for j in rng]
        nn = [jnp.where(even_c, aa_e[j][0:c], aa_o[j][0:c]) for j in rng]
        aak = [jnp.where(even_c, aa_o[j][0:c], aa_e[j][0:c]) for j in rng]
        for j in rng:
            pr = probs[j][1]
            ay_ref[rows[j], 2 * pr * LANE:(2 * pr + 1) * LANE] = jnp.where(
                even_c, aa_e[j][c:2 * c], aa_o[j][c:2 * c]).astype(BF16)
            ay_ref[rows[j], (2 * pr + 1) * LANE:(2 * pr + 2) * LANE] = jnp.where(
                even_c, aa_o[j][c:2 * c], aa_e[j][c:2 * c]).astype(BF16)
        av = [_dot(aak[j].astype(BF16), adiag(vh[j])) for j in rng]
        x = [eye2 + nn[j] for j in rng]
        pw = [nn[j].astype(BF16) for j in rng]
        pw = [_dot(pw[j], bdiag(pw[j])).astype(BF16) for j in rng]
        n_sq = int(math.log2(c)) - 1
        for it in range(n_sq):
            last = it == n_sq - 1
            lhs = [x[j].astype(BF16) if last else jnp.concatenate([x[j].astype(BF16), pw[j]], axis=0) for j in rng]
            out = [_dot(lhs[j], bdiag(pw[j])) for j in rng]
            x = [x[j] + out[j][0:c] for j in rng]
            if not last:
                pw = [out[j][c:2 * c].astype(BF16) for j in rng]
        fin = [_dot(x[j].astype(BF16), jnp.concatenate([bdiag(at[j]), bdiag(av[j].astype(BF16))], axis=1))
               for j in rng]
        for j in rng:
            lhs_ref[rows2[j], ls[j]] = jnp.concatenate([fin[j][:, 0:LANE].astype(BF16), rt[j]], axis=0)
            xav_ref[rows[j], ls[j]] = fin[j][:, LANE:2 * LANE]
            zz = jnp.concatenate([bw_ref[rows[j], ls[j]], kw_ref[rows[j], ls[j]]], axis=0)
            zt_ref[rows2[j], ls[j]] = zz.T.astype(BF16)
            tot_row = tot_ref[pl.ds(probs[j][0] * c, 1), ls[j]]
            wlt_ref[rows2[j], ls[j]] = jnp.exp(jnp.broadcast_to(tot_row, (2 * c, LANE)).T)
        return 0

    lax.fori_loop(0, tb // (c * cpi), phase_a, 0)

    def phase_b(ic, _):
        rows = pl.ds(pl.multiple_of(ic * c, c), c)
        rows2 = pl.ds(pl.multiple_of(ic * 2 * c, 2 * c), 2 * c)
        ls = [slice(pr * LANE, (pr + 1) * LANE) for pr in range(n_pairs)]
        rng = range(n_pairs)
        mt = [state_ref[pr] for pr in rng]
        out = [_dot(lhs_ref[rows2, ls[pr]], mt[pr].astype(BF16)) for pr in rng]
        ub = [(out[pr][0:c] + xav_ref[rows, ls[pr]]).astype(BF16) for pr in rng]
        vh = [vv_ref[rows, ls[pr]] for pr in rng]
        ft = [_dot(zt_ref[rows2, ls[pr]], jnp.concatenate([ub[pr], vh[pr]], axis=0)) for pr in rng]
        for pr in rng:
            state_ref[pr] = mt[pr] * wlt_ref[rows2, ls[pr]] + jnp.where(same_head, ft[pr], 0.0)
        for pr in rng:
            uv = jnp.concatenate([bdiag(ub[pr]), adiag(vh[pr])], axis=0)
            yy_ref[rows, ls[pr]] = out[pr][c:2 * c] + _dot(ay_ref[rows, 2 * pr * LANE:(2 * pr + 2) * LANE], uv)
        return 0

    lax.fori_loop(0, tb // c, phase_b, 0)

    y = yy_ref[...]
    inv_n = 1.0 / RW_N
    dlt = y - segsum(y) * inv_n
    var = segsum(dlt * dlt) * inv_n
    y = dlt * lax.rsqrt(var + RW_GN_EPS) * lnw_ref[...] + lnb_ref[...]
    y = y + segsum(r * k * rk_ref[...]) * v
    y_ref[...] = (y * _silu(gate_ref[...].astype(F32))).astype(BF16)


def _rwkv(p, vfirst, gate, consts, params, *, batch, seq, tb, cpi):
    has_vres = vfirst is not None
    nb = seq // tb
    cw = p.shape[1]
    row = lambda b, i: (b * nb + i, 0)
    full = lambda a: pl.BlockSpec(a.shape, lambda b, i: (0,) * a.ndim)
    ins = [p] + ([vfirst] if has_vres else []) + [gate] + list(consts) + list(params)
    in_specs = ([pl.BlockSpec((tb, cw), row)]
                + ([pl.BlockSpec((tb, RW_W), row)] if has_vres else [])
                + [pl.BlockSpec((tb, RW_W), row)]
                + [full(a) for a in consts] + [full(a) for a in params])
    out_specs = [pl.BlockSpec((tb, RW_W), row)]
    out_shape = [jax.ShapeDtypeStruct((batch * seq, RW_W), BF16)]
    if not has_vres:
        out_specs.append(pl.BlockSpec((tb, RW_W), row))
        out_shape.append(jax.ShapeDtypeStruct((batch * seq, RW_W), F32))
    vm = lambda rows, cols, dt: pltpu.VMEM((rows, cols), dt)
    scratch = [pltpu.VMEM((RW_HEADS // 2, LANE, LANE), F32), vm(1, cw, F32),
               vm(tb, RW_W, BF16), vm(tb, RW_W, BF16), vm(tb, RW_W, BF16), vm(tb, RW_W, BF16), vm(tb, RW_W, BF16),
               vm(tb, RW_W, F32), vm(tb, RW_W, F32), vm(tb, RW_W, F32), vm(tb, RW_W, F32),
               vm(2 * tb, RW_W, BF16), vm(tb, RW_W, F32), vm(tb, 2 * RW_W, BF16),
               vm(2 * tb, RW_W, BF16), vm(2 * tb, RW_W, F32)]
    outs = pl.pallas_call(
        functools.partial(_rwkv_kernel, tb=tb, has_vres=has_vres, cpi=cpi),
        grid=(batch, nb),
        in_specs=in_specs, out_specs=out_specs, out_shape=out_shape,
        scratch_shapes=scratch,
        compiler_params=pltpu.CompilerParams(
            dimension_semantics=("parallel", "arbitrary"), vmem_limit_bytes=VMEM_LIMIT),
        name="rwkv7",
    )(*ins)
    return outs if not has_vres else (outs[0], None)


def _outproj_kernel(x_ref, om_ref, od_ref, orw_ref, w_ref, fg_ref, o_ref, *, final):
    acc = (_dot(om_ref[...], w_ref[0:MLA_W, :])
           + _dot(od_ref[...], w_ref[MLA_W:MLA_W + DIFF_W, :])
           + _dot(orw_ref[...], w_ref[MLA_W + DIFF_W:, :]))
    y = x_ref[...] + acc
    if final:
        y = _rms(y, fg_ref[...], NORM_EPS)
    o_ref[...] = y


def _outproj(x2, om, od, orw, w, fg, *, tm, final):
    t = x2.shape[0]
    row = lambda i: (i, 0)
    const = lambda i: (0, 0)
    return pl.pallas_call(
        functools.partial(_outproj_kernel, final=final),
        grid=(t // tm,),
        in_specs=[pl.BlockSpec((tm, D_MODEL), row), pl.BlockSpec((tm, MLA_W), row),
                  pl.BlockSpec((tm, DIFF_W), row), pl.BlockSpec((tm, RW_W), row),
                  pl.BlockSpec(w.shape, const), pl.BlockSpec(fg.shape, const)],
        out_specs=pl.BlockSpec((tm, D_MODEL), row),
        out_shape=jax.ShapeDtypeStruct((t, D_MODEL), F32),
        compiler_params=pltpu.CompilerParams(
            dimension_semantics=("parallel",), vmem_limit_bytes=VMEM_LIMIT),
        name="outproj",
    )(x2, om, od, orw, w, fg)


def _pack_inproj(w_in, w_vres):
    d = w_in.shape[0]
    o = 0
    cq = w_in[:, o:o + MLA_Q_RANK]; o += MLA_Q_RANK
    ckv = w_in[:, o:o + MLA_KV_RANK]; o += MLA_KV_RANK
    kpe = w_in[:, o:o + MLA_ROPE]; o += MLA_ROPE
    dqk = w_in[:, o:o + DIFF_G]; o += DIFF_G
    dv = w_in[:, o:o + DIFF_W]; o += DIFF_W
    gate = w_in[:, o:o + D_MIX]; o += D_MIX
    rw = w_in[:, o:]
    half = MLA_ROPE // 2
    z = lambda n: jnp.zeros((d, n), w_in.dtype)
    kp = jnp.concatenate([z(MLA_NOPE), kpe, z(HP - MLA_QK)], axis=1)
    kps = jnp.concatenate([z(MLA_NOPE), kpe[:, half:], kpe[:, :half], z(HP - MLA_QK)], axis=1)
    g_mla, g_diff, g_rw = gate[:, :MLA_W], gate[:, MLA_W:MLA_W + DIFF_W], gate[:, MLA_W + DIFF_W:]
    cols = [cq, ckv, kp, kps, dqk, g_rw, g_mla, g_diff, rw]
    if w_vres is not None:
        cols += [w_vres, z(LANE - RW_MV_RANK)]
    return jnp.concatenate(cols, axis=1).astype(BF16), dv.T.astype(BF16)


def _pack_mla(w_uq, w_ukv):
    rq, rkv = w_uq.shape[0], w_ukv.shape[0]
    half = MLA_ROPE // 2
    q3 = w_uq.reshape(rq, MLA_HEADS, MLA_QK)
    zq = lambda n: jnp.zeros((rq, MLA_HEADS, n), w_uq.dtype)
    wq = jnp.concatenate([q3, zq(HP - MLA_QK)], axis=2)
    wqs = jnp.concatenate([zq(MLA_NOPE), q3[:, :, MLA_NOPE + half:], q3[:, :, MLA_NOPE:MLA_NOPE + half],
                           zq(HP - MLA_QK)], axis=2)
    kv3 = w_ukv.reshape(rkv, MLA_HEADS, MLA_NOPE + MLA_V)
    wk = jnp.concatenate([kv3[:, :, :MLA_NOPE], jnp.zeros((rkv, MLA_HEADS, HP - MLA_NOPE), w_ukv.dtype)], axis=2)
    wv = kv3[:, :, MLA_NOPE:]
    flat = lambda a: a.reshape(a.shape[0], -1).astype(BF16)
    return flat(wq), flat(wqs), flat(wk), flat(wv).T


def _pad_rows(w, top, total):
    return jnp.concatenate([jnp.zeros((top, w.shape[1]), w.dtype), w,
                            jnp.zeros((total - top - w.shape[0], w.shape[1]), w.dtype)], axis=0).astype(BF16)


def _rwkv_consts(tb):
    t = jnp.arange(tb)
    same = (t[:, None] // RW_CHUNK) == (t[None, :] // RW_CHUNK)
    tri = (same & (t[None, :] <= t[:, None])).astype(BF16)
    blk = same.astype(BF16)
    hl = jnp.arange(RW_W) // RW_N
    seg = (hl[:, None] == hl[None, :]).astype(BF16)
    return tri, blk, seg


def kernel(x, positions, pre_g, w_in, w_in_vres, w_out, mla_gq, mla_gkv, mla_wuq, mla_wukv, diff_lam, diff_gsub, rw_mu, rw_mu_vres, rw_w0, rw_w2, rw_a0, rw_a2, rw_v0, rw_v2, rw_kk, rw_ka, rw_rk, rw_lnw, rw_lnb, final_g):
    batch, seq, _ = x.shape
    depth = pre_g.shape[0]
    tm = min(512, seq)
    tq = min(256, seq)
    tq_mla = min(512, seq)
    tb = min(256, seq)
    assert seq % tm == 0 and seq % tq_mla == 0 and tq_mla == 2 * tq and tm % tq == 0
    assert seq % tb == 0 and tb % RW_CHUNK == 0

    x2 = x.reshape(batch * seq, D_MODEL)
    tab = _rope_table(positions)
    pos_f = positions.astype(F32)
    posc = pos_f.reshape(seq, 1)
    posr = pos_f.reshape(seq // tq, 1, tq)
    consts = _rwkv_consts(tb)
    r1 = lambda a: a.reshape(1, -1).astype(F32)

    vfirst = None
    for layer in range(depth):
        vres = layer > 0
        w, wdvt = _pack_inproj(w_in[layer], w_in_vres[layer - 1] if vres else None)
        wq, wqs, wk, wvt = _pack_mla(mla_wuq[layer], mla_wukv[layer])
        q, k, vt, d, dvt, gate, rw = _inproj(x2, r1(pre_g[layer]), w, tab, r1(mla_gq[layer]), r1(mla_gkv[layer]),
                                             wq, wqs, wk, wvt, wdvt, seq=seq, tm=tm, tkv=tq)
        o_mla = _mla_flash(q, k, vt, gate, batch=batch, seq=seq, tq=tq_mla)
        lam_init = 0.8 - 0.6 * math.exp(-0.3 * (layer + 1))
        gsub_col = diff_gsub[layer].reshape(-1, 1).astype(F32)
        o_diff = _diff_flash(d, dvt, gate, posc, posr, diff_lam[layer].astype(F32), gsub_col,
                             batch=batch, seq=seq, tq=tq, lam_init=lam_init)
        mu = rw_mu[layer]
        if vres:
            mu = jnp.concatenate([mu, rw_mu_vres[layer - 1], jnp.zeros((LANE - RW_MV_RANK,), mu.dtype)])
        params = [r1(mu), r1(rw_w0[layer]), _pad_rows(rw_w2[layer], 0, LANE), r1(rw_a0[layer]),
                  _pad_rows(rw_a2[layer], RW_DECAY_RANK, LANE)]
        if vres:
            params += [r1(rw_v0[layer - 1]), _pad_rows(rw_v2[layer - 1], 0, LANE)]
        params += [r1(rw_kk[layer]), r1(rw_ka[layer]), r1(rw_rk[layer]), r1(rw_lnw[layer]), r1(rw_lnb[layer])]
        o_rw, vf = _rwkv(rw, vfirst, gate, consts, params, batch=batch, seq=seq, tb=tb, cpi=2)
        if not vres:
            vfirst = vf
        x2 = _outproj(x2, o_mla, o_diff, o_rw, w_out[layer].astype(BF16), r1(final_g),
                      tm=tm, final=(layer == depth - 1))
    return x2.reshape(batch, seq, D_MODEL)
```

```python
import functools
import math

import jax
import jax.numpy as jnp
from jax import lax
from jax.experimental import pallas as pl
from jax.experimental.pallas import tpu as pltpu

F32 = jnp.float32
BF16 = jnp.bfloat16

D_MODEL = 1024
D_MIX = 1024
NORM_EPS = 1e-6
MLA_HEADS, MLA_NOPE, MLA_ROPE, MLA_V = 6, 64, 32, 64
MLA_Q_RANK, MLA_KV_RANK = 256, 128
MLA_QK = MLA_NOPE + MLA_ROPE
ROPE_THETA = 10000.0
MLA_W = MLA_HEADS * MLA_V
DIFF_HEADS, DIFF_DH = 4, 32
DIFF_W = DIFF_HEADS * 2 * DIFF_DH
DIFF_SUBLN_EPS = 1e-5
RW_HEADS, RW_N = 6, 64
RW_W = RW_HEADS * RW_N
RW_DECAY_RANK, RW_AAA_RANK, RW_MV_RANK = 64, 64, 32
RW_GN_EPS = 64e-5
RW_SHIFT_BASE = 3 * RW_W + RW_DECAY_RANK + RW_AAA_RANK

LANE = 128
SUBLANE = 8
LOG2E = math.log2(math.e)
HP = LANE
RW_CHUNK = 64

MLA_G = MLA_Q_RANK + MLA_KV_RANK + 2 * HP
DIFF_G = 2 * DIFF_W
GATE_G = D_MIX
RW_G0 = RW_SHIFT_BASE
RW_G1 = RW_SHIFT_BASE + LANE
OFF_DIFF = MLA_G
OFF_GATE = OFF_DIFF + DIFF_G
OFF_RW = OFF_GATE + GATE_G

VMEM_LIMIT = 52 * 1024 * 1024

_NT = (((1,), (1,)), ((), ()))
_TN = (((0,), (0,)), ((), ()))


def _dot(a, b):
    return jnp.dot(a, b, preferred_element_type=F32)


def _dot_nt(a, b):
    return lax.dot_general(a, b, _NT, preferred_element_type=F32)


def _split_hi_lo(x):
    hi = x.astype(BF16)
    return hi, (x - hi.astype(F32)).astype(BF16)


def _sigmoid(z):
    return 1.0 / (1.0 + jnp.exp(-z))


def _silu(g):
    return g * _sigmoid(g)


def _rms(x, g, eps):
    return x * lax.rsqrt(jnp.mean(x * x, axis=-1, keepdims=True) + eps) * g


def _rope_table_kernel(pos_ref, tab_ref):
    pos = pos_ref[...]
    lane = lax.broadcasted_iota(jnp.int32, (1, LANE), 1)
    half = MLA_ROPE // 2
    in_rope = (lane >= MLA_NOPE) & (lane < MLA_QK)
    first = lane < MLA_NOPE + half
    idx = jnp.where(first, lane - MLA_NOPE, lane - MLA_NOPE - half).astype(F32)
    inv = jnp.exp(idx * (-math.log(ROPE_THETA) / half))
    ang = pos * inv
    cos = jnp.where(in_rope, jnp.cos(ang), jnp.where(lane < MLA_NOPE, 1.0, 0.0))
    sin = jnp.sin(ang)
    sin = jnp.where(in_rope, jnp.where(first, -sin, sin), 0.0)
    scale = MLA_QK ** -0.5 * LOG2E
    tab_ref[:, 0 * LANE:1 * LANE] = cos * scale
    tab_ref[:, 1 * LANE:2 * LANE] = sin * scale
    tab_ref[:, 2 * LANE:3 * LANE] = cos
    tab_ref[:, 3 * LANE:4 * LANE] = sin


def _rope_table(positions):
    s = positions.shape[0]
    return pl.pallas_call(
        _rope_table_kernel,
        out_shape=jax.ShapeDtypeStruct((s, 4 * LANE), F32),
        name="rope_table",
    )(positions.astype(F32).reshape(s, 1))


def _inproj_kernel(x_ref, g_ref, w_ref, tab_ref, gq_ref, gkv_ref, wq_ref, wqs_ref, wk_ref, wvt_ref, wdvt_ref,
                   q_ref, k_ref, vt_ref, d_ref, dvt_ref, gate_ref, rw_ref, *, tkv):
    x = x_ref[...]
    h = _rms(x, g_ref[...], NORM_EPS).astype(BF16)

    mla = _dot(h, w_ref[:, 0:MLA_G])
    cqn = _rms(mla[:, 0:MLA_Q_RANK], gq_ref[...], NORM_EPS).astype(BF16)
    ckvn = _rms(mla[:, MLA_Q_RANK:MLA_Q_RANK + MLA_KV_RANK], gkv_ref[...], NORM_EPS).astype(BF16)
    o_kpe = MLA_Q_RANK + MLA_KV_RANK
    tab = tab_ref[...]
    cos_q, sin_q = tab[:, 0:LANE], tab[:, LANE:2 * LANE]
    cos_k, sin_k = tab[:, 2 * LANE:3 * LANE], tab[:, 3 * LANE:4 * LANE]
    kpe = mla[:, o_kpe:o_kpe + HP] * cos_k + mla[:, o_kpe + HP:o_kpe + 2 * HP] * sin_k
    qa = _dot(cqn, wq_ref[...])
    qs = _dot(cqn, wqs_ref[...])
    kn = _dot(ckvn, wk_ref[...])
    for hd in range(MLA_HEADS):
        sl = slice(hd * HP, (hd + 1) * HP)
        q_ref[:, sl] = (qa[:, sl] * cos_q + qs[:, sl] * sin_q).astype(BF16)
        k_ref[:, sl] = (kn[:, sl] + kpe).astype(BF16)
    vt = _dot_nt(wvt_ref[...], ckvn).astype(BF16)
    dvt = _dot_nt(wdvt_ref[...], h).astype(BF16)
    for t in range(vt_ref.shape[0]):
        vt_ref[t] = vt[:, t * tkv:(t + 1) * tkv]
        dvt_ref[t] = dvt[:, t * tkv:(t + 1) * tkv]

    dd = _dot(h, w_ref[:, OFF_DIFF:OFF_GATE])
    d_ref[:, 0:DIFF_W] = (dd[:, 0:DIFF_W] * (DIFF_DH ** -0.5 * LOG2E)).astype(BF16)
    d_ref[:, DIFF_W:DIFF_G] = dd[:, DIFF_W:DIFF_G].astype(BF16)
    gate_ref[...] = _dot(h, w_ref[:, OFF_GATE:OFF_RW]).astype(BF16)
    rw_ref[...] = _dot(h, w_ref[:, OFF_RW:])


def _inproj(x2, g, w, tab, gq, gkv, wq, wqs, wk, wvt, wdvt, *, seq, tm, tkv):
    t = x2.shape[0]
    ctot = w.shape[1]
    rw_g = ctot - OFF_RW
    n_seq_tiles = seq // tm
    kv_per_tile = tm // tkv
    row = lambda i: (i, 0)
    slab = lambda i: (i, 0, 0)
    const = lambda i: (0, 0)
    full = lambda a: pl.BlockSpec(a.shape, const)
    return pl.pallas_call(
        functools.partial(_inproj_kernel, tkv=tkv),
        grid=(t // tm,),
        in_specs=[
            pl.BlockSpec((tm, D_MODEL), row),
            full(g), full(w),
            pl.BlockSpec((tm, 4 * LANE), lambda i: (i % n_seq_tiles, 0)),
            full(gq), full(gkv), full(wq), full(wqs), full(wk), full(wvt), full(wdvt),
        ],
        out_specs=[
            pl.BlockSpec((tm, MLA_HEADS * HP), row),
            pl.BlockSpec((tm, MLA_HEADS * HP), row),
            pl.BlockSpec((kv_per_tile, MLA_W, tkv), slab),
            pl.BlockSpec((tm, DIFF_G), row),
            pl.BlockSpec((kv_per_tile, DIFF_W, tkv), slab),
            pl.BlockSpec((tm, GATE_G), row),
            pl.BlockSpec((tm, rw_g), row),
        ],
        out_shape=[
            jax.ShapeDtypeStruct((t, MLA_HEADS * HP), BF16),
            jax.ShapeDtypeStruct((t, MLA_HEADS * HP), BF16),
            jax.ShapeDtypeStruct((t // tkv, MLA_W, tkv), BF16),
            jax.ShapeDtypeStruct((t, DIFF_G), BF16),
            jax.ShapeDtypeStruct((t // tkv, DIFF_W, tkv), BF16),
            jax.ShapeDtypeStruct((t, GATE_G), BF16),
            jax.ShapeDtypeStruct((t, rw_g), F32),
        ],
        compiler_params=pltpu.CompilerParams(
            dimension_semantics=("parallel",), vmem_limit_bytes=VMEM_LIMIT),
        name="inproj",
    )(x2, g, w, tab, gq, gkv, wq, wqs, wk, wvt, wdvt)


def _softmax_step(s, m, l):
    m_new = jnp.maximum(m, jnp.max(s, axis=0, keepdims=True))
    alpha = jnp.exp2(m - m_new)
    p = jnp.exp2(s - m_new)
    return m_new, alpha, p, alpha * l + jnp.sum(p, axis=0, keepdims=True)


def _causal_t(n_keys, n_queries):
    return (lax.broadcasted_iota(jnp.int32, (n_keys, n_queries), 0)
            <= lax.broadcasted_iota(jnp.int32, (n_keys, n_queries), 1))


def _mla_flash_kernel(q_ref, k_ref, vt_ref, gate_ref, o_ref, *, tq, tk, n_h):
    i = pl.program_id(2)
    half = tq // 2
    pw = MLA_V
    hs = [slice(hh * HP, (hh + 1) * HP) for hh in range(n_h)]
    ps = [slice(hh * pw, (hh + 1) * pw) for hh in range(n_h)]

    def tile(j, carry, qs, mask, n_slab):
        rows = pl.ds(pl.multiple_of(j * (n_slab * tk), n_slab * tk), n_slab * tk)
        s = [_dot_nt(k_ref[rows, hs[hh]], qs[hh]) for hh in range(n_h)]
        if mask is not None:
            s = [jnp.where(mask, x, -jnp.inf) for x in s]
        st = [_softmax_step(s[hh], carry[hh][0], carry[hh][1]) for hh in range(n_h)]
        p = [st[hh][2].astype(BF16) for hh in range(n_h)]
        vt = [vt_ref[j * n_slab + t] for t in range(n_slab)]
        pv = [sum(_dot(vt[t][ps[hh]], p[hh][t * tk:(t + 1) * tk]) for t in range(n_slab))
              for hh in range(n_h)]
        return tuple((st[hh][0], st[hh][3], st[hh][1] * carry[hh][2] + pv[hh]) for hh in range(n_h))

    qs = [q_ref[:, hs[hh]] for hh in range(n_h)]
    one = (jnp.full((1, tq), -jnp.inf, F32), jnp.zeros((1, tq), F32), jnp.zeros((pw, tq), F32))
    carry = lax.fori_loop(0, i, lambda j, c: tile(j, c, qs, None, tq // tk), (one,) * n_h)
    carry = tile(2 * i, carry, qs, _causal_t(tk, tq), 1)
    late = tile(2 * i + 1, tuple(tuple(a[:, half:] for a in c) for c in carry),
                [q[half:] for q in qs], _causal_t(tk, half), 1)
    outs = [jnp.concatenate([carry[hh][2][:, :half] / carry[hh][1][:, :half], late[hh][2] / late[hh][1]], axis=1)
            for hh in range(n_h)]
    o = jnp.concatenate(outs, axis=0).T
    o_ref[...] = (o * _silu(gate_ref[...].astype(F32))).astype(BF16)


def _mla_flash(q, k, vt, gate, *, batch, seq, tq, n_h):
    nq = seq // tq
    tk = tq // 2
    nkv = seq // tk
    ow = n_h * MLA_V
    gate_blk0 = RW_W // ow
    return pl.pallas_call(
        functools.partial(_mla_flash_kernel, tq=tq, tk=tk, n_h=n_h),
        grid=(batch, MLA_HEADS // n_h, nq),
        in_specs=[
            pl.BlockSpec((tq, n_h * HP), lambda b, g, i: (b * nq + i, g)),
            pl.BlockSpec((seq, n_h * HP), lambda b, g, i: (b, g)),
            pl.BlockSpec((nkv, ow, tk), lambda b, g, i: (b, g, 0)),
            pl.BlockSpec((tq, ow), lambda b, g, i: (b * nq + i, gate_blk0 + g)),
        ],
        out_specs=pl.BlockSpec((tq, ow), lambda b, g, i: (b * nq + i, g)),
        out_shape=jax.ShapeDtypeStruct((batch * seq, MLA_W), BF16),
        compiler_params=pltpu.CompilerParams(
            dimension_semantics=("parallel", "parallel", "parallel"), vmem_limit_bytes=VMEM_LIMIT),
        name="mla_flash",
    )(q, k, vt, gate)


def _diff_flash_kernel(posc_ref, posr_ref, lam_ref, gsub_ref, q_ref, k_ref, vt_ref, gate_ref, o_ref, qm_ref, p_ref,
                       *, tq, lam_init):
    i = pl.program_id(1)
    n_maps = 2 * DIFF_HEADS
    q = q_ref[...]
    lane_q = lax.broadcasted_iota(jnp.int32, (tq, DIFF_W), 1)
    for hm in range(n_maps):
        own = (lane_q >= hm * DIFF_DH) & (lane_q < (hm + 1) * DIFF_DH)
        qm_ref[hm * tq:(hm + 1) * tq, :] = jnp.where(own, q, jnp.zeros_like(q))
    pos_q = posr_ref[i]
    slopes = [LOG2E * 2.0 ** (-8.0 * (h + 1) / DIFF_HEADS) for h in range(DIFF_HEADS)]
    dv = 2 * DIFF_DH

    def tile(j, carry, mask):
        rows = pl.ds(pl.multiple_of(j * tq, tq), tq)
        s_all = _dot_nt(k_ref[rows, :], qm_ref[...])
        dist = jnp.abs(posc_ref[rows, :] - pos_q)
        bias = [slope * dist for slope in slopes]
        st = []
        for hm in range(n_maps):
            s = s_all[:, hm * tq:(hm + 1) * tq] - bias[hm // 2]
            if mask is not None:
                s = jnp.where(mask, s, -jnp.inf)
            m, alpha, p, l = _softmax_step(s, carry[hm][0], carry[hm][1])
            p_ref[:, hm * tq:(hm + 1) * tq] = p.astype(BF16)
            st.append((m, l, alpha))
        vt = vt_ref[j]
        pv = [_dot(vt[h * dv:(h + 1) * dv], p_ref[:, 2 * h * tq:(2 * h + 2) * tq]) for h in range(DIFF_HEADS)]
        new = []
        for hm in range(n_maps):
            new.append((st[hm][0], st[hm][1],
                        st[hm][2] * carry[hm][2] + pv[hm // 2][:, (hm % 2) * tq:(hm % 2 + 1) * tq]))
        return tuple(new)

    one = (jnp.full((1, tq), -jnp.inf, F32), jnp.zeros((1, tq), F32), jnp.zeros((dv, tq), F32))
    carry = lax.fori_loop(0, i, lambda j, c: tile(j, c, None), (one,) * n_maps)
    carry = tile(i, carry, _causal_t(tq, tq))

    lam = lam_ref[...]
    lam_full = (jnp.exp(jnp.sum(lam[0:1] * lam[1:2], axis=-1, keepdims=True))
                - jnp.exp(jnp.sum(lam[2:3] * lam[3:4], axis=-1, keepdims=True)) + lam_init)
    heads = []
    for h in range(DIFF_HEADS):
        (_, l0, a0), (_, l1, a1) = carry[2 * h], carry[2 * h + 1]
        o = a0 / l0 - lam_full * (a1 / l1)
        ms = jnp.mean(o * o, axis=0, keepdims=True)
        heads.append(o * lax.rsqrt(ms + DIFF_SUBLN_EPS) * gsub_ref[...] * (1.0 - lam_init))
    o = jnp.concatenate(heads, axis=0).T
    o_ref[...] = (o * _silu(gate_ref[...].astype(F32))).astype(BF16)


def _diff_flash(d, dvt, gate, posc, posr, lam, gsub_col, *, batch, seq, tq, lam_init):
    nq = seq // tq
    gate_blk = (RW_W + MLA_W) // DIFF_W
    n_maps = 2 * DIFF_HEADS
    return pl.pallas_call(
        functools.partial(_diff_flash_kernel, tq=tq, lam_init=lam_init),
        grid=(batch, nq),
        in_specs=[
            pl.BlockSpec(posc.shape, lambda b, i: (0, 0)),
            pl.BlockSpec(posr.shape, lambda b, i: (0, 0, 0)),
            pl.BlockSpec(lam.shape, lambda b, i: (0, 0)),
            pl.BlockSpec(gsub_col.shape, lambda b, i: (0, 0)),
            pl.BlockSpec((tq, DIFF_W), lambda b, i: (b * nq + i, 0)),
            pl.BlockSpec((seq, DIFF_W), lambda b, i: (b, 1)),
            pl.BlockSpec((nq, DIFF_W, tq), lambda b, i: (b, 0, 0)),
            pl.BlockSpec((tq, DIFF_W), lambda b, i: (b * nq + i, gate_blk)),
        ],
        out_specs=pl.BlockSpec((tq, DIFF_W), lambda b, i: (b * nq + i, 0)),
        out_shape=jax.ShapeDtypeStruct((batch * seq, DIFF_W), BF16),
        scratch_shapes=[pltpu.VMEM((n_maps * tq, DIFF_W), BF16), pltpu.VMEM((tq, n_maps * tq), BF16)],
        compiler_params=pltpu.CompilerParams(
            dimension_semantics=("parallel", "parallel"), vmem_limit_bytes=VMEM_LIMIT),
        name="diff_flash",
    )(posc, posr, lam, gsub_col, d, d, dvt, gate)


def _rwkv_kernel(*refs, tb, has_vres, cpi):
    n_sc = 16
    if has_vres:
        (p_ref, vf_ref, gate_ref, tri_ref, blk_ref, seg_ref, mu_ref, w0_ref, w2_ref, a0_ref, a2_ref,
         v0_ref, v2_ref, kk_ref, ka_ref, rk_ref, lnw_ref, lnb_ref, y_ref) = refs[:-n_sc]
    else:
        (p_ref, gate_ref, tri_ref, blk_ref, seg_ref, mu_ref, w0_ref, w2_ref, a0_ref, a2_ref,
         kk_ref, ka_ref, rk_ref, lnw_ref, lnb_ref, y_ref, vfo_ref) = refs[:-n_sc]
    (state_ref, carry_ref, at_ref, rt_ref, bt_ref, kt_ref, vv_ref, bw_ref, kw_ref, tot_ref, yy_ref,
     lhs_ref, xav_ref, ay_ref, zt_ref, wlt_ref) = refs[-n_sc:]

    @pl.when(pl.program_id(1) == 0)
    def _():
        state_ref[...] = jnp.zeros_like(state_ref)
        carry_ref[...] = jnp.zeros_like(carry_ref)

    seg = seg_ref[...]

    def segsum(x):
        hi, lo = _split_hi_lo(x)
        return _dot(hi, seg) + _dot(lo, seg)

    p = p_ref[...]
    prev = pltpu.roll(p, 1, 0)
    top = jnp.where(lax.broadcasted_iota(jnp.int32, (SUBLANE, 1), 0) == 0, carry_ref[...], prev[0:SUBLANE])
    prev = jnp.concatenate([top, prev[SUBLANE:]], axis=0)
    carry_ref[...] = p[tb - 1:tb, :]
    xs = p + (prev - p) * mu_ref[...]
    r = xs[:, 0:RW_W]
    k = xs[:, RW_W:2 * RW_W]
    v = xs[:, 2 * RW_W:3 * RW_W]
    hwa = xs[:, 3 * RW_W:3 * RW_W + LANE]
    lw = -math.exp(-0.5) * _sigmoid(w0_ref[...] + _dot(jnp.tanh(hwa).astype(BF16), w2_ref[...]))
    a = _sigmoid(a0_ref[...] + _dot(hwa.astype(BF16), a2_ref[...]))
    if has_vres:
        hv = xs[:, RW_SHIFT_BASE:RW_SHIFT_BASE + LANE]
        v = v + (vf_ref[...] - v) * _sigmoid(v0_ref[...] + _dot(hv.astype(BF16), v2_ref[...]))
    else:
        vfo_ref[...] = v
    kk = k * kk_ref[...]
    kk = kk * lax.rsqrt(jnp.maximum(segsum(kk * kk), 1e-24))
    k = k * (1.0 + (a - 1.0) * ka_ref[...])

    lw_hi, lw_lo = _split_hi_lo(lw)
    cum = _dot(tri_ref[...], lw_hi) + _dot(tri_ref[...], lw_lo)
    tot = _dot(blk_ref[...], lw_hi) + _dot(blk_ref[...], lw_lo)
    w_inv = jnp.exp(-cum)
    w_end = jnp.exp(tot - cum)
    kka = kk * a
    rt_ref[...] = (r * jnp.exp(cum)).astype(BF16)
    at_ref[...] = (-kk * jnp.exp(cum - lw)).astype(BF16)
    bt_ref[...] = (kka * w_inv).astype(BF16)
    kt_ref[...] = (k * w_inv).astype(BF16)
    vv_ref[...] = v.astype(BF16)
    bw_ref[...] = kka * w_end
    kw_ref[...] = k * w_end
    tot_ref[...] = tot

    c = RW_CHUNK
    n_pairs = RW_HEADS // 2
    lane_c = lax.broadcasted_iota(jnp.int32, (c, LANE), 1)
    even_c = lane_c < RW_N
    even_2c = lax.broadcasted_iota(jnp.int32, (2 * c, LANE), 1) < RW_N
    ri = lax.broadcasted_iota(jnp.int32, (2 * c, 2 * c), 0)
    ci = lax.broadcasted_iota(jnp.int32, (2 * c, 2 * c), 1)
    cm = jnp.where(ci >= c, ci - c, ci)
    keep = ((ri >= c) & (cm <= ri - c)) | ((ri < c) & (cm < ri))
    same_head = (ri < c) == (ci < c)
    eye2 = (lax.broadcasted_iota(jnp.int32, (c, LANE), 0)
            == jnp.where(lane_c >= c, lane_c - c, lane_c)).astype(F32)

    def bdiag(m):
        zero = jnp.zeros_like(m)
        return jnp.concatenate([jnp.where(even_c, m, zero), jnp.where(even_c, zero, m)], axis=0)

    def adiag(m):
        zero = jnp.zeros_like(m)
        return jnp.concatenate([jnp.where(even_c, zero, m), jnp.where(even_c, m, zero)], axis=0)

    def phase_a(i, _):
        probs = [(i * cpi + cc, pr) for cc in range(cpi) for pr in range(n_pairs)]
        rows = [pl.ds(pl.multiple_of(ic * c, c), c) for ic, _ in probs]
        rows2 = [pl.ds(pl.multiple_of(ic * 2 * c, 2 * c), 2 * c) for ic, _ in probs]
        ls = [slice(pr * LANE, (pr + 1) * LANE) for _, pr in probs]
        rng = range(len(probs))
        at = [at_ref[rows[j], ls[j]] for j in rng]
        rt = [rt_ref[rows[j], ls[j]] for j in rng]
        bt = [bt_ref[rows[j], ls[j]] for j in rng]
        kt = [kt_ref[rows[j], ls[j]] for j in rng]
        vh = [vv_ref[rows[j], ls[j]] for j in rng]
        xar = [jnp.concatenate([at[j], rt[j]], axis=0) for j in rng]
        zero_b = jnp.zeros((2 * c, LANE), BF16)
        aa_e = [jnp.where(keep, _dot_nt(jnp.where(even_2c, xar[j], zero_b),
                                        jnp.concatenate([bt[j], kt[j]], axis=0)), 0.0) for j in rng]
        aa_o = [jnp.where(keep, _dot_nt(jnp.where(even_2c, zero_b, xar[j]),
                                        jnp.concatenate([kt[j], bt[j]], axis=0)), 0.0) for j in rng]
        nn = [jnp.where(even_c, aa_e[j][0:c], aa_o[j][0:c]) for j in rng]
        aak = [jnp.where(even_c, aa_o[j][0:c], aa_e[j][0:c]) for j in rng]
        for j in rng:
            pr = probs[j][1]
            ay_ref[rows[j], 2 * pr * LANE:(2 * pr + 1) * LANE] = jnp.where(
                even_c, aa_e[j][c:2 * c], aa_o[j][c:2 * c]).astype(BF16)
            ay_ref[rows[j], (2 * pr + 1) * LANE:(2 * pr + 2) * LANE] = jnp.where(
                even_c, aa_o[j][c:2 * c], aa_e[j][c:2 * c]).astype(BF16)
        av = [_dot(aak[j].astype(BF16), adiag(vh[j])) for j in rng]
        x = [eye2 + nn[j] for j in rng]
        pw = [nn[j].astype(BF16) for j in rng]
        pw = [_dot(pw[j], bdiag(pw[j])).astype(BF16) for j in rng]
        n_sq = int(math.log2(c)) - 1
        for it in range(n_sq):
            last = it == n_sq - 1
            lhs = [x[j].astype(BF16) if last else jnp.concatenate([x[j].astype(BF16), pw[j]], axis=0) for j in rng]
            out = [_dot(lhs[j], bdiag(pw[j])) for j in rng]
            x = [x[j] + out[j][0:c] for j in rng]
            if not last:
                pw = [out[j][c:2 * c].astype(BF16) for j in rng]
        fin = [_dot(x[j].astype(BF16), jnp.concatenate([bdiag(at[j]), bdiag(av[j].astype(BF16))], axis=1))
               for j in rng]
        for j in rng:
            lhs_ref[rows2[j], ls[j]] = jnp.concatenate([fin[j][:, 0:LANE].astype(BF16), rt[j]], axis=0)
            xav_ref[rows[j], ls[j]] = fin[j][:, LANE:2 * LANE]
            zz = jnp.concatenate([bw_ref[rows[j], ls[j]], kw_ref[rows[j], ls[j]]], axis=0)
            zt_ref[rows2[j], ls[j]] = zz.T.astype(BF16)
            tot_row = tot_ref[pl.ds(probs[j][0] * c, 1), ls[j]]
            wlt_ref[rows2[j], ls[j]] = jnp.exp(jnp.broadcast_to(tot_row, (2 * c, LANE)).T)
        return 0

    lax.fori_loop(0, tb // (c * cpi), phase_a, 0)

    def phase_b(ic, _):
        rows = pl.ds(pl.multiple_of(ic * c, c), c)
        rows2 = pl.ds(pl.multiple_of(ic * 2 * c, 2 * c), 2 * c)
        ls = [slice(pr * LANE, (pr + 1) * LANE) for pr in range(n_pairs)]
        rng = range(n_pairs)
        mt = [state_ref[pr] for pr in rng]
        out = [_dot(lhs_ref[rows2, ls[pr]], mt[pr].astype(BF16)) for pr in rng]
        ub = [(out[pr][0:c] + xav_ref[rows, ls[pr]]).astype(BF16) for pr in rng]
        vh = [vv_ref[rows, ls[pr]] for pr in rng]
        ft = [_dot(zt_ref[rows2, ls[pr]], jnp.concatenate([ub[pr], vh[pr]], axis=0)) for pr in rng]
        for pr in rng:
            state_ref[pr] = mt[pr] * wlt_ref[rows2, ls[pr]] + jnp.where(same_head, ft[pr], 0.0)
        for pr in rng:
            uv = jnp.concatenate([bdiag(ub[pr]), adiag(vh[pr])], axis=0)
            yy_ref[rows, ls[pr]] = out[pr][c:2 * c] + _dot(ay_ref[rows, 2 * pr * LANE:(2 * pr + 2) * LANE], uv)
        return 0

    lax.fori_loop(0, tb // c, phase_b, 0)

    y = yy_ref[...]
    inv_n = 1.0 / RW_N
    dlt = y - segsum(y) * inv_n
    var = segsum(dlt * dlt) * inv_n
    y = dlt * lax.rsqrt(var + RW_GN_EPS) * lnw_ref[...] + lnb_ref[...]
    y = y + segsum(r * k * rk_ref[...]) * v
    y_ref[...] = (y * _silu(gate_ref[...].astype(F32))).astype(BF16)


def _rwkv(p, vfirst, gate, consts, params, *, batch, seq, tb, cpi):
    has_vres = vfirst is not None
    nb = seq // tb
    cw = p.shape[1]
    row = lambda b, i: (b * nb + i, 0)
    full = lambda a: pl.BlockSpec(a.shape, lambda b, i: (0,) * a.ndim)
    ins = [p] + ([vfirst] if has_vres else []) + [gate] + list(consts) + list(params)
    in_specs = ([pl.BlockSpec((tb, cw), row)]
                + ([pl.BlockSpec((tb, RW_W), row)] if has_vres else [])
                + [pl.BlockSpec((tb, RW_W), row)]
                + [full(a) for a in consts] + [full(a) for a in params])
    out_specs = [pl.BlockSpec((tb, RW_W), row)]
    out_shape = [jax.ShapeDtypeStruct((batch * seq, RW_W), BF16)]
    if not has_vres:
        out_specs.append(pl.BlockSpec((tb, RW_W), row))
        out_shape.append(jax.ShapeDtypeStruct((batch * seq, RW_W), F32))
    vm = lambda rows, cols, dt: pltpu.VMEM((rows, cols), dt)
    scratch = [pltpu.VMEM((RW_HEADS // 2, LANE, LANE), F32), vm(1, cw, F32),
               vm(tb, RW_W, BF16), vm(tb, RW_W, BF16), vm(tb, RW_W, BF16), vm(tb, RW_W, BF16), vm(tb, RW_W, BF16),
               vm(tb, RW_W, F32), vm(tb, RW_W, F32), vm(tb, RW_W, F32), vm(tb, RW_W, F32),
               vm(2 * tb, RW_W, BF16), vm(tb, RW_W, F32), vm(tb, 2 * RW_W, BF16),
               vm(2 * tb, RW_W, BF16), vm(2 * tb, RW_W, F32)]
    outs = pl.pallas_call(
        functools.partial(_rwkv_kernel, tb=tb, has_vres=has_vres, cpi=cpi),
        grid=(batch, nb),
        in_specs=in_specs, out_specs=out_specs, out_shape=out_shape,
        scratch_shapes=scratch,
        compiler_params=pltpu.CompilerParams(
            dimension_semantics=("parallel", "arbitrary"), vmem_limit_bytes=VMEM_LIMIT),
        name="rwkv7",
    )(*ins)
    return outs if not has_vres else (outs[0], None)


def _outproj_kernel(x_ref, om_ref, od_ref, orw_ref, w_ref, fg_ref, o_ref, *, final):
    acc = (_dot(om_ref[...], w_ref[0:MLA_W, :])
           + _dot(od_ref[...], w_ref[MLA_W:MLA_W + DIFF_W, :])
           + _dot(orw_ref[...], w_ref[MLA_W + DIFF_W:, :]))
    y = x_ref[...] + acc
    if final:
        y = _rms(y, fg_ref[...], NORM_EPS)
    o_ref[...] = y


def _outproj(x2, om, od, orw, w, fg, *, tm, final):
    t = x2.shape[0]
    row = lambda i: (i, 0)
    const = lambda i: (0, 0)
    return pl.pallas_call(
        functools.partial(_outproj_kernel, final=final),
        grid=(t // tm,),
        in_specs=[pl.BlockSpec((tm, D_MODEL), row), pl.BlockSpec((tm, MLA_W), row),
                  pl.BlockSpec((tm, DIFF_W), row), pl.BlockSpec((tm, RW_W), row),
                  pl.BlockSpec(w.shape, const), pl.BlockSpec(fg.shape, const)],
        out_specs=pl.BlockSpec((tm, D_MODEL), row),
        out_shape=jax.ShapeDtypeStruct((t, D_MODEL), F32),
        compiler_params=pltpu.CompilerParams(
            dimension_semantics=("parallel",), vmem_limit_bytes=VMEM_LIMIT),
        name="outproj",
    )(x2, om, od, orw, w, fg)


def _pack_inproj(w_in, w_vres):
    d = w_in.shape[0]
    o = 0
    cq = w_in[:, o:o + MLA_Q_RANK]; o += MLA_Q_RANK
    ckv = w_in[:, o:o + MLA_KV_RANK]; o += MLA_KV_RANK
    kpe = w_in[:, o:o + MLA_ROPE]; o += MLA_ROPE
    dqk = w_in[:, o:o + DIFF_G]; o += DIFF_G
    dv = w_in[:, o:o + DIFF_W]; o += DIFF_W
    gate = w_in[:, o:o + D_MIX]; o += D_MIX
    rw = w_in[:, o:]
    half = MLA_ROPE // 2
    z = lambda n: jnp.zeros((d, n), w_in.dtype)
    kp = jnp.concatenate([z(MLA_NOPE), kpe, z(HP - MLA_QK)], axis=1)
    kps = jnp.concatenate([z(MLA_NOPE), kpe[:, half:], kpe[:, :half], z(HP - MLA_QK)], axis=1)
    g_mla, g_diff, g_rw = gate[:, :MLA_W], gate[:, MLA_W:MLA_W + DIFF_W], gate[:, MLA_W + DIFF_W:]
    cols = [cq, ckv, kp, kps, dqk, g_rw, g_mla, g_diff, rw]
    if w_vres is not None:
        cols += [w_vres, z(LANE - RW_MV_RANK)]
    return jnp.concatenate(cols, axis=1).astype(BF16), dv.T.astype(BF16)


def _pack_mla(w_uq, w_ukv):
    rq, rkv = w_uq.shape[0], w_ukv.shape[0]
    half = MLA_ROPE // 2
    q3 = w_uq.reshape(rq, MLA_HEADS, MLA_QK)
    zq = lambda n: jnp.zeros((rq, MLA_HEADS, n), w_uq.dtype)
    wq = jnp.concatenate([q3, zq(HP - MLA_QK)], axis=2)
    wqs = jnp.concatenate([zq(MLA_NOPE), q3[:, :, MLA_NOPE + half:], q3[:, :, MLA_NOPE:MLA_NOPE + half],
                           zq(HP - MLA_QK)], axis=2)
    kv3 = w_ukv.reshape(rkv, MLA_HEADS, MLA_NOPE + MLA_V)
    wk = jnp.concatenate([kv3[:, :, :MLA_NOPE], jnp.zeros((rkv, MLA_HEADS, HP - MLA_NOPE), w_ukv.dtype)], axis=2)
    wv = kv3[:, :, MLA_NOPE:]
    flat = lambda a: a.reshape(a.shape[0], -1).astype(BF16)
    return flat(wq), flat(wqs), flat(wk), flat(wv).T


def _pad_rows(w, top, total):
    return jnp.concatenate([jnp.zeros((top, w.shape[1]), w.dtype), w,
                            jnp.zeros((total - top - w.shape[0], w.shape[1]), w.dtype)], axis=0).astype(BF16)


def _rwkv_consts(tb):
    t = jnp.arange(tb)
    same = (t[:, None] // RW_CHUNK) == (t[None, :] // RW_CHUNK)
    tri = (same & (t[None, :] <= t[:, None])).astype(BF16)
    blk = same.astype(BF16)
    hl = jnp.arange(RW_W) // RW_N
    seg = (hl[:, None] == hl[None, :]).astype(BF16)
    return tri, blk, seg


def kernel(x, positions, pre_g, w_in, w_in_vres, w_out, mla_gq, mla_gkv, mla_wuq, mla_wukv, diff_lam, diff_gsub, rw_mu, rw_mu_vres, rw_w0, rw_w2, rw_a0, rw_a2, rw_v0, rw_v2, rw_kk, rw_ka, rw_rk, rw_lnw, rw_lnb, final_g):
    batch, seq, _ = x.shape
    depth = pre_g.shape[0]
    tm = min(512, seq)
    tq = min(256, seq)
    tq_mla = min(512, seq)
    tb = min(256, seq)
    assert seq % tm == 0 and seq % tq_mla == 0 and tq_mla == 2 * tq and tm % tq == 0
    assert seq % tb == 0 and tb % RW_CHUNK == 0

    x2 = x.reshape(batch * seq, D_MODEL)
    tab = _rope_table(positions)
    pos_f = positions.astype(F32)
    posc = pos_f.reshape(seq, 1)
    posr = pos_f.reshape(seq // tq, 1, tq)
    consts = _rwkv_consts(tb)
    r1 = lambda a: a.reshape(1, -1).astype(F32)

    vfirst = None
    for layer in range(depth):
        vres = layer > 0
        w, wdvt = _pack_inproj(w_in[layer], w_in_vres[layer - 1] if vres else None)
        wq, wqs, wk, wvt = _pack_mla(mla_wuq[layer], mla_wukv[layer])
        q, k, vt, d, dvt, gate, rw = _inproj(x2, r1(pre_g[layer]), w, tab, r1(mla_gq[layer]), r1(mla_gkv[layer]),
                                             wq, wqs, wk, wvt, wdvt, seq=seq, tm=tm, tkv=tq)
        o_mla = _mla_flash(q, k, vt, gate, batch=batch, seq=seq, tq=tq_mla, n_h=MLA_HEADS)
        lam_init = 0.8 - 0.6 * math.exp(-0.3 * (layer + 1))
        gsub_col = diff_gsub[layer].reshape(-1, 1).astype(F32)
        o_diff = _diff_flash(d, dvt, gate, posc, posr, diff_lam[layer].astype(F32), gsub_col,
                             batch=batch, seq=seq, tq=tq, lam_init=lam_init)
        mu = rw_mu[layer]
        if vres:
            mu = jnp.concatenate([mu, rw_mu_vres[layer - 1], jnp.zeros((LANE - RW_MV_RANK,), mu.dtype)])
        params = [r1(mu), r1(rw_w0[layer]), _pad_rows(rw_w2[layer], 0, LANE), r1(rw_a0[layer]),
                  _pad_rows(rw_a2[layer], RW_DECAY_RANK, LANE)]
        if vres:
            params += [r1(rw_v0[layer - 1]), _pad_rows(rw_v2[layer - 1], 0, LANE)]
        params += [r1(rw_kk[layer]), r1(rw_ka[layer]), r1(rw_rk[layer]), r1(rw_lnw[layer]), r1(rw_lnb[layer])]
        o_rw, vf = _rwkv(rw, vfirst, gate, consts, params, batch=batch, seq=seq, tb=tb, cpi=4)
        if not vres:
            vfirst = vf
        x2 = _outproj(x2, o_mla, o_diff, o_rw, w_out[layer].astype(BF16), r1(final_g),
                      tm=tm, final=(layer == depth - 1))
    return x2.reshape(batch, seq, D_MODEL)
```

```python
import functools
import math

import jax
import jax.numpy as jnp
from jax import lax
from jax.experimental import pallas as pl
from jax.experimental.pallas import tpu as pltpu

F32 = jnp.float32
BF16 = jnp.bfloat16

D_MODEL = 1024
D_MIX = 1024
NORM_EPS = 1e-6
MLA_HEADS, MLA_NOPE, MLA_ROPE, MLA_V = 6, 64, 32, 64
MLA_Q_RANK, MLA_KV_RANK = 256, 128
MLA_QK = MLA_NOPE + MLA_ROPE
ROPE_THETA = 10000.0
MLA_W = MLA_HEADS * MLA_V
DIFF_HEADS, DIFF_DH = 4, 32
DIFF_W = DIFF_HEADS * 2 * DIFF_DH
DIFF_SUBLN_EPS = 1e-5
RW_HEADS, RW_N = 6, 64
RW_W = RW_HEADS * RW_N
RW_DECAY_RANK, RW_AAA_RANK, RW_MV_RANK = 64, 64, 32
RW_GN_EPS = 64e-5
RW_SHIFT_BASE = 3 * RW_W + RW_DECAY_RANK + RW_AAA_RANK

LANE = 128
SUBLANE = 8
BF16_ROWS = 16
LOG2E = math.log2(math.e)
HP = LANE
MLA_VA = MLA_V + BF16_ROWS
DIFF_VA = 2 * DIFF_DH + BF16_ROWS
RW_CHUNK = 64

MLA_G = MLA_Q_RANK + MLA_KV_RANK + 2 * HP
DIFF_G = 2 * DIFF_W
GATE_G = D_MIX
RW_G0 = RW_SHIFT_BASE
RW_G1 = RW_SHIFT_BASE + LANE
OFF_DIFF = MLA_G
OFF_GATE = OFF_DIFF + DIFF_G
OFF_RW = OFF_GATE + GATE_G

VMEM_LIMIT = 52 * 1024 * 1024

_NT = (((1,), (1,)), ((), ()))
_TN = (((0,), (0,)), ((), ()))


def _dot(a, b):
    return jnp.dot(a, b, preferred_element_type=F32)


def _dot_nt(a, b):
    return lax.dot_general(a, b, _NT, preferred_element_type=F32)


def _split_hi_lo(x):
    hi = x.astype(BF16)
    return hi, (x - hi.astype(F32)).astype(BF16)


def _sigmoid(z):
    return 1.0 / (1.0 + jnp.exp(-z))


def _silu(g):
    return g * _sigmoid(g)


def _rms(x, g, eps):
    return x * lax.rsqrt(jnp.mean(x * x, axis=-1, keepdims=True) + eps) * g


def _rope_table_kernel(pos_ref, tab_ref):
    pos = pos_ref[...]
    lane = lax.broadcasted_iota(jnp.int32, (1, LANE), 1)
    half = MLA_ROPE // 2
    in_rope = (lane >= MLA_NOPE) & (lane < MLA_QK)
    first = lane < MLA_NOPE + half
    idx = jnp.where(first, lane - MLA_NOPE, lane - MLA_NOPE - half).astype(F32)
    inv = jnp.exp(idx * (-math.log(ROPE_THETA) / half))
    ang = pos * inv
    cos = jnp.where(in_rope, jnp.cos(ang), jnp.where(lane < MLA_NOPE, 1.0, 0.0))
    sin = jnp.sin(ang)
    sin = jnp.where(in_rope, jnp.where(first, -sin, sin), 0.0)
    scale = MLA_QK ** -0.5 * LOG2E
    tab_ref[:, 0 * LANE:1 * LANE] = cos * scale
    tab_ref[:, 1 * LANE:2 * LANE] = sin * scale
    tab_ref[:, 2 * LANE:3 * LANE] = cos
    tab_ref[:, 3 * LANE:4 * LANE] = sin


def _rope_table(positions):
    s = positions.shape[0]
    return pl.pallas_call(
        _rope_table_kernel,
        out_shape=jax.ShapeDtypeStruct((s, 4 * LANE), F32),
        name="rope_table",
    )(positions.astype(F32).reshape(s, 1))


def _inproj_kernel(x_ref, g_ref, w_ref, tab_ref, gq_ref, gkv_ref, wq_ref, wqs_ref, wk_ref, wvt_ref, wdvt_ref,
                   vone_ref, dvone_ref,
                   q_ref, k_ref, vt_ref, d_ref, dvt_ref, gate_ref, rw_ref, *, tkv):
    x = x_ref[...]
    h = _rms(x, g_ref[...], NORM_EPS).astype(BF16)

    mla = _dot(h, w_ref[:, 0:MLA_G])
    cqn = _rms(mla[:, 0:MLA_Q_RANK], gq_ref[...], NORM_EPS).astype(BF16)
    ckvn = _rms(mla[:, MLA_Q_RANK:MLA_Q_RANK + MLA_KV_RANK], gkv_ref[...], NORM_EPS).astype(BF16)
    o_kpe = MLA_Q_RANK + MLA_KV_RANK
    tab = tab_ref[...]
    cos_q, sin_q = tab[:, 0:LANE], tab[:, LANE:2 * LANE]
    cos_k, sin_k = tab[:, 2 * LANE:3 * LANE], tab[:, 3 * LANE:4 * LANE]
    kpe = mla[:, o_kpe:o_kpe + HP] * cos_k + mla[:, o_kpe + HP:o_kpe + 2 * HP] * sin_k
    qa = _dot(cqn, wq_ref[...])
    qs = _dot(cqn, wqs_ref[...])
    kn = _dot(ckvn, wk_ref[...])
    for hd in range(MLA_HEADS):
        sl = slice(hd * HP, (hd + 1) * HP)
        q_ref[:, sl] = (qa[:, sl] * cos_q + qs[:, sl] * sin_q).astype(BF16)
        k_ref[:, sl] = (kn[:, sl] + kpe).astype(BF16)
    vt = (_dot_nt(wvt_ref[...], ckvn) + vone_ref[...]).astype(BF16)
    dvt = (_dot_nt(wdvt_ref[...], h) + dvone_ref[...]).astype(BF16)
    for t in range(vt_ref.shape[0]):
        vt_ref[t] = vt[:, t * tkv:(t + 1) * tkv]
        dvt_ref[t] = dvt[:, t * tkv:(t + 1) * tkv]

    dd = _dot(h, w_ref[:, OFF_DIFF:OFF_GATE])
    d_ref[:, 0:DIFF_W] = (dd[:, 0:DIFF_W] * (DIFF_DH ** -0.5 * LOG2E)).astype(BF16)
    d_ref[:, DIFF_W:DIFF_G] = dd[:, DIFF_W:DIFF_G].astype(BF16)
    gate_ref[...] = _dot(h, w_ref[:, OFF_GATE:OFF_RW]).astype(BF16)
    rw_ref[...] = _dot(h, w_ref[:, OFF_RW:])


def _inproj(x2, g, w, tab, gq, gkv, wq, wqs, wk, wvt, wdvt, vone, dvone, *, seq, tm, tkv):
    t = x2.shape[0]
    ctot = w.shape[1]
    rw_g = ctot - OFF_RW
    n_seq_tiles = seq // tm
    kv_per_tile = tm // tkv
    row = lambda i: (i, 0)
    slab = lambda i: (i, 0, 0)
    const = lambda i: (0, 0)
    full = lambda a: pl.BlockSpec(a.shape, const)
    return pl.pallas_call(
        functools.partial(_inproj_kernel, tkv=tkv),
        grid=(t // tm,),
        in_specs=[
            pl.BlockSpec((tm, D_MODEL), row),
            full(g), full(w),
            pl.BlockSpec((tm, 4 * LANE), lambda i: (i % n_seq_tiles, 0)),
            full(gq), full(gkv), full(wq), full(wqs), full(wk), full(wvt), full(wdvt),
            full(vone), full(dvone),
        ],
        out_specs=[
            pl.BlockSpec((tm, MLA_HEADS * HP), row),
            pl.BlockSpec((tm, MLA_HEADS * HP), row),
            pl.BlockSpec((kv_per_tile, wvt.shape[0], tkv), slab),
            pl.BlockSpec((tm, DIFF_G), row),
            pl.BlockSpec((kv_per_tile, wdvt.shape[0], tkv), slab),
            pl.BlockSpec((tm, GATE_G), row),
            pl.BlockSpec((tm, rw_g), row),
        ],
        out_shape=[
            jax.ShapeDtypeStruct((t, MLA_HEADS * HP), BF16),
            jax.ShapeDtypeStruct((t, MLA_HEADS * HP), BF16),
            jax.ShapeDtypeStruct((t // tkv, wvt.shape[0], tkv), BF16),
            jax.ShapeDtypeStruct((t, DIFF_G), BF16),
            jax.ShapeDtypeStruct((t // tkv, wdvt.shape[0], tkv), BF16),
            jax.ShapeDtypeStruct((t, GATE_G), BF16),
            jax.ShapeDtypeStruct((t, rw_g), F32),
        ],
        compiler_params=pltpu.CompilerParams(
            dimension_semantics=("parallel",), vmem_limit_bytes=VMEM_LIMIT),
        name="inproj",
    )(x2, g, w, tab, gq, gkv, wq, wqs, wk, wvt, wdvt, vone, dvone)


def _softmax_step(s, m):
    m_new = jnp.maximum(m, jnp.max(s, axis=0, keepdims=True))
    return m_new, jnp.exp2(m - m_new), jnp.exp2(s - m_new).astype(BF16)


def _causal_t(n_keys, n_queries):
    return (lax.broadcasted_iota(jnp.int32, (n_keys, n_queries), 0)
            <= lax.broadcasted_iota(jnp.int32, (n_keys, n_queries), 1))


def _mla_flash_kernel(q_ref, k_ref, vt_ref, gate_ref, o_ref, *, tq, tk, n_h):
    i = pl.program_id(2)
    half = tq // 2
    pw = MLA_VA
    hs = [slice(hh * HP, (hh + 1) * HP) for hh in range(n_h)]
    ps = [slice(hh * pw, (hh + 1) * pw) for hh in range(n_h)]

    def tile(j, carry, qs, mask, n_slab):
        rows = pl.ds(pl.multiple_of(j * (n_slab * tk), n_slab * tk), n_slab * tk)
        s = [_dot_nt(k_ref[rows, hs[hh]], qs[hh]) for hh in range(n_h)]
        if mask is not None:
            s = [jnp.where(mask, x, -jnp.inf) for x in s]
        st = [_softmax_step(s[hh], carry[hh][0]) for hh in range(n_h)]
        vt = [vt_ref[j * n_slab + t] for t in range(n_slab)]
        pv = [sum(_dot(vt[t][ps[hh]], st[hh][2][t * tk:(t + 1) * tk]) for t in range(n_slab))
              for hh in range(n_h)]
        return tuple((st[hh][0], st[hh][1] * carry[hh][1] + pv[hh]) for hh in range(n_h))

    qs = [q_ref[:, hs[hh]] for hh in range(n_h)]
    one = (jnp.full((1, tq), -jnp.inf, F32), jnp.zeros((pw, tq), F32))
    carry = lax.fori_loop(0, i, lambda j, c: tile(j, c, qs, None, tq // tk), (one,) * n_h)
    carry = tile(2 * i, carry, qs, _causal_t(tk, tq), 1)
    late = tile(2 * i + 1, tuple(tuple(a[:, half:] for a in c) for c in carry),
                [q[half:] for q in qs], _causal_t(tk, half), 1)
    accs = [jnp.concatenate([carry[hh][1][:, :half], late[hh][1]], axis=1) for hh in range(n_h)]
    o = jnp.concatenate([a[0:MLA_V] / a[MLA_V:MLA_V + 1] for a in accs], axis=0).T
    o_ref[...] = (o * _silu(gate_ref[...].astype(F32))).astype(BF16)


def _mla_flash(q, k, vt, gate, *, batch, seq, tq, n_h):
    nq = seq // tq
    tk = tq // 2
    nkv = seq // tk
    ow = n_h * MLA_V
    vw = n_h * MLA_VA
    gate_blk0 = RW_W // ow
    return pl.pallas_call(
        functools.partial(_mla_flash_kernel, tq=tq, tk=tk, n_h=n_h),
        grid=(batch, MLA_HEADS // n_h, nq),
        in_specs=[
            pl.BlockSpec((tq, n_h * HP), lambda b, g, i: (b * nq + i, g)),
            pl.BlockSpec((seq, n_h * HP), lambda b, g, i: (b, g)),
            pl.BlockSpec((nkv, vw, tk), lambda b, g, i: (b, g, 0)),
            pl.BlockSpec((tq, ow), lambda b, g, i: (b * nq + i, gate_blk0 + g)),
        ],
        out_specs=pl.BlockSpec((tq, ow), lambda b, g, i: (b * nq + i, g)),
        out_shape=jax.ShapeDtypeStruct((batch * seq, MLA_W), BF16),
        compiler_params=pltpu.CompilerParams(
            dimension_semantics=("parallel", "parallel", "parallel"), vmem_limit_bytes=VMEM_LIMIT),
        name="mla_flash",
    )(q, k, vt, gate)


def _diff_flash_kernel(posc_ref, posr_ref, lam_ref, gsub_ref, q_ref, k_ref, vt_ref, gate_ref, o_ref, qm_ref, p_ref,
                       *, tq, lam_init):
    i = pl.program_id(1)
    n_maps = 2 * DIFF_HEADS
    q = q_ref[...]
    lane_q = lax.broadcasted_iota(jnp.int32, (tq, DIFF_W), 1)
    for hm in range(n_maps):
        own = (lane_q >= hm * DIFF_DH) & (lane_q < (hm + 1) * DIFF_DH)
        qm_ref[hm * tq:(hm + 1) * tq, :] = jnp.where(own, q, jnp.zeros_like(q))
    pos_q = posr_ref[i]
    slopes = [LOG2E * 2.0 ** (-8.0 * (h + 1) / DIFF_HEADS) for h in range(DIFF_HEADS)]
    dv = 2 * DIFF_DH
    va = DIFF_VA

    def tile(j, carry, mask):
        rows = pl.ds(pl.multiple_of(j * tq, tq), tq)
        kt = k_ref[rows, :]
        dist = jnp.abs(posc_ref[rows, :] - pos_q)
        bias = [slope * dist for slope in slopes]
        st = []
        for hm in range(n_maps):
            s = _dot_nt(kt, qm_ref[hm * tq:(hm + 1) * tq, :]) - bias[hm // 2]
            if mask is not None:
                s = jnp.where(mask, s, -jnp.inf)
            m, alpha, p = _softmax_step(s, carry[hm][0])
            p_ref[:, hm * tq:(hm + 1) * tq] = p
            st.append((m, alpha))
        vt = vt_ref[j]
        pv = [_dot(vt[h * va:(h + 1) * va], p_ref[:, 2 * h * tq:(2 * h + 2) * tq]) for h in range(DIFF_HEADS)]
        new = []
        for hm in range(n_maps):
            new.append((st[hm][0], st[hm][1] * carry[hm][1] + pv[hm // 2][:, (hm % 2) * tq:(hm % 2 + 1) * tq]))
        return tuple(new)

    one = (jnp.full((1, tq), -jnp.inf, F32), jnp.zeros((va, tq), F32))
    carry = lax.fori_loop(0, i, lambda j, c: tile(j, c, None), (one,) * n_maps)
    carry = tile(i, carry, _causal_t(tq, tq))

    lam = lam_ref[...]
    lam_full = (jnp.exp(jnp.sum(lam[0:1] * lam[1:2], axis=-1, keepdims=True))
                - jnp.exp(jnp.sum(lam[2:3] * lam[3:4], axis=-1, keepdims=True)) + lam_init)
    heads = []
    for h in range(DIFF_HEADS):
        a0, a1 = carry[2 * h][1], carry[2 * h + 1][1]
        o = a0[0:dv] / a0[dv:dv + 1] - lam_full * (a1[0:dv] / a1[dv:dv + 1])
        ms = jnp.mean(o * o, axis=0, keepdims=True)
        heads.append(o * lax.rsqrt(ms + DIFF_SUBLN_EPS) * gsub_ref[...] * (1.0 - lam_init))
    o = jnp.concatenate(heads, axis=0).T
    o_ref[...] = (o * _silu(gate_ref[...].astype(F32))).astype(BF16)


def _diff_flash(d, dvt, gate, posc, posr, lam, gsub_col, *, batch, seq, tq, lam_init):
    nq = seq // tq
    gate_blk = (RW_W + MLA_W) // DIFF_W
    n_maps = 2 * DIFF_HEADS
    return pl.pallas_call(
        functools.partial(_diff_flash_kernel, tq=tq, lam_init=lam_init),
        grid=(batch, nq),
        in_specs=[
            pl.BlockSpec(posc.shape, lambda b, i: (0, 0)),
            pl.BlockSpec(posr.shape, lambda b, i: (0, 0, 0)),
            pl.BlockSpec(lam.shape, lambda b, i: (0, 0)),
            pl.BlockSpec(gsub_col.shape, lambda b, i: (0, 0)),
            pl.BlockSpec((tq, DIFF_W), lambda b, i: (b * nq + i, 0)),
            pl.BlockSpec((seq, DIFF_W), lambda b, i: (b, 1)),
            pl.BlockSpec((nq, DIFF_HEADS * DIFF_VA, tq), lambda b, i: (b, 0, 0)),
            pl.BlockSpec((tq, DIFF_W), lambda b, i: (b * nq + i, gate_blk)),
        ],
        out_specs=pl.BlockSpec((tq, DIFF_W), lambda b, i: (b * nq + i, 0)),
        out_shape=jax.ShapeDtypeStruct((batch * seq, DIFF_W), BF16),
        scratch_shapes=[pltpu.VMEM((n_maps * tq, DIFF_W), BF16), pltpu.VMEM((tq, n_maps * tq), BF16)],
        compiler_params=pltpu.CompilerParams(
            dimension_semantics=("parallel", "parallel"), vmem_limit_bytes=VMEM_LIMIT),
        name="diff_flash",
    )(posc, posr, lam, gsub_col, d, d, dvt, gate)


def _rwkv_kernel(*refs, tb, has_vres, cpi):
    n_sc = 17
    if has_vres:
        (p_ref, vf_ref, gate_ref, tri_ref, blk_ref, seg_ref, mu_ref, w0_ref, w2_ref, a0_ref, a2_ref,
         v0_ref, v2_ref, kk_ref, ka_ref, rk_ref, lnw_ref, lnb_ref, y_ref) = refs[:-n_sc]
    else:
        (p_ref, gate_ref, tri_ref, blk_ref, seg_ref, mu_ref, w0_ref, w2_ref, a0_ref, a2_ref,
         kk_ref, ka_ref, rk_ref, lnw_ref, lnb_ref, y_ref, vfo_ref) = refs[:-n_sc]
    (state_ref, carry_ref, at_ref, rt_ref, bt_ref, kt_ref, vv_ref, bw_ref, kw_ref, tot_ref, yy_ref,
     lhs_ref, xav_ref, ay_ref, tc_ref, cc_ref, wlt_ref) = refs[-n_sc:]

    @pl.when(pl.program_id(1) == 0)
    def _():
        state_ref[...] = jnp.zeros_like(state_ref)
        carry_ref[...] = jnp.zeros_like(carry_ref)

    seg = seg_ref[...]
    pair_lanes = [slice(pr * LANE, (pr + 1) * LANE) for pr in range(RW_HEADS // 2)]

    def segsum(x):
        hi, lo = _split_hi_lo(x)
        return jnp.concatenate([_dot(hi[:, ls], seg) + _dot(lo[:, ls], seg) for ls in pair_lanes], axis=1)

    p = p_ref[...]
    prev = pltpu.roll(p, 1, 0)
    top = jnp.where(lax.broadcasted_iota(jnp.int32, (SUBLANE, 1), 0) == 0, carry_ref[...], prev[0:SUBLANE])
    prev = jnp.concatenate([top, prev[SUBLANE:]], axis=0)
    carry_ref[...] = p[tb - 1:tb, :]
    xs = p + (prev - p) * mu_ref[...]
    r = xs[:, 0:RW_W]
    k = xs[:, RW_W:2 * RW_W]
    v = xs[:, 2 * RW_W:3 * RW_W]
    hwa = xs[:, 3 * RW_W:3 * RW_W + LANE]
    lw = -math.exp(-0.5) * _sigmoid(w0_ref[...] + _dot(jnp.tanh(hwa).astype(BF16), w2_ref[...]))
    a = _sigmoid(a0_ref[...] + _dot(hwa.astype(BF16), a2_ref[...]))
    if has_vres:
        hv = xs[:, RW_SHIFT_BASE:RW_SHIFT_BASE + LANE]
        v = v + (vf_ref[...] - v) * _sigmoid(v0_ref[...] + _dot(hv.astype(BF16), v2_ref[...]))
    else:
        vfo_ref[...] = v
    kk = k * kk_ref[...]
    kk = kk * lax.rsqrt(jnp.maximum(segsum(kk * kk), 1e-24))
    k = k * (1.0 + (a - 1.0) * ka_ref[...])

    lw_hi, lw_lo = _split_hi_lo(lw)
    cum = _dot(tri_ref[...], lw_hi) + _dot(tri_ref[...], lw_lo)
    tot = _dot(blk_ref[...], lw_hi) + _dot(blk_ref[...], lw_lo)
    w_inv = jnp.exp(-cum)
    w_end = jnp.exp(tot - cum)
    kka = kk * a
    rt_ref[...] = (r * jnp.exp(cum)).astype(BF16)
    at_ref[...] = (-kk * jnp.exp(cum - lw)).astype(BF16)
    bt_ref[...] = (kka * w_inv).astype(BF16)
    kt_ref[...] = (k * w_inv).astype(BF16)
    vv_ref[...] = v.astype(BF16)
    bw_ref[...] = kka * w_end
    kw_ref[...] = k * w_end
    tot_ref[...] = tot

    c = RW_CHUNK
    n_pairs = RW_HEADS // 2
    lane_c = lax.broadcasted_iota(jnp.int32, (c, LANE), 1)
    even_c = lane_c < RW_N
    even_2c = lax.broadcasted_iota(jnp.int32, (2 * c, LANE), 1) < RW_N
    ri = lax.broadcasted_iota(jnp.int32, (2 * c, 2 * c), 0)
    ci = lax.broadcasted_iota(jnp.int32, (2 * c, 2 * c), 1)
    cm = jnp.where(ci >= c, ci - c, ci)
    keep = ((ri >= c) & (cm <= ri - c)) | ((ri < c) & (cm < ri))
    same_head = (ri < c) == (ci < c)
    eye2 = (lax.broadcasted_iota(jnp.int32, (c, LANE), 0)
            == jnp.where(lane_c >= c, lane_c - c, lane_c)).astype(F32)

    def bdiag(m):
        zero = jnp.zeros_like(m)
        return jnp.concatenate([jnp.where(even_c, m, zero), jnp.where(even_c, zero, m)], axis=0)

    def adiag(m):
        zero = jnp.zeros_like(m)
        return jnp.concatenate([jnp.where(even_c, zero, m), jnp.where(even_c, m, zero)], axis=0)

    def phase_a(i, _):
        probs = [(i * cpi + cc, pr) for cc in range(cpi) for pr in range(n_pairs)]
        rows = [pl.ds(pl.multiple_of(ic * c, c), c) for ic, _ in probs]
        rows2 = [pl.ds(pl.multiple_of(ic * 2 * c, 2 * c), 2 * c) for ic, _ in probs]
        ls = [slice(pr * LANE, (pr + 1) * LANE) for _, pr in probs]
        rng = range(len(probs))
        at = [at_ref[rows[j], ls[j]] for j in rng]
        rt = [rt_ref[rows[j], ls[j]] for j in rng]
        bt = [bt_ref[rows[j], ls[j]] for j in rng]
        kt = [kt_ref[rows[j], ls[j]] for j in rng]
        vh = [vv_ref[rows[j], ls[j]] for j in rng]
        xar = [jnp.concatenate([at[j], rt[j]], axis=0) for j in rng]
        zero_b = jnp.zeros((2 * c, LANE), BF16)
        aa_e = [jnp.where(keep, _dot_nt(jnp.where(even_2c, xar[j], zero_b),
                                        jnp.concatenate([bt[j], kt[j]], axis=0)), 0.0) for j in rng]
        aa_o = [jnp.where(keep, _dot_nt(jnp.where(even_2c, zero_b, xar[j]),
                                        jnp.concatenate([kt[j], bt[j]], axis=0)), 0.0) for j in rng]
        nn = [jnp.where(even_c, aa_e[j][0:c], aa_o[j][0:c]) for j in rng]
        aak = [jnp.where(even_c, aa_o[j][0:c], aa_e[j][0:c]) for j in rng]
        for j in rng:
            pr = probs[j][1]
            ay_ref[rows[j], 2 * pr * LANE:(2 * pr + 1) * LANE] = jnp.where(
                even_c, aa_e[j][c:2 * c], aa_o[j][c:2 * c]).astype(BF16)
            ay_ref[rows[j], (2 * pr + 1) * LANE:(2 * pr + 2) * LANE] = jnp.where(
                even_c, aa_o[j][c:2 * c], aa_e[j][c:2 * c]).astype(BF16)
        av = [_dot(aak[j].astype(BF16), adiag(vh[j])) for j in rng]
        x = [eye2 + nn[j] for j in rng]
        pw = [nn[j].astype(BF16) for j in rng]
        pw = [_dot(pw[j], bdiag(pw[j])).astype(BF16) for j in rng]
        n_sq = int(math.log2(c)) - 1
        for it in range(n_sq):
            last = it == n_sq - 1
            lhs = [x[j].astype(BF16) if last else jnp.concatenate([x[j].astype(BF16), pw[j]], axis=0) for j in rng]
            out = [_dot(lhs[j], bdiag(pw[j])) for j in rng]
            x = [x[j] + out[j][0:c] for j in rng]
            if not last:
                pw = [out[j][c:2 * c].astype(BF16) for j in rng]
        fin = [_dot(x[j].astype(BF16), jnp.concatenate([bdiag(at[j]), bdiag(av[j].astype(BF16))], axis=1))
               for j in rng]
        ahat = [fin[j][:, 0:LANE].astype(BF16) for j in rng]
        xav = [fin[j][:, LANE:2 * LANE] for j in rng]
        zt = [jnp.concatenate([bw_ref[rows[j], ls[j]], kw_ref[rows[j], ls[j]]], axis=0).T.astype(BF16) for j in rng]
        zero_c = jnp.zeros((c, LANE), BF16)
        tcb = [_dot(zt[j], jnp.concatenate([ahat[j], zero_c], axis=0)) for j in rng]
        ccv = [_dot(zt[j], jnp.concatenate([xav[j].astype(BF16), vh[j]], axis=0)) for j in rng]
        for j in rng:
            lhs_ref[rows2[j], ls[j]] = jnp.concatenate([ahat[j], rt[j]], axis=0)
            xav_ref[rows[j], ls[j]] = xav[j]
            tc_ref[rows2[j], ls[j]] = jnp.where(same_head, tcb[j], 0.0).astype(BF16)
            cc_ref[rows2[j], ls[j]] = jnp.where(same_head, ccv[j], 0.0)
            tot_row = tot_ref[pl.ds(probs[j][0] * c, 1), ls[j]]
            wlt_ref[rows2[j], ls[j]] = jnp.exp(jnp.broadcast_to(tot_row, (2 * c, LANE)).T)
        return 0

    lax.fori_loop(0, tb // (c * cpi), phase_a, 0)

    ls = [slice(pr * LANE, (pr + 1) * LANE) for pr in range(n_pairs)]
    rng = range(n_pairs)
    mt = [state_ref[pr] for pr in rng]
    for ic in range(tb // c):
        rows = slice(ic * c, (ic + 1) * c)
        rows2 = slice(ic * 2 * c, (ic + 1) * 2 * c)
        mtb = [mt[pr].astype(BF16) for pr in rng]
        nxt = [_dot(tc_ref[rows2, ls[pr]], mtb[pr]) for pr in rng]
        mt = [mt[pr] * wlt_ref[rows2, ls[pr]] + nxt[pr] + cc_ref[rows2, ls[pr]] for pr in rng]
        out = [_dot(lhs_ref[rows2, ls[pr]], mtb[pr]) for pr in rng]
        ub = [(out[pr][0:c] + xav_ref[rows, ls[pr]]).astype(BF16) for pr in rng]
        vh = [vv_ref[rows, ls[pr]] for pr in rng]
        for pr in rng:
            uv = jnp.concatenate([bdiag(ub[pr]), adiag(vh[pr])], axis=0)
            yy_ref[rows, ls[pr]] = out[pr][c:2 * c] + _dot(ay_ref[rows, 2 * pr * LANE:(2 * pr + 2) * LANE], uv)
    for pr in rng:
        state_ref[pr] = mt[pr]

    y = yy_ref[...]
    inv_n = 1.0 / RW_N
    dlt = y - segsum(y) * inv_n
    var = segsum(dlt * dlt) * inv_n
    y = dlt * lax.rsqrt(var + RW_GN_EPS) * lnw_ref[...] + lnb_ref[...]
    y = y + segsum(r * k * rk_ref[...]) * v
    y_ref[...] = (y * _silu(gate_ref[...].astype(F32))).astype(BF16)


def _rwkv(p, vfirst, gate, consts, params, *, batch, seq, tb, cpi):
    has_vres = vfirst is not None
    nb = seq // tb
    cw = p.shape[1]
    row = lambda b, i: (b * nb + i, 0)
    full = lambda a: pl.BlockSpec(a.shape, lambda b, i: (0,) * a.ndim)
    ins = [p] + ([vfirst] if has_vres else []) + [gate] + list(consts) + list(params)
    in_specs = ([pl.BlockSpec((tb, cw), row)]
                + ([pl.BlockSpec((tb, RW_W), row)] if has_vres else [])
                + [pl.BlockSpec((tb, RW_W), row)]
                + [full(a) for a in consts] + [full(a) for a in params])
    out_specs = [pl.BlockSpec((tb, RW_W), row)]
    out_shape = [jax.ShapeDtypeStruct((batch * seq, RW_W), BF16)]
    if not has_vres:
        out_specs.append(pl.BlockSpec((tb, RW_W), row))
        out_shape.append(jax.ShapeDtypeStruct((batch * seq, RW_W), F32))
    vm = lambda rows, cols, dt: pltpu.VMEM((rows, cols), dt)
    scratch = [pltpu.VMEM((RW_HEADS // 2, LANE, LANE), F32), vm(1, cw, F32),
               vm(tb, RW_W, BF16), vm(tb, RW_W, BF16), vm(tb, RW_W, BF16), vm(tb, RW_W, BF16), vm(tb, RW_W, BF16),
               vm(tb, RW_W, F32), vm(tb, RW_W, F32), vm(tb, RW_W, F32), vm(tb, RW_W, F32),
               vm(2 * tb, RW_W, BF16), vm(tb, RW_W, F32), vm(tb, 2 * RW_W, BF16),
               vm(2 * tb, RW_W, BF16), vm(2 * tb, RW_W, F32), vm(2 * tb, RW_W, F32)]
    outs = pl.pallas_call(
        functools.partial(_rwkv_kernel, tb=tb, has_vres=has_vres, cpi=cpi),
        grid=(batch, nb),
        in_specs=in_specs, out_specs=out_specs, out_shape=out_shape,
        scratch_shapes=scratch,
        compiler_params=pltpu.CompilerParams(
            dimension_semantics=("parallel", "arbitrary"), vmem_limit_bytes=VMEM_LIMIT),
        name="rwkv7",
    )(*ins)
    return outs if not has_vres else (outs[0], None)


def _outproj_kernel(x_ref, om_ref, od_ref, orw_ref, w_ref, fg_ref, o_ref, *, final):
    acc = (_dot(om_ref[...], w_ref[0:MLA_W, :])
           + _dot(od_ref[...], w_ref[MLA_W:MLA_W + DIFF_W, :])
           + _dot(orw_ref[...], w_ref[MLA_W + DIFF_W:, :]))
    y = x_ref[...] + acc
    if final:
        y = _rms(y, fg_ref[...], NORM_EPS)
    o_ref[...] = y


def _outproj(x2, om, od, orw, w, fg, *, tm, final):
    t = x2.shape[0]
    row = lambda i: (i, 0)
    const = lambda i: (0, 0)
    return pl.pallas_call(
        functools.partial(_outproj_kernel, final=final),
        grid=(t // tm,),
        in_specs=[pl.BlockSpec((tm, D_MODEL), row), pl.BlockSpec((tm, MLA_W), row),
                  pl.BlockSpec((tm, DIFF_W), row), pl.BlockSpec((tm, RW_W), row),
                  pl.BlockSpec(w.shape, const), pl.BlockSpec(fg.shape, const)],
        out_specs=pl.BlockSpec((tm, D_MODEL), row),
        out_shape=jax.ShapeDtypeStruct((t, D_MODEL), F32),
        compiler_params=pltpu.CompilerParams(
            dimension_semantics=("parallel",), vmem_limit_bytes=VMEM_LIMIT),
        name="outproj",
    )(x2, om, od, orw, w, fg)


def _pack_inproj(w_in, w_vres):
    d = w_in.shape[0]
    o = 0
    cq = w_in[:, o:o + MLA_Q_RANK]; o += MLA_Q_RANK
    ckv = w_in[:, o:o + MLA_KV_RANK]; o += MLA_KV_RANK
    kpe = w_in[:, o:o + MLA_ROPE]; o += MLA_ROPE
    dqk = w_in[:, o:o + DIFF_G]; o += DIFF_G
    dv = w_in[:, o:o + DIFF_W]; o += DIFF_W
    gate = w_in[:, o:o + D_MIX]; o += D_MIX
    rw = w_in[:, o:]
    half = MLA_ROPE // 2
    z = lambda n: jnp.zeros((d, n), w_in.dtype)
    kp = jnp.concatenate([z(MLA_NOPE), kpe, z(HP - MLA_QK)], axis=1)
    kps = jnp.concatenate([z(MLA_NOPE), kpe[:, half:], kpe[:, :half], z(HP - MLA_QK)], axis=1)
    g_mla, g_diff, g_rw = gate[:, :MLA_W], gate[:, MLA_W:MLA_W + DIFF_W], gate[:, MLA_W + DIFF_W:]
    cols = [cq, ckv, kp, kps, dqk, g_rw, g_mla, g_diff, rw]
    if w_vres is not None:
        cols += [w_vres, z(LANE - RW_MV_RANK)]
    return jnp.concatenate(cols, axis=1).astype(BF16), _with_ones_rows(dv.T, DIFF_HEADS, 2 * DIFF_DH)


def _pack_mla(w_uq, w_ukv):
    rq, rkv = w_uq.shape[0], w_ukv.shape[0]
    half = MLA_ROPE // 2
    q3 = w_uq.reshape(rq, MLA_HEADS, MLA_QK)
    zq = lambda n: jnp.zeros((rq, MLA_HEADS, n), w_uq.dtype)
    wq = jnp.concatenate([q3, zq(HP - MLA_QK)], axis=2)
    wqs = jnp.concatenate([zq(MLA_NOPE), q3[:, :, MLA_NOPE + half:], q3[:, :, MLA_NOPE:MLA_NOPE + half],
                           zq(HP - MLA_QK)], axis=2)
    kv3 = w_ukv.reshape(rkv, MLA_HEADS, MLA_NOPE + MLA_V)
    wk = jnp.concatenate([kv3[:, :, :MLA_NOPE], jnp.zeros((rkv, MLA_HEADS, HP - MLA_NOPE), w_ukv.dtype)], axis=2)
    wv = kv3[:, :, MLA_NOPE:]
    flat = lambda a: a.reshape(a.shape[0], -1).astype(BF16)
    return flat(wq), flat(wqs), flat(wk), _with_ones_rows(flat(wv).T, MLA_HEADS, MLA_V)


def _with_ones_rows(wt, heads, width):
    w3 = wt.reshape(heads, width, wt.shape[1])
    w3 = jnp.concatenate([w3, jnp.zeros((heads, BF16_ROWS, wt.shape[1]), wt.dtype)], axis=1)
    ones = jnp.concatenate([jnp.zeros((heads, width, 1), F32), jnp.ones((heads, BF16_ROWS, 1), F32)], axis=1)
    return w3.reshape(heads * (width + BF16_ROWS), -1).astype(BF16), ones.reshape(-1, 1)


def _pad_rows(w, top, total):
    return jnp.concatenate([jnp.zeros((top, w.shape[1]), w.dtype), w,
                            jnp.zeros((total - top - w.shape[0], w.shape[1]), w.dtype)], axis=0).astype(BF16)


def _rwkv_consts(tb):
    t = jnp.arange(tb)
    same = (t[:, None] // RW_CHUNK) == (t[None, :] // RW_CHUNK)
    tri = (same & (t[None, :] <= t[:, None])).astype(BF16)
    blk = same.astype(BF16)
    hl = jnp.arange(LANE) // RW_N
    seg = (hl[:, None] == hl[None, :]).astype(BF16)
    return tri, blk, seg


def kernel(x, positions, pre_g, w_in, w_in_vres, w_out, mla_gq, mla_gkv, mla_wuq, mla_wukv, diff_lam, diff_gsub, rw_mu, rw_mu_vres, rw_w0, rw_w2, rw_a0, rw_a2, rw_v0, rw_v2, rw_kk, rw_ka, rw_rk, rw_lnw, rw_lnb, final_g):
    batch, seq, _ = x.shape
    depth = pre_g.shape[0]
    tm = min(512, seq)
    tq = min(256, seq)
    tq_mla = min(512, seq)
    tb = min(256, seq)
    assert seq % tm == 0 and seq % tq_mla == 0 and tq_mla == 2 * tq and tm % tq == 0
    assert seq % tb == 0 and tb % RW_CHUNK == 0

    x2 = x.reshape(batch * seq, D_MODEL)
    tab = _rope_table(positions)
    pos_f = positions.astype(F32)
    posc = pos_f.reshape(seq, 1)
    posr = pos_f.reshape(seq // tq, 1, tq)
    consts = _rwkv_consts(tb)
    r1 = lambda a: a.reshape(1, -1).astype(F32)

    vfirst = None
    for layer in range(depth):
        vres = layer > 0
        w, (wdvt, dvone) = _pack_inproj(w_in[layer], w_in_vres[layer - 1] if vres else None)
        wq, wqs, wk, (wvt, vone) = _pack_mla(mla_wuq[layer], mla_wukv[layer])
        q, k, vt, d, dvt, gate, rw = _inproj(x2, r1(pre_g[layer]), w, tab, r1(mla_gq[layer]), r1(mla_gkv[layer]),
                                             wq, wqs, wk, wvt, wdvt, vone, dvone,
                                             seq=seq, tm=tm, tkv=tq)
        o_mla = _mla_flash(q, k, vt, gate, batch=batch, seq=seq, tq=tq_mla, n_h=MLA_HEADS)
        lam_init = 0.8 - 0.6 * math.exp(-0.3 * (layer + 1))
        gsub_col = diff_gsub[layer].reshape(-1, 1).astype(F32)
        o_diff = _diff_flash(d, dvt, gate, posc, posr, diff_lam[layer].astype(F32), gsub_col,
                             batch=batch, seq=seq, tq=tq, lam_init=lam_init)
        mu = rw_mu[layer]
        if vres:
            mu = jnp.concatenate([mu, rw_mu_vres[layer - 1], jnp.zeros((LANE - RW_MV_RANK,), mu.dtype)])
        params = [r1(mu), r1(rw_w0[layer]), _pad_rows(rw_w2[layer], 0, LANE), r1(rw_a0[layer]),
                  _pad_rows(rw_a2[layer], RW_DECAY_RANK, LANE)]
        if vres:
            params += [r1(rw_v0[layer - 1]), _pad_rows(rw_v2[layer - 1], 0, LANE)]
        params += [r1(rw_kk[layer]), r1(rw_ka[layer]), r1(rw_rk[layer]), r1(rw_lnw[layer]), r1(rw_lnb[layer])]
        o_rw, vf = _rwkv(rw, vfirst, gate, consts, params, batch=batch, seq=seq, tb=tb, cpi=4)
        if not vres:
            vfirst = vf
        x2 = _outproj(x2, o_mla, o_diff, o_rw, w_out[layer].astype(BF16), r1(final_g),
                      tm=tm, final=(layer == depth - 1))
    return x2.reshape(batch, seq, D_MODEL)
```

```python
import functools
import math

import jax
import jax.numpy as jnp
from jax import lax
from jax.experimental import pallas as pl
from jax.experimental.pallas import tpu as pltpu

F32 = jnp.float32
BF16 = jnp.bfloat16

D_MODEL = 1024
D_MIX = 1024
NORM_EPS = 1e-6
MLA_HEADS, MLA_NOPE, MLA_ROPE, MLA_V = 6, 64, 32, 64
MLA_Q_RANK, MLA_KV_RANK = 256, 128
MLA_QK = MLA_NOPE + MLA_ROPE
ROPE_THETA = 10000.0
MLA_W = MLA_HEADS * MLA_V
DIFF_HEADS, DIFF_DH = 4, 32
DIFF_W = DIFF_HEADS * 2 * DIFF_DH
DIFF_SUBLN_EPS = 1e-5
RW_HEADS, RW_N = 6, 64
RW_W = RW_HEADS * RW_N
RW_DECAY_RANK, RW_AAA_RANK, RW_MV_RANK = 64, 64, 32
RW_GN_EPS = 64e-5
RW_SHIFT_BASE = 3 * RW_W + RW_DECAY_RANK + RW_AAA_RANK

LANE = 128
SUBLANE = 8
BF16_ROWS = 16
LOG2E = math.log2(math.e)
HP = LANE
MLA_VA = MLA_V + BF16_ROWS
DIFF_VA = 2 * DIFF_DH + BF16_ROWS
RW_CHUNK = 64
RW_N_SCRATCH = 17
RW_SKEW = 14

MLA_G = MLA_Q_RANK + MLA_KV_RANK + 2 * HP
DIFF_G = 2 * DIFF_W
GATE_G = D_MIX
RW_G0 = RW_SHIFT_BASE
RW_G1 = RW_SHIFT_BASE + LANE
OFF_DIFF = MLA_G
OFF_GATE = OFF_DIFF + DIFF_G
OFF_RW = OFF_GATE + GATE_G

VMEM_LIMIT = 52 * 1024 * 1024

_NT = (((1,), (1,)), ((), ()))
_TN = (((0,), (0,)), ((), ()))


def _dot(a, b):
    return jnp.dot(a, b, preferred_element_type=F32)


def _dot_nt(a, b):
    return lax.dot_general(a, b, _NT, preferred_element_type=F32)


def _split_hi_lo(x):
    hi = x.astype(BF16)
    return hi, (x - hi.astype(F32)).astype(BF16)


def _sigmoid(z):
    return 1.0 / (1.0 + jnp.exp(-z))


def _silu(g):
    return g * _sigmoid(g)


def _rms(x, g, eps):
    return x * lax.rsqrt(jnp.mean(x * x, axis=-1, keepdims=True) + eps) * g


def _rope_table_kernel(pos_ref, tab_ref):
    pos = pos_ref[...]
    lane = lax.broadcasted_iota(jnp.int32, (1, LANE), 1)
    half = MLA_ROPE // 2
    in_rope = (lane >= MLA_NOPE) & (lane < MLA_QK)
    first = lane < MLA_NOPE + half
    idx = jnp.where(first, lane - MLA_NOPE, lane - MLA_NOPE - half).astype(F32)
    inv = jnp.exp(idx * (-math.log(ROPE_THETA) / half))
    ang = pos * inv
    cos = jnp.where(in_rope, jnp.cos(ang), jnp.where(lane < MLA_NOPE, 1.0, 0.0))
    sin = jnp.sin(ang)
    sin = jnp.where(in_rope, jnp.where(first, -sin, sin), 0.0)
    scale = MLA_QK ** -0.5 * LOG2E
    tab_ref[:, 0 * LANE:1 * LANE] = cos * scale
    tab_ref[:, 1 * LANE:2 * LANE] = jnp.where(first, sin, 0.0) * scale
    tab_ref[:, 2 * LANE:3 * LANE] = jnp.where(first, 0.0, sin) * scale
    tab_ref[:, 3 * LANE:4 * LANE] = cos
    tab_ref[:, 4 * LANE:5 * LANE] = sin


def _rope_table(positions):
    s = positions.shape[0]
    return pl.pallas_call(
        _rope_table_kernel,
        out_shape=jax.ShapeDtypeStruct((s, 5 * LANE), F32),
        name="rope_table",
    )(positions.astype(F32).reshape(s, 1))


def _inproj_kernel(x_ref, g_ref, w_ref, tab_ref, gq_ref, gkv_ref, wq_ref, wk_ref, wvt_ref, wdvt_ref,
                   vone_ref, dvone_ref,
                   q_ref, k_ref, vt_ref, d_ref, dvt_ref, gate_ref, rw_ref, *, tkv):
    x = x_ref[...]
    h = _rms(x, g_ref[...], NORM_EPS).astype(BF16)

    mla = _dot(h, w_ref[:, 0:MLA_G])
    cqn = _rms(mla[:, 0:MLA_Q_RANK], gq_ref[...], NORM_EPS).astype(BF16)
    ckvn = _rms(mla[:, MLA_Q_RANK:MLA_Q_RANK + MLA_KV_RANK], gkv_ref[...], NORM_EPS).astype(BF16)
    o_kpe = MLA_Q_RANK + MLA_KV_RANK
    tab = tab_ref[...]
    cos_q, sin_q1, sin_q2 = tab[:, 0:LANE], tab[:, LANE:2 * LANE], tab[:, 2 * LANE:3 * LANE]
    cos_k, sin_k = tab[:, 3 * LANE:4 * LANE], tab[:, 4 * LANE:5 * LANE]
    kpe = mla[:, o_kpe:o_kpe + HP] * cos_k + mla[:, o_kpe + HP:o_kpe + 2 * HP] * sin_k
    qa = _dot(cqn, wq_ref[...])
    kn = _dot(ckvn, wk_ref[...])
    half = MLA_ROPE // 2
    for hd in range(MLA_HEADS):
        sl = slice(hd * HP, (hd + 1) * HP)
        qh = qa[:, sl]
        q_ref[:, sl] = (qh * cos_q + pltpu.roll(qh, HP - half, 1) * sin_q1
                        + pltpu.roll(qh, half, 1) * sin_q2).astype(BF16)
        k_ref[:, sl] = (kn[:, sl] + kpe).astype(BF16)
    vt = (_dot_nt(wvt_ref[...], ckvn) + vone_ref[...]).astype(BF16)
    dvt = (_dot_nt(wdvt_ref[...], h) + dvone_ref[...]).astype(BF16)
    for t in range(vt_ref.shape[0]):
        vt_ref[t] = vt[:, t * tkv:(t + 1) * tkv]
        dvt_ref[t] = dvt[:, t * tkv:(t + 1) * tkv]

    dd = _dot(h, w_ref[:, OFF_DIFF:OFF_GATE])
    d_ref[:, 0:DIFF_W] = (dd[:, 0:DIFF_W] * (DIFF_DH ** -0.5 * LOG2E)).astype(BF16)
    d_ref[:, DIFF_W:DIFF_G] = dd[:, DIFF_W:DIFF_G].astype(BF16)
    gate_ref[...] = _dot(h, w_ref[:, OFF_GATE:OFF_RW]).astype(BF16)
    rw_ref[...] = _dot(h, w_ref[:, OFF_RW:])


def _inproj(x2, g, w, tab, gq, gkv, wq, wk, wvt, wdvt, vone, dvone, *, seq, tm, tkv):
    t = x2.shape[0]
    ctot = w.shape[1]
    rw_g = ctot - OFF_RW
    n_seq_tiles = seq // tm
    kv_per_tile = tm // tkv
    row = lambda i: (i, 0)
    slab = lambda i: (i, 0, 0)
    const = lambda i: (0, 0)
    full = lambda a: pl.BlockSpec(a.shape, const)
    return pl.pallas_call(
        functools.partial(_inproj_kernel, tkv=tkv),
        grid=(t // tm,),
        in_specs=[
            pl.BlockSpec((tm, D_MODEL), row),
            full(g), full(w),
            pl.BlockSpec((tm, tab.shape[1]), lambda i: (i % n_seq_tiles, 0)),
            full(gq), full(gkv), full(wq), full(wk), full(wvt), full(wdvt),
            full(vone), full(dvone),
        ],
        out_specs=[
            pl.BlockSpec((tm, MLA_HEADS * HP), row),
            pl.BlockSpec((tm, MLA_HEADS * HP), row),
            pl.BlockSpec((kv_per_tile, wvt.shape[0], tkv), slab),
            pl.BlockSpec((tm, DIFF_G), row),
            pl.BlockSpec((kv_per_tile, wdvt.shape[0], tkv), slab),
            pl.BlockSpec((tm, GATE_G), row),
            pl.BlockSpec((tm, rw_g), row),
        ],
        out_shape=[
            jax.ShapeDtypeStruct((t, MLA_HEADS * HP), BF16),
            jax.ShapeDtypeStruct((t, MLA_HEADS * HP), BF16),
            jax.ShapeDtypeStruct((t // tkv, wvt.shape[0], tkv), BF16),
            jax.ShapeDtypeStruct((t, DIFF_G), BF16),
            jax.ShapeDtypeStruct((t // tkv, wdvt.shape[0], tkv), BF16),
            jax.ShapeDtypeStruct((t, GATE_G), BF16),
            jax.ShapeDtypeStruct((t, rw_g), F32),
        ],
        compiler_params=pltpu.CompilerParams(
            dimension_semantics=("parallel",), vmem_limit_bytes=VMEM_LIMIT),
        name="inproj",
    )(x2, g, w, tab, gq, gkv, wq, wk, wvt, wdvt, vone, dvone)


def _softmax_step(s, m):
    m_new = jnp.maximum(m, jnp.max(s, axis=0, keepdims=True))
    return m_new, jnp.exp2(m - m_new), jnp.exp2(s - m_new).astype(BF16)


def _causal_t(n_keys, n_queries):
    return (lax.broadcasted_iota(jnp.int32, (n_keys, n_queries), 0)
            <= lax.broadcasted_iota(jnp.int32, (n_keys, n_queries), 1))


def _mla_flash_kernel(q_ref, k_ref, vt_ref, gate_ref, o_ref, *, tq, tk, n_h):
    i = pl.program_id(2)
    half = tq // 2
    pw = MLA_VA
    hs = [slice(hh * HP, (hh + 1) * HP) for hh in range(n_h)]
    ps = [slice(hh * pw, (hh + 1) * pw) for hh in range(n_h)]

    def tile(j, carry, qs, mask, n_slab):
        rows = pl.ds(pl.multiple_of(j * (n_slab * tk), n_slab * tk), n_slab * tk)
        s = [_dot_nt(k_ref[rows, hs[hh]], qs[hh]) for hh in range(n_h)]
        if mask is not None:
            s = [jnp.where(mask, x, -jnp.inf) for x in s]
        st = [_softmax_step(s[hh], carry[hh][0]) for hh in range(n_h)]
        vt = [vt_ref[j * n_slab + t] for t in range(n_slab)]
        pv = [sum(_dot(vt[t][ps[hh]], st[hh][2][t * tk:(t + 1) * tk]) for t in range(n_slab))
              for hh in range(n_h)]
        return tuple((st[hh][0], st[hh][1] * carry[hh][1] + pv[hh]) for hh in range(n_h))

    qs = [q_ref[:, hs[hh]] for hh in range(n_h)]
    one = (jnp.full((1, tq), -jnp.inf, F32), jnp.zeros((pw, tq), F32))
    carry = lax.fori_loop(0, i, lambda j, c: tile(j, c, qs, None, tq // tk), (one,) * n_h)
    carry = tile(2 * i, carry, qs, _causal_t(tk, tq), 1)
    late = tile(2 * i + 1, tuple(tuple(a[:, half:] for a in c) for c in carry),
                [q[half:] for q in qs], _causal_t(tk, half), 1)
    accs = [jnp.concatenate([carry[hh][1][:, :half], late[hh][1]], axis=1) for hh in range(n_h)]
    o = jnp.concatenate([a[0:MLA_V] / a[MLA_V:MLA_V + 1] for a in accs], axis=0).T
    o_ref[...] = (o * _silu(gate_ref[...].astype(F32))).astype(BF16)


def _mla_flash(q, k, vt, gate, *, batch, seq, tq, n_h):
    nq = seq // tq
    tk = tq // 2
    nkv = seq // tk
    ow = n_h * MLA_V
    vw = n_h * MLA_VA
    gate_blk0 = RW_W // ow
    return pl.pallas_call(
        functools.partial(_mla_flash_kernel, tq=tq, tk=tk, n_h=n_h),
        grid=(batch, MLA_HEADS // n_h, nq),
        in_specs=[
            pl.BlockSpec((tq, n_h * HP), lambda b, g, i: (b * nq + i, g)),
            pl.BlockSpec((seq, n_h * HP), lambda b, g, i: (b, g)),
            pl.BlockSpec((nkv, vw, tk), lambda b, g, i: (b, g, 0)),
            pl.BlockSpec((tq, ow), lambda b, g, i: (b * nq + i, gate_blk0 + g)),
        ],
        out_specs=pl.BlockSpec((tq, ow), lambda b, g, i: (b * nq + i, g)),
        out_shape=jax.ShapeDtypeStruct((batch * seq, MLA_W), BF16),
        compiler_params=pltpu.CompilerParams(
            dimension_semantics=("parallel", "parallel", "parallel"), vmem_limit_bytes=VMEM_LIMIT),
        name="mla_flash",
    )(q, k, vt, gate)


def _diff_flash_kernel(posc_ref, posr_ref, lam_ref, gsub_ref, q_ref, k_ref, vt_ref, gate_ref, o_ref, qm_ref, p_ref,
                       *, tq, lam_init):
    i = pl.program_id(1)
    n_maps = 2 * DIFF_HEADS
    q = q_ref[...]
    lane_q = lax.broadcasted_iota(jnp.int32, (tq, DIFF_W), 1)
    for hm in range(n_maps):
        own = (lane_q >= hm * DIFF_DH) & (lane_q < (hm + 1) * DIFF_DH)
        qm_ref[hm * tq:(hm + 1) * tq, :] = jnp.where(own, q, jnp.zeros_like(q))
    pos_q = posr_ref[i]
    slopes = [LOG2E * 2.0 ** (-8.0 * (h + 1) / DIFF_HEADS) for h in range(DIFF_HEADS)]
    dv = 2 * DIFF_DH
    va = DIFF_VA

    def tile(j, carry, mask):
        rows = pl.ds(pl.multiple_of(j * tq, tq), tq)
        kt = k_ref[rows, :]
        dist = jnp.abs(posc_ref[rows, :] - pos_q)
        bias = [slope * dist for slope in slopes]
        st = []
        for hm in range(n_maps):
            s = _dot_nt(kt, qm_ref[hm * tq:(hm + 1) * tq, :]) - bias[hm // 2]
            if mask is not None:
                s = jnp.where(mask, s, -jnp.inf)
            m, alpha, p = _softmax_step(s, carry[hm][0])
            p_ref[:, hm * tq:(hm + 1) * tq] = p
            st.append((m, alpha))
        vt = vt_ref[j]
        pv = [_dot(vt[h * va:(h + 1) * va], p_ref[:, 2 * h * tq:(2 * h + 2) * tq]) for h in range(DIFF_HEADS)]
        new = []
        for hm in range(n_maps):
            new.append((st[hm][0], st[hm][1] * carry[hm][1] + pv[hm // 2][:, (hm % 2) * tq:(hm % 2 + 1) * tq]))
        return tuple(new)

    one = (jnp.full((1, tq), -jnp.inf, F32), jnp.zeros((va, tq), F32))
    carry = lax.fori_loop(0, i, lambda j, c: tile(j, c, None), (one,) * n_maps)
    carry = tile(i, carry, _causal_t(tq, tq))

    lam = lam_ref[...]
    lam_full = (jnp.exp(jnp.sum(lam[0:1] * lam[1:2], axis=-1, keepdims=True))
                - jnp.exp(jnp.sum(lam[2:3] * lam[3:4], axis=-1, keepdims=True)) + lam_init)
    heads = []
    for h in range(DIFF_HEADS):
        a0, a1 = carry[2 * h][1], carry[2 * h + 1][1]
        o = a0[0:dv] / a0[dv:dv + 1] - lam_full * (a1[0:dv] / a1[dv:dv + 1])
        ms = jnp.mean(o * o, axis=0, keepdims=True)
        heads.append(o * lax.rsqrt(ms + DIFF_SUBLN_EPS) * gsub_ref[...] * (1.0 - lam_init))
    o = jnp.concatenate(heads, axis=0).T
    o_ref[...] = (o * _silu(gate_ref[...].astype(F32))).astype(BF16)


def _diff_flash(d, dvt, gate, posc, posr, lam, gsub_col, *, batch, seq, tq, lam_init):
    nq = seq // tq
    gate_blk = (RW_W + MLA_W) // DIFF_W
    n_maps = 2 * DIFF_HEADS
    return pl.pallas_call(
        functools.partial(_diff_flash_kernel, tq=tq, lam_init=lam_init),
        grid=(batch, nq),
        in_specs=[
            pl.BlockSpec(posc.shape, lambda b, i: (0, 0)),
            pl.BlockSpec(posr.shape, lambda b, i: (0, 0, 0)),
            pl.BlockSpec(lam.shape, lambda b, i: (0, 0)),
            pl.BlockSpec(gsub_col.shape, lambda b, i: (0, 0)),
            pl.BlockSpec((tq, DIFF_W), lambda b, i: (b * nq + i, 0)),
            pl.BlockSpec((seq, DIFF_W), lambda b, i: (b, 1)),
            pl.BlockSpec((nq, DIFF_HEADS * DIFF_VA, tq), lambda b, i: (b, 0, 0)),
            pl.BlockSpec((tq, DIFF_W), lambda b, i: (b * nq + i, gate_blk)),
        ],
        out_specs=pl.BlockSpec((tq, DIFF_W), lambda b, i: (b * nq + i, 0)),
        out_shape=jax.ShapeDtypeStruct((batch * seq, DIFF_W), BF16),
        scratch_shapes=[pltpu.VMEM((n_maps * tq, DIFF_W), BF16), pltpu.VMEM((tq, n_maps * tq), BF16)],
        compiler_params=pltpu.CompilerParams(
            dimension_semantics=("parallel", "parallel"), vmem_limit_bytes=VMEM_LIMIT),
        name="diff_flash",
    )(posc, posr, lam, gsub_col, d, d, dvt, gate)


def _rwkv_kernel(*refs, tb, has_vres, cpi, n_seq, n_shared):
    n_in = 3 if has_vres else 2
    n_io = len(refs) - n_seq * RW_N_SCRATCH
    live = []
    for e in range(n_seq):
        views = [r if n_in <= idx < n_in + n_shared else r.at[e] for idx, r in enumerate(refs[:n_io])]
        views += refs[n_io + e * RW_N_SCRATCH:n_io + (e + 1) * RW_N_SCRATCH]
        live.append(_rwkv_seq(*views, tb=tb, has_vres=has_vres, cpi=cpi, skew=e * RW_SKEW))
    while live:
        live = [g for g in live if next(g, "done") != "done"]


def _rwkv_seq(*refs, tb, has_vres, cpi, skew):
    for _ in range(skew):
        yield
    n_sc = RW_N_SCRATCH
    if has_vres:
        (p_ref, vf_ref, gate_ref, tri_ref, seg_ref, mu_ref, w0_ref, w2_ref, a0_ref, a2_ref,
         v0_ref, v2_ref, kk_ref, ka_ref, rk_ref, lnw_ref, lnb_ref, y_ref) = refs[:-n_sc]
    else:
        (p_ref, gate_ref, tri_ref, seg_ref, mu_ref, w0_ref, w2_ref, a0_ref, a2_ref,
         kk_ref, ka_ref, rk_ref, lnw_ref, lnb_ref, y_ref, vfo_ref) = refs[:-n_sc]
    (state_ref, carry_ref, at_ref, rt_ref, bt_ref, kt_ref, vv_ref, bw_ref, kw_ref, tot_ref, yy_ref,
     lhs_ref, xav_ref, ay_ref, tc_ref, cc_ref, wlt_ref) = refs[-n_sc:]

    @pl.when(pl.program_id(1) == 0)
    def _():
        state_ref[...] = jnp.zeros_like(state_ref)
        carry_ref[...] = jnp.zeros_like(carry_ref)

    seg = seg_ref[...]
    pair_lanes = [slice(pr * LANE, (pr + 1) * LANE) for pr in range(RW_HEADS // 2)]

    def segsum(x):
        xb = x.astype(BF16)
        return jnp.concatenate([_dot(xb[:, ls], seg) for ls in pair_lanes], axis=1)

    p = p_ref[...]
    prev = pltpu.roll(p, 1, 0)
    top = jnp.where(lax.broadcasted_iota(jnp.int32, (SUBLANE, 1), 0) == 0, carry_ref[...], prev[0:SUBLANE])
    prev = jnp.concatenate([top, prev[SUBLANE:]], axis=0)
    carry_ref[...] = p[tb - 1:tb, :]
    yield
    xs = p + (prev - p) * mu_ref[...]
    yield
    r = xs[:, 0:RW_W]
    k = xs[:, RW_W:2 * RW_W]
    v = xs[:, 2 * RW_W:3 * RW_W]
    hwa = xs[:, 3 * RW_W:3 * RW_W + LANE]
    lw = -math.exp(-0.5) * _sigmoid(w0_ref[...] + _dot(jnp.tanh(hwa).astype(BF16), w2_ref[...]))
    yield
    a = _sigmoid(a0_ref[...] + _dot(hwa.astype(BF16), a2_ref[...]))
    yield
    if has_vres:
        hv = xs[:, RW_SHIFT_BASE:RW_SHIFT_BASE + LANE]
        v = v + (vf_ref[...] - v) * _sigmoid(v0_ref[...] + _dot(hv.astype(BF16), v2_ref[...]))
    else:
        vfo_ref[...] = v
    yield
    kk = k * kk_ref[...]
    kk = kk * lax.rsqrt(jnp.maximum(segsum(kk * kk), 1e-24))
    yield
    k = k * (1.0 + (a - 1.0) * ka_ref[...])
    yield

    lw_hi, lw_lo = _split_hi_lo(lw)
    cum = _dot(tri_ref[...], lw_hi) + _dot(tri_ref[...], lw_lo)
    tot = jnp.concatenate([jnp.broadcast_to(cum[e - 1:e], (RW_CHUNK, RW_W))
                           for e in range(RW_CHUNK, tb + 1, RW_CHUNK)], axis=0)
    yield
    w_inv = jnp.exp(-cum)
    w_end = jnp.exp(tot - cum)
    yield
    kka = kk * a
    rt_ref[...] = (r * jnp.exp(cum)).astype(BF16)
    yield
    at_ref[...] = (-kk * jnp.exp(cum - lw)).astype(BF16)
    yield
    bt_ref[...] = (kka * w_inv).astype(BF16)
    kt_ref[...] = (k * w_inv).astype(BF16)
    yield
    vv_ref[...] = v.astype(BF16)
    bw_ref[...] = kka * w_end
    kw_ref[...] = k * w_end
    tot_ref[...] = tot
    yield

    c = RW_CHUNK
    n_pairs = RW_HEADS // 2
    lane_c = lax.broadcasted_iota(jnp.int32, (c, LANE), 1)
    even_c = lane_c < RW_N
    even_2c = lax.broadcasted_iota(jnp.int32, (2 * c, LANE), 1) < RW_N
    ri = lax.broadcasted_iota(jnp.int32, (2 * c, 2 * c), 0)
    ci = lax.broadcasted_iota(jnp.int32, (2 * c, 2 * c), 1)
    cm = jnp.where(ci >= c, ci - c, ci)
    keep = ((ri >= c) & (cm <= ri - c)) | ((ri < c) & (cm < ri))
    same_head = (ri < c) == (ci < c)
    eye2 = (lax.broadcasted_iota(jnp.int32, (c, LANE), 0)
            == jnp.where(lane_c >= c, lane_c - c, lane_c)).astype(F32)

    def bdiag(m):
        zero = jnp.zeros_like(m)
        return jnp.concatenate([jnp.where(even_c, m, zero), jnp.where(even_c, zero, m)], axis=0)

    def adiag(m):
        zero = jnp.zeros_like(m)
        return jnp.concatenate([jnp.where(even_c, zero, m), jnp.where(even_c, m, zero)], axis=0)

    def phase_a(i):
        probs = [(i * cpi + cc, pr) for cc in range(cpi) for pr in range(n_pairs)]
        rows = [pl.ds(pl.multiple_of(ic * c, c), c) for ic, _ in probs]
        rows2 = [pl.ds(pl.multiple_of(ic * 2 * c, 2 * c), 2 * c) for ic, _ in probs]
        ls = [slice(pr * LANE, (pr + 1) * LANE) for _, pr in probs]
        rng = range(len(probs))
        at = [at_ref[rows[j], ls[j]] for j in rng]
        rt = [rt_ref[rows[j], ls[j]] for j in rng]
        bt = [bt_ref[rows[j], ls[j]] for j in rng]
        kt = [kt_ref[rows[j], ls[j]] for j in rng]
        vh = [vv_ref[rows[j], ls[j]] for j in rng]
        xar = [jnp.concatenate([at[j], rt[j]], axis=0) for j in rng]
        zero_b = jnp.zeros((2 * c, LANE), BF16)
        aa_e = [jnp.where(keep, _dot_nt(jnp.where(even_2c, xar[j], zero_b),
                                        jnp.concatenate([bt[j], kt[j]], axis=0)), 0.0) for j in rng]
        yield
        aa_o = [jnp.where(keep, _dot_nt(jnp.where(even_2c, zero_b, xar[j]),
                                        jnp.concatenate([kt[j], bt[j]], axis=0)), 0.0) for j in rng]
        yield
        nn = [jnp.where(even_c, aa_e[j][0:c], aa_o[j][0:c]) for j in rng]
        aak = [jnp.where(even_c, aa_o[j][0:c], aa_e[j][0:c]) for j in rng]
        for j in rng:
            pr = probs[j][1]
            ay_ref[rows[j], 2 * pr * LANE:(2 * pr + 1) * LANE] = jnp.where(
                even_c, aa_e[j][c:2 * c], aa_o[j][c:2 * c]).astype(BF16)
            ay_ref[rows[j], (2 * pr + 1) * LANE:(2 * pr + 2) * LANE] = jnp.where(
                even_c, aa_o[j][c:2 * c], aa_e[j][c:2 * c]).astype(BF16)
        yield
        av = [_dot(aak[j].astype(BF16), adiag(vh[j])) for j in rng]
        yield
        x = [eye2 + nn[j] for j in rng]
        pw = [nn[j].astype(BF16) for j in rng]
        pw = [_dot(pw[j], bdiag(pw[j])).astype(BF16) for j in rng]
        yield
        n_sq = int(math.log2(c)) - 1
        for it in range(n_sq):
            last = it == n_sq - 1
            lhs = [x[j].astype(BF16) if last else jnp.concatenate([x[j].astype(BF16), pw[j]], axis=0) for j in rng]
            out = [_dot(lhs[j], bdiag(pw[j])) for j in rng]
            x = [x[j] + out[j][0:c] for j in rng]
            if not last:
                pw = [out[j][c:2 * c].astype(BF16) for j in rng]
            yield
        fin = [_dot(x[j].astype(BF16), jnp.concatenate([bdiag(at[j]), bdiag(av[j].astype(BF16))], axis=1))
               for j in rng]
        yield
        ahat = [fin[j][:, 0:LANE].astype(BF16) for j in rng]
        xav = [fin[j][:, LANE:2 * LANE] for j in rng]
        zt = [jnp.concatenate([bw_ref[rows[j], ls[j]], kw_ref[rows[j], ls[j]]], axis=0).T.astype(BF16) for j in rng]
        zero_c = jnp.zeros((c, LANE), BF16)
        tcb = [_dot(zt[j], jnp.concatenate([ahat[j], zero_c], axis=0)) for j in rng]
        yield
        ccv = [_dot(zt[j], jnp.concatenate([xav[j].astype(BF16), vh[j]], axis=0)) for j in rng]
        yield
        for j in rng:
            lhs_ref[rows2[j], ls[j]] = jnp.concatenate([ahat[j], rt[j]], axis=0)
            xav_ref[rows[j], ls[j]] = xav[j]
            tc_ref[rows2[j], ls[j]] = jnp.where(same_head, tcb[j], 0.0).astype(BF16)
            cc_ref[rows2[j], ls[j]] = jnp.where(same_head, ccv[j], 0.0)
            tot_row = tot_ref[pl.ds(probs[j][0] * c, 1), ls[j]]
            wlt_ref[rows2[j], ls[j]] = jnp.exp(jnp.broadcast_to(tot_row, (2 * c, LANE)).T)
        yield

    for i in range(tb // (c * cpi)):
        yield from phase_a(i)

    ls = [slice(pr * LANE, (pr + 1) * LANE) for pr in range(n_pairs)]
    rng = range(n_pairs)
    mt = [state_ref[pr] for pr in rng]
    for ic in range(tb // c):
        rows = slice(ic * c, (ic + 1) * c)
        rows2 = slice(ic * 2 * c, (ic + 1) * 2 * c)
        mtb = [mt[pr].astype(BF16) for pr in rng]
        nxt = [_dot(tc_ref[rows2, ls[pr]], mtb[pr]) for pr in rng]
        mt = [mt[pr] * wlt_ref[rows2, ls[pr]] + nxt[pr] + cc_ref[rows2, ls[pr]] for pr in rng]
        out = [_dot(lhs_ref[rows2, ls[pr]], mtb[pr]) for pr in rng]
        ub = [(out[pr][0:c] + xav_ref[rows, ls[pr]]).astype(BF16) for pr in rng]
        vh = [vv_ref[rows, ls[pr]] for pr in rng]
        for pr in rng:
            uv = jnp.concatenate([bdiag(ub[pr]), adiag(vh[pr])], axis=0)
            yy_ref[rows, ls[pr]] = out[pr][c:2 * c] + _dot(ay_ref[rows, 2 * pr * LANE:(2 * pr + 2) * LANE], uv)
        yield
    for pr in rng:
        state_ref[pr] = mt[pr]

    y = yy_ref[...]
    inv_n = 1.0 / RW_N
    dlt = y - segsum(y) * inv_n
    yield
    var = segsum(dlt * dlt) * inv_n
    yield
    y = dlt * lax.rsqrt(var + RW_GN_EPS) * lnw_ref[...] + lnb_ref[...]
    yield
    y = y + segsum(r * k * rk_ref[...]) * v
    y_ref[...] = (y * _silu(gate_ref[...].astype(F32))).astype(BF16)


def _rwkv(p, vfirst, gate, consts, params, *, batch, seq, tb, cpi, n_seq):
    has_vres = vfirst is not None
    cw = p.shape[1]
    per_seq = lambda a: a.reshape(batch // n_seq, n_seq, seq, a.shape[-1])
    blk = lambda w: pl.BlockSpec((None, n_seq, tb, w), lambda b, i: (b, 0, i, 0))
    full = lambda a: pl.BlockSpec(a.shape, lambda b, i: (0,) * a.ndim)
    shared = list(consts) + list(params)
    ins = [per_seq(p)] + ([per_seq(vfirst)] if has_vres else []) + [per_seq(gate)] + shared
    in_specs = [blk(cw)] + ([blk(RW_W)] if has_vres else []) + [blk(RW_W)] + [full(a) for a in shared]
    out_specs = [blk(RW_W)]
    out_shape = [jax.ShapeDtypeStruct((batch // n_seq, n_seq, seq, RW_W), BF16)]
    if not has_vres:
        out_specs.append(blk(RW_W))
        out_shape.append(jax.ShapeDtypeStruct((batch // n_seq, n_seq, seq, RW_W), F32))
    vm = lambda rows, cols, dt: pltpu.VMEM((rows, cols), dt)
    scratch = [pltpu.VMEM((RW_HEADS // 2, LANE, LANE), F32), vm(1, cw, F32),
               vm(tb, RW_W, BF16), vm(tb, RW_W, BF16), vm(tb, RW_W, BF16), vm(tb, RW_W, BF16), vm(tb, RW_W, BF16),
               vm(tb, RW_W, F32), vm(tb, RW_W, F32), vm(tb, RW_W, F32), vm(tb, RW_W, F32),
               vm(2 * tb, RW_W, BF16), vm(tb, RW_W, F32), vm(tb, 2 * RW_W, BF16),
               vm(2 * tb, RW_W, BF16), vm(2 * tb, RW_W, F32), vm(2 * tb, RW_W, F32)]
    assert len(scratch) == RW_N_SCRATCH
    outs = pl.pallas_call(
        functools.partial(_rwkv_kernel, tb=tb, has_vres=has_vres, cpi=cpi, n_seq=n_seq, n_shared=len(shared)),
        grid=(batch // n_seq, seq // tb),
        in_specs=in_specs, out_specs=out_specs, out_shape=out_shape,
        scratch_shapes=scratch * n_seq,
        compiler_params=pltpu.CompilerParams(
            dimension_semantics=("parallel", "arbitrary"), vmem_limit_bytes=VMEM_LIMIT),
        name="rwkv7",
    )(*ins)
    outs = [o.reshape(batch * seq, RW_W) for o in outs]
    return outs if not has_vres else (outs[0], None)


def _outproj_kernel(x_ref, om_ref, od_ref, orw_ref, w_ref, fg_ref, o_ref, *, final):
    acc = (_dot(om_ref[...], w_ref[0:MLA_W, :])
           + _dot(od_ref[...], w_ref[MLA_W:MLA_W + DIFF_W, :])
           + _dot(orw_ref[...], w_ref[MLA_W + DIFF_W:, :]))
    y = x_ref[...] + acc
    if final:
        y = _rms(y, fg_ref[...], NORM_EPS)
    o_ref[...] = y


def _outproj(x2, om, od, orw, w, fg, *, tm, final):
    t = x2.shape[0]
    row = lambda i: (i, 0)
    const = lambda i: (0, 0)
    return pl.pallas_call(
        functools.partial(_outproj_kernel, final=final),
        grid=(t // tm,),
        in_specs=[pl.BlockSpec((tm, D_MODEL), row), pl.BlockSpec((tm, MLA_W), row),
                  pl.BlockSpec((tm, DIFF_W), row), pl.BlockSpec((tm, RW_W), row),
                  pl.BlockSpec(w.shape, const), pl.BlockSpec(fg.shape, const)],
        out_specs=pl.BlockSpec((tm, D_MODEL), row),
        out_shape=jax.ShapeDtypeStruct((t, D_MODEL), F32),
        compiler_params=pltpu.CompilerParams(
            dimension_semantics=("parallel",), vmem_limit_bytes=VMEM_LIMIT),
        name="outproj",
    )(x2, om, od, orw, w, fg)


def _pack_inproj(w_in, w_vres):
    d = w_in.shape[0]
    o = 0
    cq = w_in[:, o:o + MLA_Q_RANK]; o += MLA_Q_RANK
    ckv = w_in[:, o:o + MLA_KV_RANK]; o += MLA_KV_RANK
    kpe = w_in[:, o:o + MLA_ROPE]; o += MLA_ROPE
    dqk = w_in[:, o:o + DIFF_G]; o += DIFF_G
    dv = w_in[:, o:o + DIFF_W]; o += DIFF_W
    gate = w_in[:, o:o + D_MIX]; o += D_MIX
    rw = w_in[:, o:]
    half = MLA_ROPE // 2
    z = lambda n: jnp.zeros((d, n), w_in.dtype)
    kp = jnp.concatenate([z(MLA_NOPE), kpe, z(HP - MLA_QK)], axis=1)
    kps = jnp.concatenate([z(MLA_NOPE), kpe[:, half:], kpe[:, :half], z(HP - MLA_QK)], axis=1)
    g_mla, g_diff, g_rw = gate[:, :MLA_W], gate[:, MLA_W:MLA_W + DIFF_W], gate[:, MLA_W + DIFF_W:]
    cols = [cq, ckv, kp, kps, dqk, g_rw, g_mla, g_diff, rw]
    if w_vres is not None:
        cols += [w_vres, z(LANE - RW_MV_RANK)]
    return jnp.concatenate(cols, axis=1).astype(BF16), _with_ones_rows(dv.T, DIFF_HEADS, 2 * DIFF_DH)


def _pack_mla(w_uq, w_ukv):
    rq, rkv = w_uq.shape[0], w_ukv.shape[0]
    q3 = w_uq.reshape(rq, MLA_HEADS, MLA_QK)
    zq = lambda n: jnp.zeros((rq, MLA_HEADS, n), w_uq.dtype)
    wq = jnp.concatenate([q3, zq(HP - MLA_QK)], axis=2)
    kv3 = w_ukv.reshape(rkv, MLA_HEADS, MLA_NOPE + MLA_V)
    wk = jnp.concatenate([kv3[:, :, :MLA_NOPE], jnp.zeros((rkv, MLA_HEADS, HP - MLA_NOPE), w_ukv.dtype)], axis=2)
    wv = kv3[:, :, MLA_NOPE:]
    flat = lambda a: a.reshape(a.shape[0], -1).astype(BF16)
    return flat(wq), flat(wk), _with_ones_rows(flat(wv).T, MLA_HEADS, MLA_V)


def _with_ones_rows(wt, heads, width):
    w3 = wt.reshape(heads, width, wt.shape[1])
    w3 = jnp.concatenate([w3, jnp.zeros((heads, BF16_ROWS, wt.shape[1]), wt.dtype)], axis=1)
    ones = jnp.concatenate([jnp.zeros((heads, width, 1), F32), jnp.ones((heads, BF16_ROWS, 1), F32)], axis=1)
    return w3.reshape(heads * (width + BF16_ROWS), -1).astype(BF16), ones.reshape(-1, 1)


def _pad_rows(w, top, total):
    return jnp.concatenate([jnp.zeros((top, w.shape[1]), w.dtype), w,
                            jnp.zeros((total - top - w.shape[0], w.shape[1]), w.dtype)], axis=0).astype(BF16)


def _rwkv_consts(tb):
    t = jnp.arange(tb)
    same = (t[:, None] // RW_CHUNK) == (t[None, :] // RW_CHUNK)
    tri = (same & (t[None, :] <= t[:, None])).astype(BF16)
    hl = jnp.arange(LANE) // RW_N
    seg = (hl[:, None] == hl[None, :]).astype(BF16)
    return tri, seg


def kernel(x, positions, pre_g, w_in, w_in_vres, w_out, mla_gq, mla_gkv, mla_wuq, mla_wukv, diff_lam, diff_gsub, rw_mu, rw_mu_vres, rw_w0, rw_w2, rw_a0, rw_a2, rw_v0, rw_v2, rw_kk, rw_ka, rw_rk, rw_lnw, rw_lnb, final_g):
    batch, seq, _ = x.shape
    depth = pre_g.shape[0]
    tm = min(512, seq)
    tq = min(256, seq)
    tq_mla = min(512, seq)
    tb = min(256, seq)
    assert seq % tm == 0 and seq % tq_mla == 0 and tq_mla == 2 * tq and tm % tq == 0
    assert seq % tb == 0 and tb % RW_CHUNK == 0

    x2 = x.reshape(batch * seq, D_MODEL)
    tab = _rope_table(positions)
    pos_f = positions.astype(F32)
    posc = pos_f.reshape(seq, 1)
    posr = pos_f.reshape(seq // tq, 1, tq)
    consts = _rwkv_consts(tb)
    r1 = lambda a: a.reshape(1, -1).astype(F32)

    vfirst = None
    for layer in range(depth):
        vres = layer > 0
        w, (wdvt, dvone) = _pack_inproj(w_in[layer], w_in_vres[layer - 1] if vres else None)
        wq, wk, (wvt, vone) = _pack_mla(mla_wuq[layer], mla_wukv[layer])
        q, k, vt, d, dvt, gate, rw = _inproj(x2, r1(pre_g[layer]), w, tab, r1(mla_gq[layer]), r1(mla_gkv[layer]),
                                             wq, wk, wvt, wdvt, vone, dvone,
                                             seq=seq, tm=tm, tkv=tq)
        o_mla = _mla_flash(q, k, vt, gate, batch=batch, seq=seq, tq=tq_mla, n_h=MLA_HEADS)
        lam_init = 0.8 - 0.6 * math.exp(-0.3 * (layer + 1))
        gsub_col = diff_gsub[layer].reshape(-1, 1).astype(F32)
        o_diff = _diff_flash(d, dvt, gate, posc, posr, diff_lam[layer].astype(F32), gsub_col,
                             batch=batch, seq=seq, tq=tq, lam_init=lam_init)
        mu = rw_mu[layer]
        if vres:
            mu = jnp.concatenate([mu, rw_mu_vres[layer - 1], jnp.zeros((LANE - RW_MV_RANK,), mu.dtype)])
        params = [r1(mu), r1(rw_w0[layer]), _pad_rows(rw_w2[layer], 0, LANE), r1(rw_a0[layer]),
                  _pad_rows(rw_a2[layer], RW_DECAY_RANK, LANE)]
        if vres:
            params += [r1(rw_v0[layer - 1]), _pad_rows(rw_v2[layer - 1], 0, LANE)]
        params += [r1(rw_kk[layer]), r1(rw_ka[layer]), r1(rw_rk[layer]), r1(rw_lnw[layer]), r1(rw_lnb[layer])]
        o_rw, vf = _rwkv(rw, vfirst, gate, consts, params, batch=batch, seq=seq, tb=tb, cpi=4,
                          n_seq=2 if batch % 2 == 0 else 1)
        if not vres:
            vfirst = vf
        x2 = _outproj(x2, o_mla, o_diff, o_rw, w_out[layer].astype(BF16), r1(final_g),
                      tm=tm, final=(layer == depth - 1))
    return x2.reshape(batch, seq, D_MODEL)
```

```python
import functools
import math

import jax
import jax.numpy as jnp
from jax import lax
from jax.experimental import pallas as pl
from jax.experimental.pallas import tpu as pltpu

F32 = jnp.float32
BF16 = jnp.bfloat16

D_MODEL = 1024
D_MIX = 1024
NORM_EPS = 1e-6
MLA_HEADS, MLA_NOPE, MLA_ROPE, MLA_V = 6, 64, 32, 64
MLA_Q_RANK, MLA_KV_RANK = 256, 128
MLA_QK = MLA_NOPE + MLA_ROPE
ROPE_THETA = 10000.0
MLA_W = MLA_HEADS * MLA_V
DIFF_HEADS, DIFF_DH = 4, 32
DIFF_W = DIFF_HEADS * 2 * DIFF_DH
DIFF_SUBLN_EPS = 1e-5
RW_HEADS, RW_N = 6, 64
RW_W = RW_HEADS * RW_N
RW_DECAY_RANK, RW_AAA_RANK, RW_MV_RANK = 64, 64, 32
RW_GN_EPS = 64e-5
RW_SHIFT_BASE = 3 * RW_W + RW_DECAY_RANK + RW_AAA_RANK

LANE = 128
SUBLANE = 8
BF16_ROWS = 16
LOG2E = math.log2(math.e)
HP = LANE
MLA_VA = MLA_V + BF16_ROWS
DIFF_VA = 2 * DIFF_DH + BF16_ROWS
RW_CHUNK = 64
RW_N_SCRATCH = 17
RW_SKEW = 14

MLA_G = MLA_Q_RANK + MLA_KV_RANK + 2 * HP
DIFF_G = 2 * DIFF_W
GATE_G = D_MIX
RW_G0 = RW_SHIFT_BASE
RW_G1 = RW_SHIFT_BASE + LANE
OFF_DIFF = MLA_G
OFF_GATE = OFF_DIFF + DIFF_G
OFF_RW = OFF_GATE + GATE_G

VMEM_LIMIT = 52 * 1024 * 1024

_NT = (((1,), (1,)), ((), ()))
_TN = (((0,), (0,)), ((), ()))


def _dot(a, b):
    return jnp.dot(a, b, preferred_element_type=F32)


def _dot_nt(a, b):
    return lax.dot_general(a, b, _NT, preferred_element_type=F32)


def _split_hi_lo(x):
    hi = x.astype(BF16)
    return hi, (x - hi.astype(F32)).astype(BF16)


def _sigmoid(z):
    return 1.0 / (1.0 + jnp.exp(-z))


def _silu(g):
    return g * _sigmoid(g)


def _rms(x, g, eps):
    return x * lax.rsqrt(jnp.mean(x * x, axis=-1, keepdims=True) + eps) * g


def _rope_table_kernel(pos_ref, tab_ref):
    pos = pos_ref[...]
    lane = lax.broadcasted_iota(jnp.int32, (1, LANE), 1)
    half = MLA_ROPE // 2
    in_rope = (lane >= MLA_NOPE) & (lane < MLA_QK)
    first = lane < MLA_NOPE + half
    idx = jnp.where(first, lane - MLA_NOPE, lane - MLA_NOPE - half).astype(F32)
    inv = jnp.exp(idx * (-math.log(ROPE_THETA) / half))
    ang = pos * inv
    cos = jnp.where(in_rope, jnp.cos(ang), jnp.where(lane < MLA_NOPE, 1.0, 0.0))
    sin = jnp.sin(ang)
    sin = jnp.where(in_rope, jnp.where(first, -sin, sin), 0.0)
    scale = MLA_QK ** -0.5 * LOG2E
    tab_ref[:, 0 * LANE:1 * LANE] = cos * scale
    tab_ref[:, 1 * LANE:2 * LANE] = jnp.where(first, sin, 0.0) * scale
    tab_ref[:, 2 * LANE:3 * LANE] = jnp.where(first, 0.0, sin) * scale
    tab_ref[:, 3 * LANE:4 * LANE] = cos
    tab_ref[:, 4 * LANE:5 * LANE] = sin


def _rope_table(positions):
    s = positions.shape[0]
    return pl.pallas_call(
        _rope_table_kernel,
        out_shape=jax.ShapeDtypeStruct((s, 5 * LANE), F32),
        name="rope_table",
    )(positions.astype(F32).reshape(s, 1))


def _inproj_kernel(x_ref, g_ref, w_ref, tab_ref, gq_ref, gkv_ref, wq_ref, wk_ref, wvt_ref, wdvt_ref,
                   vone_ref, dvone_ref,
                   q_ref, k_ref, vt_ref, d_ref, dvt_ref, gate_ref, rw_ref, *, tkv):
    x = x_ref[...]
    h = _rms(x, g_ref[...], NORM_EPS).astype(BF16)

    mla = _dot(h, w_ref[:, 0:MLA_G])
    cqn = _rms(mla[:, 0:MLA_Q_RANK], gq_ref[...], NORM_EPS).astype(BF16)
    ckvn = _rms(mla[:, MLA_Q_RANK:MLA_Q_RANK + MLA_KV_RANK], gkv_ref[...], NORM_EPS).astype(BF16)
    o_kpe = MLA_Q_RANK + MLA_KV_RANK
    tab = tab_ref[...]
    cos_q, sin_q1, sin_q2 = tab[:, 0:LANE], tab[:, LANE:2 * LANE], tab[:, 2 * LANE:3 * LANE]
    cos_k, sin_k = tab[:, 3 * LANE:4 * LANE], tab[:, 4 * LANE:5 * LANE]
    kpe = mla[:, o_kpe:o_kpe + HP] * cos_k + mla[:, o_kpe + HP:o_kpe + 2 * HP] * sin_k
    qa = _dot(cqn, wq_ref[...])
    kn = _dot(ckvn, wk_ref[...])
    half = MLA_ROPE // 2
    for hd in range(MLA_HEADS):
        sl = slice(hd * HP, (hd + 1) * HP)
        qh = qa[:, sl]
        q_ref[:, sl] = (qh * cos_q + pltpu.roll(qh, HP - half, 1) * sin_q1
                        + pltpu.roll(qh, half, 1) * sin_q2).astype(BF16)
        k_ref[:, sl] = (kn[:, sl] + kpe).astype(BF16)
    vt = (_dot_nt(wvt_ref[...], ckvn) + vone_ref[...]).astype(BF16)
    dvt = (_dot_nt(wdvt_ref[...], h) + dvone_ref[...]).astype(BF16)
    for t in range(vt_ref.shape[0]):
        vt_ref[t] = vt[:, t * tkv:(t + 1) * tkv]
        dvt_ref[t] = dvt[:, t * tkv:(t + 1) * tkv]

    dd = _dot(h, w_ref[:, OFF_DIFF:OFF_GATE])
    d_ref[:, 0:DIFF_W] = (dd[:, 0:DIFF_W] * (DIFF_DH ** -0.5 * LOG2E)).astype(BF16)
    d_ref[:, DIFF_W:DIFF_G] = dd[:, DIFF_W:DIFF_G].astype(BF16)
    gate_ref[...] = _dot(h, w_ref[:, OFF_GATE:OFF_RW]).astype(BF16)
    rw_ref[...] = _dot(h, w_ref[:, OFF_RW:])


def _inproj(x2, g, w, tab, gq, gkv, wq, wk, wvt, wdvt, vone, dvone, *, seq, tm, tkv):
    t = x2.shape[0]
    ctot = w.shape[1]
    rw_g = ctot - OFF_RW
    n_seq_tiles = seq // tm
    kv_per_tile = tm // tkv
    row = lambda i: (i, 0)
    slab = lambda i: (i, 0, 0)
    const = lambda i: (0, 0)
    full = lambda a: pl.BlockSpec(a.shape, const)
    return pl.pallas_call(
        functools.partial(_inproj_kernel, tkv=tkv),
        grid=(t // tm,),
        in_specs=[
            pl.BlockSpec((tm, D_MODEL), row),
            full(g), full(w),
            pl.BlockSpec((tm, tab.shape[1]), lambda i: (i % n_seq_tiles, 0)),
            full(gq), full(gkv), full(wq), full(wk), full(wvt), full(wdvt),
            full(vone), full(dvone),
        ],
        out_specs=[
            pl.BlockSpec((tm, MLA_HEADS * HP), row),
            pl.BlockSpec((tm, MLA_HEADS * HP), row),
            pl.BlockSpec((kv_per_tile, wvt.shape[0], tkv), slab),
            pl.BlockSpec((tm, DIFF_G), row),
            pl.BlockSpec((kv_per_tile, wdvt.shape[0], tkv), slab),
            pl.BlockSpec((tm, GATE_G), row),
            pl.BlockSpec((tm, rw_g), row),
        ],
        out_shape=[
            jax.ShapeDtypeStruct((t, MLA_HEADS * HP), BF16),
            jax.ShapeDtypeStruct((t, MLA_HEADS * HP), BF16),
            jax.ShapeDtypeStruct((t // tkv, wvt.shape[0], tkv), BF16),
            jax.ShapeDtypeStruct((t, DIFF_G), BF16),
            jax.ShapeDtypeStruct((t // tkv, wdvt.shape[0], tkv), BF16),
            jax.ShapeDtypeStruct((t, GATE_G), BF16),
            jax.ShapeDtypeStruct((t, rw_g), F32),
        ],
        compiler_params=pltpu.CompilerParams(
            dimension_semantics=("parallel",), vmem_limit_bytes=VMEM_LIMIT),
        name="inproj",
    )(x2, g, w, tab, gq, gkv, wq, wk, wvt, wdvt, vone, dvone)


def _softmax_step(s, m):
    m_new = jnp.maximum(m, jnp.max(s, axis=0, keepdims=True))
    return m_new, jnp.exp2(m - m_new), jnp.exp2(s - m_new).astype(BF16)


def _causal_t(n_keys, n_queries):
    return (lax.broadcasted_iota(jnp.int32, (n_keys, n_queries), 0)
            <= lax.broadcasted_iota(jnp.int32, (n_keys, n_queries), 1))


def _mla_flash_kernel(q_ref, k_ref, vt_ref, gate_ref, o_ref, *, tq, tk, n_h):
    i = pl.program_id(2)
    half = tq // 2
    pw = MLA_VA
    hs = [slice(hh * HP, (hh + 1) * HP) for hh in range(n_h)]
    ps = [slice(hh * pw, (hh + 1) * pw) for hh in range(n_h)]

    def tile(j, carry, qs, mask, n_slab):
        rows = pl.ds(pl.multiple_of(j * (n_slab * tk), n_slab * tk), n_slab * tk)
        s = [_dot_nt(k_ref[rows, hs[hh]], qs[hh]) for hh in range(n_h)]
        if mask is not None:
            s = [jnp.where(mask, x, -jnp.inf) for x in s]
        st = [_softmax_step(s[hh], carry[hh][0]) for hh in range(n_h)]
        vt = [vt_ref[j * n_slab + t] for t in range(n_slab)]
        pv = [sum(_dot(vt[t][ps[hh]], st[hh][2][t * tk:(t + 1) * tk]) for t in range(n_slab))
              for hh in range(n_h)]
        return tuple((st[hh][0], st[hh][1] * carry[hh][1] + pv[hh]) for hh in range(n_h))

    qs = [q_ref[:, hs[hh]] for hh in range(n_h)]
    one = (jnp.full((1, tq), -jnp.inf, F32), jnp.zeros((pw, tq), F32))
    carry = lax.fori_loop(0, i, lambda j, c: tile(j, c, qs, None, tq // tk), (one,) * n_h)
    carry = tile(2 * i, carry, qs, _causal_t(tk, tq), 1)
    late = tile(2 * i + 1, tuple(tuple(a[:, half:] for a in c) for c in carry),
                [q[half:] for q in qs], _causal_t(tk, half), 1)
    accs = [jnp.concatenate([carry[hh][1][:, :half], late[hh][1]], axis=1) for hh in range(n_h)]
    o = jnp.concatenate([a[0:MLA_V] / a[MLA_V:MLA_V + 1] for a in accs], axis=0).T
    o_ref[...] = (o * _silu(gate_ref[...].astype(F32))).astype(BF16)


def _mla_flash(q, k, vt, gate, *, batch, seq, tq, n_h):
    nq = seq // tq
    tk = tq // 2
    nkv = seq // tk
    ow = n_h * MLA_V
    vw = n_h * MLA_VA
    gate_blk0 = RW_W // ow
    return pl.pallas_call(
        functools.partial(_mla_flash_kernel, tq=tq, tk=tk, n_h=n_h),
        grid=(batch, MLA_HEADS // n_h, nq),
        in_specs=[
            pl.BlockSpec((tq, n_h * HP), lambda b, g, i: (b * nq + i, g)),
            pl.BlockSpec((seq, n_h * HP), lambda b, g, i: (b, g)),
            pl.BlockSpec((nkv, vw, tk), lambda b, g, i: (b, g, 0)),
            pl.BlockSpec((tq, ow), lambda b, g, i: (b * nq + i, gate_blk0 + g)),
        ],
        out_specs=pl.BlockSpec((tq, ow), lambda b, g, i: (b * nq + i, g)),
        out_shape=jax.ShapeDtypeStruct((batch * seq, MLA_W), BF16),
        compiler_params=pltpu.CompilerParams(
            dimension_semantics=("parallel", "parallel", "parallel"), vmem_limit_bytes=VMEM_LIMIT),
        name="mla_flash",
    )(q, k, vt, gate)


def _diff_flash_kernel(posc_ref, posr_ref, lam_ref, gsub_ref, q_ref, k_ref, vt_ref, gate_ref, o_ref, qm_ref, p_ref,
                       *, tq, tk, lam_init):
    i = pl.program_id(1)
    n_maps = 2 * DIFF_HEADS
    half = tq // 2
    q = q_ref[...]
    lane_q = lax.broadcasted_iota(jnp.int32, (tq, DIFF_W), 1)
    for hm in range(n_maps):
        own = (lane_q >= hm * DIFF_DH) & (lane_q < (hm + 1) * DIFF_DH)
        qm_ref[hm * tq:(hm + 1) * tq, :] = jnp.where(own, q, jnp.zeros_like(q))
    pos_q = posr_ref[i]
    slopes = [LOG2E * 2.0 ** (-8.0 * (h + 1) / DIFF_HEADS) for h in range(DIFF_HEADS)]
    dv = 2 * DIFF_DH
    va = DIFF_VA

    def tile(j, carry, q0, nq, mask, n_slab):
        nk = n_slab * tk
        rows = pl.ds(pl.multiple_of(j * nk, nk), nk)
        kt = k_ref[rows, :]
        dist = jnp.abs(posc_ref[rows, :] - pos_q[:, q0:q0 + nq])
        bias = [slope * dist for slope in slopes]
        st = []
        for hm in range(n_maps):
            cols = slice(hm * tq + q0, hm * tq + q0 + nq)
            s = _dot_nt(kt, qm_ref[cols, :]) - bias[hm // 2]
            if mask is not None:
                s = jnp.where(mask, s, -jnp.inf)
            m, alpha, p = _softmax_step(s, carry[hm][0])
            p_ref[0:nk, cols] = p
            st.append((m, alpha))
        vts = [vt_ref[j * n_slab + t] for t in range(n_slab)]
        new = []
        for hm in range(n_maps):
            cols = slice(hm * tq + q0, hm * tq + q0 + nq)
            vr = slice((hm // 2) * va, (hm // 2 + 1) * va)
            pv = sum(_dot(vts[t][vr], p_ref[t * tk:(t + 1) * tk, cols]) for t in range(n_slab))
            new.append((st[hm][0], st[hm][1] * carry[hm][1] + pv))
        return tuple(new)

    one = (jnp.full((1, tq), -jnp.inf, F32), jnp.zeros((va, tq), F32))
    carry = lax.fori_loop(0, i, lambda j, c: tile(j, c, 0, tq, None, tq // tk), (one,) * n_maps)
    carry = tile(2 * i, carry, 0, tq, _causal_t(tk, tq), 1)
    late = tile(2 * i + 1, tuple(tuple(a[:, half:] for a in c) for c in carry),
                half, half, _causal_t(tk, half), 1)
    accs = [jnp.concatenate([carry[hm][1][:, :half], late[hm][1]], axis=1) for hm in range(n_maps)]

    lam = lam_ref[...]
    lam_full = (jnp.exp(jnp.sum(lam[0:1] * lam[1:2], axis=-1, keepdims=True))
                - jnp.exp(jnp.sum(lam[2:3] * lam[3:4], axis=-1, keepdims=True)) + lam_init)
    heads = []
    for h in range(DIFF_HEADS):
        a0, a1 = accs[2 * h], accs[2 * h + 1]
        o = a0[0:dv] / a0[dv:dv + 1] - lam_full * (a1[0:dv] / a1[dv:dv + 1])
        ms = jnp.mean(o * o, axis=0, keepdims=True)
        heads.append(o * lax.rsqrt(ms + DIFF_SUBLN_EPS) * gsub_ref[...] * (1.0 - lam_init))
    o = jnp.concatenate(heads, axis=0).T
    o_ref[...] = (o * _silu(gate_ref[...].astype(F32))).astype(BF16)


def _diff_flash(d, dvt, gate, posc, posr, lam, gsub_col, *, batch, seq, tq, lam_init):
    nq = seq // tq
    gate_blk = (RW_W + MLA_W) // DIFF_W
    n_maps = 2 * DIFF_HEADS
    tk = tq // 2
    nkv = seq // tk
    return pl.pallas_call(
        functools.partial(_diff_flash_kernel, tq=tq, tk=tk, lam_init=lam_init),
        grid=(batch, nq),
        in_specs=[
            pl.BlockSpec(posc.shape, lambda b, i: (0, 0)),
            pl.BlockSpec(posr.shape, lambda b, i: (0, 0, 0)),
            pl.BlockSpec(lam.shape, lambda b, i: (0, 0)),
            pl.BlockSpec(gsub_col.shape, lambda b, i: (0, 0)),
            pl.BlockSpec((tq, DIFF_W), lambda b, i: (b * nq + i, 0)),
            pl.BlockSpec((seq, DIFF_W), lambda b, i: (b, 1)),
            pl.BlockSpec((nkv, DIFF_HEADS * DIFF_VA, tk), lambda b, i: (b, 0, 0)),
            pl.BlockSpec((tq, DIFF_W), lambda b, i: (b * nq + i, gate_blk)),
        ],
        out_specs=pl.BlockSpec((tq, DIFF_W), lambda b, i: (b * nq + i, 0)),
        out_shape=jax.ShapeDtypeStruct((batch * seq, DIFF_W), BF16),
        scratch_shapes=[pltpu.VMEM((n_maps * tq, DIFF_W), BF16), pltpu.VMEM((tq, n_maps * tq), BF16)],
        compiler_params=pltpu.CompilerParams(
            dimension_semantics=("parallel", "parallel"), vmem_limit_bytes=VMEM_LIMIT),
        name="diff_flash",
    )(posc, posr, lam, gsub_col, d, d, dvt, gate)


def _rwkv_kernel(*refs, tb, has_vres, cpi, n_seq, n_shared):
    n_in = 3 if has_vres else 2
    n_io = len(refs) - n_seq * RW_N_SCRATCH
    live = []
    for e in range(n_seq):
        views = [r if n_in <= idx < n_in + n_shared else r.at[e] for idx, r in enumerate(refs[:n_io])]
        views += refs[n_io + e * RW_N_SCRATCH:n_io + (e + 1) * RW_N_SCRATCH]
        live.append(_rwkv_seq(*views, tb=tb, has_vres=has_vres, cpi=cpi, skew=e * RW_SKEW))
    while live:
        live = [g for g in live if next(g, "done") != "done"]


def _rwkv_seq(*refs, tb, has_vres, cpi, skew):
    for _ in range(skew):
        yield
    n_sc = RW_N_SCRATCH
    if has_vres:
        (p_ref, vf_ref, gate_ref, tri_ref, seg_ref, mu_ref, w0_ref, w2_ref, a0_ref, a2_ref,
         v0_ref, v2_ref, kk_ref, ka_ref, rk_ref, lnw_ref, lnb_ref, y_ref) = refs[:-n_sc]
    else:
        (p_ref, gate_ref, tri_ref, seg_ref, mu_ref, w0_ref, w2_ref, a0_ref, a2_ref,
         kk_ref, ka_ref, rk_ref, lnw_ref, lnb_ref, y_ref, vfo_ref) = refs[:-n_sc]
    (state_ref, carry_ref, at_ref, rt_ref, bt_ref, kt_ref, vv_ref, bw_ref, kw_ref, tot_ref, yy_ref,
     lhs_ref, xav_ref, ay_ref, tc_ref, cc_ref, wlt_ref) = refs[-n_sc:]

    @pl.when(pl.program_id(1) == 0)
    def _():
        state_ref[...] = jnp.zeros_like(state_ref)
        carry_ref[...] = jnp.zeros_like(carry_ref)

    seg = seg_ref[...]
    pair_lanes = [slice(pr * LANE, (pr + 1) * LANE) for pr in range(RW_HEADS // 2)]

    def segsum(x):
        xb = x.astype(BF16)
        return jnp.concatenate([_dot(xb[:, ls], seg) for ls in pair_lanes], axis=1)

    p = p_ref[...]
    prev = pltpu.roll(p, 1, 0)
    top = jnp.where(lax.broadcasted_iota(jnp.int32, (SUBLANE, 1), 0) == 0, carry_ref[...], prev[0:SUBLANE])
    prev = jnp.concatenate([top, prev[SUBLANE:]], axis=0)
    carry_ref[...] = p[tb - 1:tb, :]
    yield
    xs = p + (prev - p) * mu_ref[...]
    yield
    r = xs[:, 0:RW_W]
    k = xs[:, RW_W:2 * RW_W]
    v = xs[:, 2 * RW_W:3 * RW_W]
    hwa = xs[:, 3 * RW_W:3 * RW_W + LANE]
    lw = -math.exp(-0.5) * _sigmoid(w0_ref[...] + _dot(jnp.tanh(hwa).astype(BF16), w2_ref[...]))
    yield
    a = _sigmoid(a0_ref[...] + _dot(hwa.astype(BF16), a2_ref[...]))
    yield
    if has_vres:
        hv = xs[:, RW_SHIFT_BASE:RW_SHIFT_BASE + LANE]
        v = v + (vf_ref[...] - v) * _sigmoid(v0_ref[...] + _dot(hv.astype(BF16), v2_ref[...]))
    else:
        vfo_ref[...] = v
    yield
    kk = k * kk_ref[...]
    kk = kk * lax.rsqrt(jnp.maximum(segsum(kk * kk), 1e-24))
    yield
    k = k * (1.0 + (a - 1.0) * ka_ref[...])
    yield

    lw_hi, lw_lo = _split_hi_lo(lw)
    cum = _dot(tri_ref[...], lw_hi) + _dot(tri_ref[...], lw_lo)
    tot = jnp.concatenate([jnp.broadcast_to(cum[e - 1:e], (RW_CHUNK, RW_W))
                           for e in range(RW_CHUNK, tb + 1, RW_CHUNK)], axis=0)
    yield
    w_inv = jnp.exp(-cum)
    w_end = jnp.exp(tot - cum)
    yield
    kka = kk * a
    rt_ref[...] = (r * jnp.exp(cum)).astype(BF16)
    yield
    at_ref[...] = (-kk * jnp.exp(cum - lw)).astype(BF16)
    yield
    bt_ref[...] = (kka * w_inv).astype(BF16)
    kt_ref[...] = (k * w_inv).astype(BF16)
    yield
    vv_ref[...] = v.astype(BF16)
    bw_ref[...] = kka * w_end
    kw_ref[...] = k * w_end
    tot_ref[...] = tot
    yield

    c = RW_CHUNK
    n_pairs = RW_HEADS // 2
    lane_c = lax.broadcasted_iota(jnp.int32, (c, LANE), 1)
    even_c = lane_c < RW_N
    even_2c = lax.broadcasted_iota(jnp.int32, (2 * c, LANE), 1) < RW_N
    ri = lax.broadcasted_iota(jnp.int32, (2 * c, 2 * c), 0)
    ci = lax.broadcasted_iota(jnp.int32, (2 * c, 2 * c), 1)
    cm = jnp.where(ci >= c, ci - c, ci)
    keep = ((ri >= c) & (cm <= ri - c)) | ((ri < c) & (cm < ri))
    same_head = (ri < c) == (ci < c)
    eye2 = (lax.broadcasted_iota(jnp.int32, (c, LANE), 0)
            == jnp.where(lane_c >= c, lane_c - c, lane_c)).astype(F32)

    def bdiag(m):
        zero = jnp.zeros_like(m)
        return jnp.concatenate([jnp.where(even_c, m, zero), jnp.where(even_c, zero, m)], axis=0)

    def adiag(m):
        zero = jnp.zeros_like(m)
        return jnp.concatenate([jnp.where(even_c, zero, m), jnp.where(even_c, m, zero)], axis=0)

    def phase_a(i):
        probs = [(i * cpi + cc, pr) for cc in range(cpi) for pr in range(n_pairs)]
        rows = [pl.ds(pl.multiple_of(ic * c, c), c) for ic, _ in probs]
        rows2 = [pl.ds(pl.multiple_of(ic * 2 * c, 2 * c), 2 * c) for ic, _ in probs]
        ls = [slice(pr * LANE, (pr + 1) * LANE) for _, pr in probs]
        rng = range(len(probs))
        at = [at_ref[rows[j], ls[j]] for j in rng]
        rt = [rt_ref[rows[j], ls[j]] for j in rng]
        bt = [bt_ref[rows[j], ls[j]] for j in rng]
        kt = [kt_ref[rows[j], ls[j]] for j in rng]
        vh = [vv_ref[rows[j], ls[j]] for j in rng]
        xar = [jnp.concatenate([at[j], rt[j]], axis=0) for j in rng]
        zero_b = jnp.zeros((2 * c, LANE), BF16)
        aa_e = [jnp.where(keep, _dot_nt(jnp.where(even_2c, xar[j], zero_b),
                                        jnp.concatenate([bt[j], kt[j]], axis=0)), 0.0) for j in rng]
        yield
        aa_o = [jnp.where(keep, _dot_nt(jnp.where(even_2c, zero_b, xar[j]),
                                        jnp.concatenate([kt[j], bt[j]], axis=0)), 0.0) for j in rng]
        yield
        nn = [jnp.where(even_c, aa_e[j][0:c], aa_o[j][0:c]) for j in rng]
        aak = [jnp.where(even_c, aa_o[j][0:c], aa_e[j][0:c]) for j in rng]
        for j in rng:
            pr = probs[j][1]
            ay_ref[rows[j], 2 * pr * LANE:(2 * pr + 1) * LANE] = jnp.where(
                even_c, aa_e[j][c:2 * c], aa_o[j][c:2 * c]).astype(BF16)
            ay_ref[rows[j], (2 * pr + 1) * LANE:(2 * pr + 2) * LANE] = jnp.where(
                even_c, aa_o[j][c:2 * c], aa_e[j][c:2 * c]).astype(BF16)
        yield
        av = [_dot(aak[j].astype(BF16), adiag(vh[j])) for j in rng]
        yield
        x = [eye2 + nn[j] for j in rng]
        pw = [nn[j].astype(BF16) for j in rng]
        pw = [_dot(pw[j], bdiag(pw[j])).astype(BF16) for j in rng]
        yield
        n_sq = int(math.log2(c)) - 1
        for it in range(n_sq):
            last = it == n_sq - 1
            lhs = [x[j].astype(BF16) if last else jnp.concatenate([x[j].astype(BF16), pw[j]], axis=0) for j in rng]
            out = [_dot(lhs[j], bdiag(pw[j])) for j in rng]
            x = [x[j] + out[j][0:c] for j in rng]
            if not last:
                pw = [out[j][c:2 * c].astype(BF16) for j in rng]
            yield
        fin = [_dot(x[j].astype(BF16), jnp.concatenate([bdiag(at[j]), bdiag(av[j].astype(BF16))], axis=1))
               for j in rng]
        yield
        ahat = [fin[j][:, 0:LANE].astype(BF16) for j in rng]
        xav = [fin[j][:, LANE:2 * LANE] for j in rng]
        zt = [jnp.concatenate([bw_ref[rows[j], ls[j]], kw_ref[rows[j], ls[j]]], axis=0).T.astype(BF16) for j in rng]
        zero_c = jnp.zeros((c, LANE), BF16)
        tcb = [_dot(zt[j], jnp.concatenate([ahat[j], zero_c], axis=0)) for j in rng]
        yield
        ccv = [_dot(zt[j], jnp.concatenate([xav[j].astype(BF16), vh[j]], axis=0)) for j in rng]
        yield
        for j in rng:
            lhs_ref[rows2[j], ls[j]] = jnp.concatenate([ahat[j], rt[j]], axis=0)
            xav_ref[rows[j], ls[j]] = xav[j]
            tc_ref[rows2[j], ls[j]] = jnp.where(same_head, tcb[j], 0.0).astype(BF16)
            cc_ref[rows2[j], ls[j]] = jnp.where(same_head, ccv[j], 0.0)
            tot_row = tot_ref[pl.ds(probs[j][0] * c, 1), ls[j]]
            wlt_ref[rows2[j], ls[j]] = jnp.exp(jnp.broadcast_to(tot_row, (2 * c, LANE)).T)
        yield

    for i in range(tb // (c * cpi)):
        yield from phase_a(i)

    ls = [slice(pr * LANE, (pr + 1) * LANE) for pr in range(n_pairs)]
    rng = range(n_pairs)
    mt = [state_ref[pr] for pr in rng]
    for ic in range(tb // c):
        rows = slice(ic * c, (ic + 1) * c)
        rows2 = slice(ic * 2 * c, (ic + 1) * 2 * c)
        mtb = [mt[pr].astype(BF16) for pr in rng]
        nxt = [_dot(tc_ref[rows2, ls[pr]], mtb[pr]) for pr in rng]
        mt = [mt[pr] * wlt_ref[rows2, ls[pr]] + nxt[pr] + cc_ref[rows2, ls[pr]] for pr in rng]
        out = [_dot(lhs_ref[rows2, ls[pr]], mtb[pr]) for pr in rng]
        ub = [(out[pr][0:c] + xav_ref[rows, ls[pr]]).astype(BF16) for pr in rng]
        vh = [vv_ref[rows, ls[pr]] for pr in rng]
        for pr in rng:
            uv = jnp.concatenate([bdiag(ub[pr]), adiag(vh[pr])], axis=0)
            yy_ref[rows, ls[pr]] = out[pr][c:2 * c] + _dot(ay_ref[rows, 2 * pr * LANE:(2 * pr + 2) * LANE], uv)
        yield
    for pr in rng:
        state_ref[pr] = mt[pr]

    y = yy_ref[...]
    inv_n = 1.0 / RW_N
    dlt = y - segsum(y) * inv_n
    yield
    var = segsum(dlt * dlt) * inv_n
    yield
    y = dlt * lax.rsqrt(var + RW_GN_EPS) * lnw_ref[...] + lnb_ref[...]
    yield
    y = y + segsum(r * k * rk_ref[...]) * v
    y_ref[...] = (y * _silu(gate_ref[...].astype(F32))).astype(BF16)


def _rwkv(p, vfirst, gate, consts, params, *, batch, seq, tb, cpi, n_seq):
    has_vres = vfirst is not None
    cw = p.shape[1]
    per_seq = lambda a: a.reshape(batch // n_seq, n_seq, seq, a.shape[-1])
    blk = lambda w: pl.BlockSpec((None, n_seq, tb, w), lambda b, i: (b, 0, i, 0))
    full = lambda a: pl.BlockSpec(a.shape, lambda b, i: (0,) * a.ndim)
    shared = list(consts) + list(params)
    ins = [per_seq(p)] + ([per_seq(vfirst)] if has_vres else []) + [per_seq(gate)] + shared
    in_specs = [blk(cw)] + ([blk(RW_W)] if has_vres else []) + [blk(RW_W)] + [full(a) for a in shared]
    out_specs = [blk(RW_W)]
    out_shape = [jax.ShapeDtypeStruct((batch // n_seq, n_seq, seq, RW_W), BF16)]
    if not has_vres:
        out_specs.append(blk(RW_W))
        out_shape.append(jax.ShapeDtypeStruct((batch // n_seq, n_seq, seq, RW_W), F32))
    vm = lambda rows, cols, dt: pltpu.VMEM((rows, cols), dt)
    scratch = [pltpu.VMEM((RW_HEADS // 2, LANE, LANE), F32), vm(1, cw, F32),
               vm(tb, RW_W, BF16), vm(tb, RW_W, BF16), vm(tb, RW_W, BF16), vm(tb, RW_W, BF16), vm(tb, RW_W, BF16),
               vm(tb, RW_W, F32), vm(tb, RW_W, F32), vm(tb, RW_W, F32), vm(tb, RW_W, F32),
               vm(2 * tb, RW_W, BF16), vm(tb, RW_W, F32), vm(tb, 2 * RW_W, BF16),
               vm(2 * tb, RW_W, BF16), vm(2 * tb, RW_W, F32), vm(2 * tb, RW_W, F32)]
    assert len(scratch) == RW_N_SCRATCH
    outs = pl.pallas_call(
        functools.partial(_rwkv_kernel, tb=tb, has_vres=has_vres, cpi=cpi, n_seq=n_seq, n_shared=len(shared)),
        grid=(batch // n_seq, seq // tb),
        in_specs=in_specs, out_specs=out_specs, out_shape=out_shape,
        scratch_shapes=scratch * n_seq,
        compiler_params=pltpu.CompilerParams(
            dimension_semantics=("parallel", "arbitrary"), vmem_limit_bytes=VMEM_LIMIT),
        name="rwkv7",
    )(*ins)
    outs = [o.reshape(batch * seq, RW_W) for o in outs]
    return outs if not has_vres else (outs[0], None)


def _outproj_kernel(x_ref, om_ref, od_ref, orw_ref, w_ref, fg_ref, o_ref, *, final):
    mixed = jnp.concatenate([om_ref[...], od_ref[...], orw_ref[...]], axis=1)
    y = x_ref[...] + _dot(mixed, w_ref[...])
    if final:
        y = _rms(y, fg_ref[...], NORM_EPS)
    o_ref[...] = y


def _outproj(x2, om, od, orw, w, fg, *, tm, final):
    t = x2.shape[0]
    row = lambda i: (i, 0)
    const = lambda i: (0, 0)
    return pl.pallas_call(
        functools.partial(_outproj_kernel, final=final),
        grid=(t // tm,),
        in_specs=[pl.BlockSpec((tm, D_MODEL), row), pl.BlockSpec((tm, MLA_W), row),
                  pl.BlockSpec((tm, DIFF_W), row), pl.BlockSpec((tm, RW_W), row),
                  pl.BlockSpec(w.shape, const), pl.BlockSpec(fg.shape, const)],
        out_specs=pl.BlockSpec((tm, D_MODEL), row),
        out_shape=jax.ShapeDtypeStruct((t, D_MODEL), F32),
        compiler_params=pltpu.CompilerParams(
            dimension_semantics=("parallel",), vmem_limit_bytes=VMEM_LIMIT),
        name="outproj",
    )(x2, om, od, orw, w, fg)


def _pack_inproj(w_in, w_vres):
    d = w_in.shape[0]
    o = 0
    cq = w_in[:, o:o + MLA_Q_RANK]; o += MLA_Q_RANK
    ckv = w_in[:, o:o + MLA_KV_RANK]; o += MLA_KV_RANK
    kpe = w_in[:, o:o + MLA_ROPE]; o += MLA_ROPE
    dqk = w_in[:, o:o + DIFF_G]; o += DIFF_G
    dv = w_in[:, o:o + DIFF_W]; o += DIFF_W
    gate = w_in[:, o:o + D_MIX]; o += D_MIX
    rw = w_in[:, o:]
    half = MLA_ROPE // 2
    z = lambda n: jnp.zeros((d, n), w_in.dtype)
    kp = jnp.concatenate([z(MLA_NOPE), kpe, z(HP - MLA_QK)], axis=1)
    kps = jnp.concatenate([z(MLA_NOPE), kpe[:, half:], kpe[:, :half], z(HP - MLA_QK)], axis=1)
    g_mla, g_diff, g_rw = gate[:, :MLA_W], gate[:, MLA_W:MLA_W + DIFF_W], gate[:, MLA_W + DIFF_W:]
    cols = [cq, ckv, kp, kps, dqk, g_rw, g_mla, g_diff, rw]
    if w_vres is not None:
        cols += [w_vres, z(LANE - RW_MV_RANK)]
    return jnp.concatenate(cols, axis=1).astype(BF16), _with_ones_rows(dv.T, DIFF_HEADS, 2 * DIFF_DH)


def _pack_mla(w_uq, w_ukv):
    rq, rkv = w_uq.shape[0], w_ukv.shape[0]
    q3 = w_uq.reshape(rq, MLA_HEADS, MLA_QK)
    zq = lambda n: jnp.zeros((rq, MLA_HEADS, n), w_uq.dtype)
    wq = jnp.concatenate([q3, zq(HP - MLA_QK)], axis=2)
    kv3 = w_ukv.reshape(rkv, MLA_HEADS, MLA_NOPE + MLA_V)
    wk = jnp.concatenate([kv3[:, :, :MLA_NOPE], jnp.zeros((rkv, MLA_HEADS, HP - MLA_NOPE), w_ukv.dtype)], axis=2)
    wv = kv3[:, :, MLA_NOPE:]
    flat = lambda a: a.reshape(a.shape[0], -1).astype(BF16)
    return flat(wq), flat(wk), _with_ones_rows(flat(wv).T, MLA_HEADS, MLA_V)


def _with_ones_rows(wt, heads, width):
    w3 = wt.reshape(heads, width, wt.shape[1])
    w3 = jnp.concatenate([w3, jnp.zeros((heads, BF16_ROWS, wt.shape[1]), wt.dtype)], axis=1)
    ones = jnp.concatenate([jnp.zeros((heads, width, 1), F32), jnp.ones((heads, BF16_ROWS, 1), F32)], axis=1)
    return w3.reshape(heads * (width + BF16_ROWS), -1).astype(BF16), ones.reshape(-1, 1)


def _pad_rows(w, top, total):
    return jnp.concatenate([jnp.zeros((top, w.shape[1]), w.dtype), w,
                            jnp.zeros((total - top - w.shape[0], w.shape[1]), w.dtype)], axis=0).astype(BF16)


def _rwkv_consts(tb):
    t = jnp.arange(tb)
    same = (t[:, None] // RW_CHUNK) == (t[None, :] // RW_CHUNK)
    tri = (same & (t[None, :] <= t[:, None])).astype(BF16)
    hl = jnp.arange(LANE) // RW_N
    seg = (hl[:, None] == hl[None, :]).astype(BF16)
    return tri, seg


def kernel(x, positions, pre_g, w_in, w_in_vres, w_out, mla_gq, mla_gkv, mla_wuq, mla_wukv, diff_lam, diff_gsub, rw_mu, rw_mu_vres, rw_w0, rw_w2, rw_a0, rw_a2, rw_v0, rw_v2, rw_kk, rw_ka, rw_rk, rw_lnw, rw_lnb, final_g):
    batch, seq, _ = x.shape
    depth = pre_g.shape[0]
    tm = min(512, seq)
    tq = min(512, seq)
    tkv = tq // 2
    tb = min(256, seq)
    assert seq % tm == 0 and seq % tq == 0 and tm % tkv == 0
    assert seq % tb == 0 and tb % RW_CHUNK == 0

    x2 = x.reshape(batch * seq, D_MODEL)
    tab = _rope_table(positions)
    pos_f = positions.astype(F32)
    posc = pos_f.reshape(seq, 1)
    posr = pos_f.reshape(seq // tq, 1, tq)
    consts = _rwkv_consts(tb)
    r1 = lambda a: a.reshape(1, -1).astype(F32)

    vfirst = None
    for layer in range(depth):
        vres = layer > 0
        w, (wdvt, dvone) = _pack_inproj(w_in[layer], w_in_vres[layer - 1] if vres else None)
        wq, wk, (wvt, vone) = _pack_mla(mla_wuq[layer], mla_wukv[layer])
        q, k, vt, d, dvt, gate, rw = _inproj(x2, r1(pre_g[layer]), w, tab, r1(mla_gq[layer]), r1(mla_gkv[layer]),
                                             wq, wk, wvt, wdvt, vone, dvone,
                                             seq=seq, tm=tm, tkv=tkv)
        o_mla = _mla_flash(q, k, vt, gate, batch=batch, seq=seq, tq=tq, n_h=MLA_HEADS)
        lam_init = 0.8 - 0.6 * math.exp(-0.3 * (layer + 1))
        gsub_col = diff_gsub[layer].reshape(-1, 1).astype(F32)
        o_diff = _diff_flash(d, dvt, gate, posc, posr, diff_lam[layer].astype(F32), gsub_col,
                             batch=batch, seq=seq, tq=tq, lam_init=lam_init)
        mu = rw_mu[layer]
        if vres:
            mu = jnp.concatenate([mu, rw_mu_vres[layer - 1], jnp.zeros((LANE - RW_MV_RANK,), mu.dtype)])
        params = [r1(mu), r1(rw_w0[layer]), _pad_rows(rw_w2[layer], 0, LANE), r1(rw_a0[layer]),
                  _pad_rows(rw_a2[layer], RW_DECAY_RANK, LANE)]
        if vres:
            params += [r1(rw_v0[layer - 1]), _pad_rows(rw_v2[layer - 1], 0, LANE)]
        params += [r1(rw_kk[layer]), r1(rw_ka[layer]), r1(rw_rk[layer]), r1(rw_lnw[layer]), r1(rw_lnb[layer])]
        o_rw, vf = _rwkv(rw, vfirst, gate, consts, params, batch=batch, seq=seq, tb=tb, cpi=4,
                          n_seq=2 if batch % 2 == 0 else 1)
        if not vres:
            vfirst = vf
        x2 = _outproj(x2, o_mla, o_diff, o_rw, w_out[layer].astype(BF16), r1(final_g),
                      tm=tm, final=(layer == depth - 1))
    return x2.reshape(batch, seq, D_MODEL)
```

```python
import functools
import math

import jax
import jax.numpy as jnp
from jax import lax
from jax.experimental import pallas as pl
from jax.experimental.pallas import tpu as pltpu

F32 = jnp.float32
BF16 = jnp.bfloat16

D_MODEL = 1024
D_MIX = 1024
NORM_EPS = 1e-6
MLA_HEADS, MLA_NOPE, MLA_ROPE, MLA_V = 6, 64, 32, 64
MLA_Q_RANK, MLA_KV_RANK = 256, 128
MLA_QK = MLA_NOPE + MLA_ROPE
ROPE_THETA = 10000.0
MLA_W = MLA_HEADS * MLA_V
DIFF_HEADS, DIFF_DH = 4, 32
DIFF_W = DIFF_HEADS * 2 * DIFF_DH
DIFF_SUBLN_EPS = 1e-5
RW_HEADS, RW_N = 6, 64
RW_W = RW_HEADS * RW_N
RW_DECAY_RANK, RW_AAA_RANK, RW_MV_RANK = 64, 64, 32
RW_GN_EPS = 64e-5
RW_SHIFT_BASE = 3 * RW_W + RW_DECAY_RANK + RW_AAA_RANK

LANE = 128
SUBLANE = 8
BF16_ROWS = 16
LOG2E = math.log2(math.e)
HP = LANE
MLA_VA = MLA_V + BF16_ROWS
DIFF_VA = 2 * DIFF_DH + BF16_ROWS
RW_CHUNK = 64
RW_N_SCRATCH = 17
RW_SKEW = 14

MLA_G = MLA_Q_RANK + MLA_KV_RANK + LANE
DIFF_G = 2 * DIFF_W
GATE_G = D_MIX
RW_G0 = RW_SHIFT_BASE
RW_G1 = RW_SHIFT_BASE + LANE
OFF_DIFF = MLA_G
OFF_GATE = OFF_DIFF + DIFF_G
OFF_RW = OFF_GATE + GATE_G

VMEM_LIMIT = 52 * 1024 * 1024

_NT = (((1,), (1,)), ((), ()))
_TN = (((0,), (0,)), ((), ()))


def _dot(a, b):
    return jnp.dot(a, b, preferred_element_type=F32)


def _dot_nt(a, b):
    return lax.dot_general(a, b, _NT, preferred_element_type=F32)


def _split_hi_lo(x):
    hi = x.astype(BF16)
    return hi, (x - hi.astype(F32)).astype(BF16)


def _sigmoid(z):
    return 1.0 / (1.0 + jnp.exp(-z))


def _silu(g):
    return g * _sigmoid(g)


def _rms(x, g, eps):
    return x * lax.rsqrt(jnp.mean(x * x, axis=-1, keepdims=True) + eps) * g


def _rope_table_kernel(pos_ref, tab_ref):
    pos = pos_ref[...]
    lane = lax.broadcasted_iota(jnp.int32, (1, LANE), 1)
    half = MLA_ROPE // 2
    in_rope = (lane >= MLA_NOPE) & (lane < MLA_QK)
    first = lane < MLA_NOPE + half
    idx = jnp.where(first, lane - MLA_NOPE, lane - MLA_NOPE - half).astype(F32)
    inv = jnp.exp(idx * (-math.log(ROPE_THETA) / half))
    ang = pos * inv
    cos = jnp.where(in_rope, jnp.cos(ang), jnp.where(lane < MLA_NOPE, 1.0, 0.0))
    sin = jnp.sin(ang)
    sin = jnp.where(in_rope, jnp.where(first, -sin, sin), 0.0)
    scale = MLA_QK ** -0.5 * LOG2E
    tab_ref[:, 0 * LANE:1 * LANE] = cos * scale
    tab_ref[:, 1 * LANE:2 * LANE] = jnp.where(first, sin, 0.0) * scale
    tab_ref[:, 2 * LANE:3 * LANE] = jnp.where(first, 0.0, sin) * scale
    tab_ref[:, 3 * LANE:4 * LANE] = cos
    tab_ref[:, 4 * LANE:5 * LANE] = sin


def _rope_table(positions):
    s = positions.shape[0]
    return pl.pallas_call(
        _rope_table_kernel,
        out_shape=jax.ShapeDtypeStruct((s, 5 * LANE), F32),
        name="rope_table",
    )(positions.astype(F32).reshape(s, 1))


def _inproj_kernel(x_ref, g_ref, w_ref, tab_ref, gq_ref, gkv_ref, wq_ref, wk_ref, wvt_ref, wdvt_ref,
                   q_ref, k_ref, vt_ref, d_ref, dvt_ref, gate_ref, rw_ref, *, tkv):
    o_kpe = MLA_Q_RANK + MLA_KV_RANK
    half = MLA_ROPE // 2
    ones = jnp.ones((BF16_ROWS, tkv), BF16)
    for t in range(vt_ref.shape[0]):
        rows = slice(t * tkv, (t + 1) * tkv)
        h = _rms(x_ref[rows, :], g_ref[...], NORM_EPS).astype(BF16)

        mla = _dot(h, w_ref[:, 0:MLA_G])
        cqn = _rms(mla[:, 0:MLA_Q_RANK], gq_ref[...], NORM_EPS).astype(BF16)
        ckvn = _rms(mla[:, MLA_Q_RANK:o_kpe], gkv_ref[...], NORM_EPS).astype(BF16)
        tab = tab_ref[rows, :]
        cos_q, sin_q1, sin_q2 = tab[:, 0:LANE], tab[:, LANE:2 * LANE], tab[:, 2 * LANE:3 * LANE]
        cos_k, sin_k = tab[:, 3 * LANE:4 * LANE], tab[:, 4 * LANE:5 * LANE]
        kg = mla[:, o_kpe:o_kpe + LANE]
        kpe = pltpu.roll(kg, MLA_NOPE, 1) * cos_k + pltpu.roll(kg, MLA_NOPE - MLA_ROPE, 1) * sin_k
        qa = _dot(cqn, wq_ref[...])
        kn = _dot(ckvn, wk_ref[...])
        for hd in range(MLA_HEADS):
            sl = slice(hd * HP, (hd + 1) * HP)
            qh = qa[:, sl]
            q_ref[rows, sl] = (qh * cos_q + pltpu.roll(qh, HP - half, 1) * sin_q1
                               + pltpu.roll(qh, half, 1) * sin_q2).astype(BF16)
            k_ref[rows, sl] = (kn[:, sl] + kpe).astype(BF16)
        for src, dst, heads, width in ((_dot_nt(wvt_ref[...], ckvn), vt_ref, MLA_HEADS, MLA_V),
                                       (_dot_nt(wdvt_ref[...], h), dvt_ref, DIFF_HEADS, 2 * DIFF_DH)):
            src = src.astype(BF16)
            for hd in range(heads):
                r0 = hd * (width + BF16_ROWS)
                dst[t, r0:r0 + width, :] = src[hd * width:(hd + 1) * width, :]
                dst[t, r0 + width:r0 + width + BF16_ROWS, :] = ones

        dd = _dot(h, w_ref[:, OFF_DIFF:OFF_GATE])
        d_ref[rows, 0:DIFF_W] = (dd[:, 0:DIFF_W] * (DIFF_DH ** -0.5 * LOG2E)).astype(BF16)
        d_ref[rows, DIFF_W:DIFF_G] = dd[:, DIFF_W:DIFF_G].astype(BF16)
        gate_ref[rows, :] = _dot(h, w_ref[:, OFF_GATE:OFF_RW]).astype(BF16)
        rw_ref[rows, :] = _dot(h, w_ref[:, OFF_RW:])


def _inproj(x2, g, w, tab, gq, gkv, wq, wk, wvt, wdvt, *, seq, tm, tkv):
    t = x2.shape[0]
    ctot = w.shape[1]
    rw_g = ctot - OFF_RW
    n_seq_tiles = seq // tm
    kv_per_tile = tm // tkv
    row = lambda i: (i, 0)
    slab = lambda i: (i, 0, 0)
    const = lambda i: (0, 0)
    full = lambda a: pl.BlockSpec(a.shape, const)
    return pl.pallas_call(
        functools.partial(_inproj_kernel, tkv=tkv),
        grid=(t // tm,),
        in_specs=[
            pl.BlockSpec((tm, D_MODEL), row),
            full(g), full(w),
            pl.BlockSpec((tm, tab.shape[1]), lambda i: (i % n_seq_tiles, 0)),
            full(gq), full(gkv), full(wq), full(wk), full(wvt), full(wdvt),
        ],
        out_specs=[
            pl.BlockSpec((tm, MLA_HEADS * HP), row),
            pl.BlockSpec((tm, MLA_HEADS * HP), row),
            pl.BlockSpec((kv_per_tile, MLA_HEADS * MLA_VA, tkv), slab),
            pl.BlockSpec((tm, DIFF_G), row),
            pl.BlockSpec((kv_per_tile, DIFF_HEADS * DIFF_VA, tkv), slab),
            pl.BlockSpec((tm, GATE_G), row),
            pl.BlockSpec((tm, rw_g), row),
        ],
        out_shape=[
            jax.ShapeDtypeStruct((t, MLA_HEADS * HP), BF16),
            jax.ShapeDtypeStruct((t, MLA_HEADS * HP), BF16),
            jax.ShapeDtypeStruct((t // tkv, MLA_HEADS * MLA_VA, tkv), BF16),
            jax.ShapeDtypeStruct((t, DIFF_G), BF16),
            jax.ShapeDtypeStruct((t // tkv, DIFF_HEADS * DIFF_VA, tkv), BF16),
            jax.ShapeDtypeStruct((t, GATE_G), BF16),
            jax.ShapeDtypeStruct((t, rw_g), F32),
        ],
        compiler_params=pltpu.CompilerParams(
            dimension_semantics=("parallel",), vmem_limit_bytes=VMEM_LIMIT),
        name="inproj",
    )(x2, g, w, tab, gq, gkv, wq, wk, wvt, wdvt)


def _softmax_step(s, m):
    m_new = jnp.maximum(m, jnp.max(s, axis=0, keepdims=True))
    return m_new, jnp.exp2(m - m_new), jnp.exp2(s - m_new).astype(BF16)


def _causal_t(n_keys, n_queries):
    return (lax.broadcasted_iota(jnp.int32, (n_keys, n_queries), 0)
            <= lax.broadcasted_iota(jnp.int32, (n_keys, n_queries), 1))


def _mla_flash_kernel(q_ref, k_ref, vt_ref, gate_ref, o_ref, *, tq, tk, n_h):
    i = pl.program_id(2)
    half = tq // 2
    pw = MLA_VA
    hs = [slice(hh * HP, (hh + 1) * HP) for hh in range(n_h)]
    ps = [slice(hh * pw, (hh + 1) * pw) for hh in range(n_h)]

    def tile(j, carry, qs, mask, n_slab):
        rows = pl.ds(pl.multiple_of(j * (n_slab * tk), n_slab * tk), n_slab * tk)
        s = [_dot_nt(k_ref[rows, hs[hh]], qs[hh]) for hh in range(n_h)]
        if mask is not None:
            s = [jnp.where(mask, x, -jnp.inf) for x in s]
        st = [_softmax_step(s[hh], carry[hh][0]) for hh in range(n_h)]
        vt = [vt_ref[j * n_slab + t] for t in range(n_slab)]
        pv = [sum(_dot(vt[t][ps[hh]], st[hh][2][t * tk:(t + 1) * tk]) for t in range(n_slab))
              for hh in range(n_h)]
        return tuple((st[hh][0], st[hh][1] * carry[hh][1] + pv[hh]) for hh in range(n_h))

    qs = [q_ref[:, hs[hh]] for hh in range(n_h)]
    one = (jnp.full((1, tq), -jnp.inf, F32), jnp.zeros((pw, tq), F32))
    carry = lax.fori_loop(0, i, lambda j, c: tile(j, c, qs, None, tq // tk), (one,) * n_h)
    carry = tile(2 * i, carry, qs, _causal_t(tk, tq), 1)
    late = tile(2 * i + 1, tuple(tuple(a[:, half:] for a in c) for c in carry),
                [q[half:] for q in qs], _causal_t(tk, half), 1)
    accs = [jnp.concatenate([carry[hh][1][:, :half], late[hh][1]], axis=1) for hh in range(n_h)]
    o = jnp.concatenate([a[0:MLA_V] / a[MLA_V:MLA_V + 1] for a in accs], axis=0).T
    o_ref[...] = (o * _silu(gate_ref[...].astype(F32))).astype(BF16)


def _mla_flash(q, k, vt, gate, *, batch, seq, tq, n_h):
    nq = seq // tq
    tk = tq // 2
    nkv = seq // tk
    ow = n_h * MLA_V
    vw = n_h * MLA_VA
    gate_blk0 = RW_W // ow
    return pl.pallas_call(
        functools.partial(_mla_flash_kernel, tq=tq, tk=tk, n_h=n_h),
        grid=(batch, MLA_HEADS // n_h, nq),
        in_specs=[
            pl.BlockSpec((tq, n_h * HP), lambda b, g, i: (b * nq + i, g)),
            pl.BlockSpec((seq, n_h * HP), lambda b, g, i: (b, g)),
            pl.BlockSpec((nkv, vw, tk), lambda b, g, i: (b, g, 0)),
            pl.BlockSpec((tq, ow), lambda b, g, i: (b * nq + i, gate_blk0 + g)),
        ],
        out_specs=pl.BlockSpec((tq, ow), lambda b, g, i: (b * nq + i, g)),
        out_shape=jax.ShapeDtypeStruct((batch * seq, MLA_W), BF16),
        compiler_params=pltpu.CompilerParams(
            dimension_semantics=("parallel", "parallel", "parallel"), vmem_limit_bytes=VMEM_LIMIT),
        name="mla_flash",
    )(q, k, vt, gate)


def _diff_flash_kernel(posc_ref, lam_ref, gsub_ref, q_ref, k_ref, vt_ref, gate_ref, o_ref, qm_ref, p_ref,
                       *, tq, tk, lam_init):
    i = pl.program_id(1)
    n_maps = 2 * DIFF_HEADS
    half = tq // 2
    q = q_ref[...]
    lane_q = lax.broadcasted_iota(jnp.int32, (tq, DIFF_W), 1)
    for hm in range(n_maps):
        own = (lane_q >= hm * DIFF_DH) & (lane_q < (hm + 1) * DIFF_DH)
        qm_ref[hm * tq:(hm + 1) * tq, :] = jnp.where(own, q, jnp.zeros_like(q))
    slopes = [LOG2E * 2.0 ** (-8.0 * (h + 1) / DIFF_HEADS) for h in range(DIFF_HEADS)]
    dv = 2 * DIFF_DH
    va = DIFF_VA

    def tile(j, carry, q0, nq, mask, n_slab):
        nk = n_slab * tk
        rows = pl.ds(pl.multiple_of(j * nk, nk), nk)
        kt = k_ref[rows, :]
        pos_k = posc_ref[rows, :]
        bias = [slope * pos_k for slope in slopes]
        st = []
        for hm in range(n_maps):
            cols = slice(hm * tq + q0, hm * tq + q0 + nq)
            s = _dot_nt(kt, qm_ref[cols, :]) + bias[hm // 2]
            if mask is not None:
                s = jnp.where(mask, s, -jnp.inf)
            m, alpha, p = _softmax_step(s, carry[hm][0])
            p_ref[0:nk, cols] = p
            st.append((m, alpha))
        vts = [vt_ref[j * n_slab + t] for t in range(n_slab)]
        new = []
        for hm in range(n_maps):
            cols = slice(hm * tq + q0, hm * tq + q0 + nq)
            vr = slice((hm // 2) * va, (hm // 2 + 1) * va)
            pv = sum(_dot(vts[t][vr], p_ref[t * tk:(t + 1) * tk, cols]) for t in range(n_slab))
            new.append((st[hm][0], st[hm][1] * carry[hm][1] + pv))
        return tuple(new)

    one = (jnp.full((1, tq), -jnp.inf, F32), jnp.zeros((va, tq), F32))
    carry = lax.fori_loop(0, i, lambda j, c: tile(j, c, 0, tq, None, tq // tk), (one,) * n_maps)
    carry = tile(2 * i, carry, 0, tq, _causal_t(tk, tq), 1)
    late = tile(2 * i + 1, tuple(tuple(a[:, half:] for a in c) for c in carry),
                half, half, _causal_t(tk, half), 1)
    accs = [jnp.concatenate([carry[hm][1][:, :half], late[hm][1]], axis=1) for hm in range(n_maps)]

    lam = lam_ref[...]
    lam_full = (jnp.exp(jnp.sum(lam[0:1] * lam[1:2], axis=-1, keepdims=True))
                - jnp.exp(jnp.sum(lam[2:3] * lam[3:4], axis=-1, keepdims=True)) + lam_init)
    heads = []
    for h in range(DIFF_HEADS):
        a0, a1 = accs[2 * h], accs[2 * h + 1]
        o = a0[0:dv] / a0[dv:dv + 1] - lam_full * (a1[0:dv] / a1[dv:dv + 1])
        ms = jnp.mean(o * o, axis=0, keepdims=True)
        heads.append(o * lax.rsqrt(ms + DIFF_SUBLN_EPS) * gsub_ref[...] * (1.0 - lam_init))
    o = jnp.concatenate(heads, axis=0).T
    o_ref[...] = (o * _silu(gate_ref[...].astype(F32))).astype(BF16)


def _diff_flash(d, dvt, gate, posc, lam, gsub_col, *, batch, seq, tq, lam_init):
    nq = seq // tq
    gate_blk = (RW_W + MLA_W) // DIFF_W
    n_maps = 2 * DIFF_HEADS
    tk = tq // 2
    nkv = seq // tk
    return pl.pallas_call(
        functools.partial(_diff_flash_kernel, tq=tq, tk=tk, lam_init=lam_init),
        grid=(batch, nq),
        in_specs=[
            pl.BlockSpec(posc.shape, lambda b, i: (0, 0)),
            pl.BlockSpec(lam.shape, lambda b, i: (0, 0)),
            pl.BlockSpec(gsub_col.shape, lambda b, i: (0, 0)),
            pl.BlockSpec((tq, DIFF_W), lambda b, i: (b * nq + i, 0)),
            pl.BlockSpec((seq, DIFF_W), lambda b, i: (b, 1)),
            pl.BlockSpec((nkv, DIFF_HEADS * DIFF_VA, tk), lambda b, i: (b, 0, 0)),
            pl.BlockSpec((tq, DIFF_W), lambda b, i: (b * nq + i, gate_blk)),
        ],
        out_specs=pl.BlockSpec((tq, DIFF_W), lambda b, i: (b * nq + i, 0)),
        out_shape=jax.ShapeDtypeStruct((batch * seq, DIFF_W), BF16),
        scratch_shapes=[pltpu.VMEM((n_maps * tq, DIFF_W), BF16), pltpu.VMEM((tq, n_maps * tq), BF16)],
        compiler_params=pltpu.CompilerParams(
            dimension_semantics=("parallel", "parallel"), vmem_limit_bytes=VMEM_LIMIT),
        name="diff_flash",
    )(posc, lam, gsub_col, d, d, dvt, gate)


def _rwkv_kernel(*refs, tb, has_vres, cpi, n_seq, n_shared):
    n_in = 3 if has_vres else 2
    n_io = len(refs) - n_seq * RW_N_SCRATCH
    live = []
    for e in range(n_seq):
        views = [r if n_in <= idx < n_in + n_shared else r.at[e] for idx, r in enumerate(refs[:n_io])]
        views += refs[n_io + e * RW_N_SCRATCH:n_io + (e + 1) * RW_N_SCRATCH]
        live.append(_rwkv_seq(*views, tb=tb, has_vres=has_vres, cpi=cpi, skew=e * RW_SKEW))
    while live:
        live = [g for g in live if next(g, "done") != "done"]


def _rwkv_seq(*refs, tb, has_vres, cpi, skew):
    for _ in range(skew):
        yield
    n_sc = RW_N_SCRATCH
    if has_vres:
        (p_ref, vf_ref, gate_ref, tri_ref, seg_ref, mu_ref, w0_ref, w2_ref, a0_ref, a2_ref,
         v0_ref, v2_ref, kk_ref, ka_ref, rk_ref, lnw_ref, lnb_ref, y_ref) = refs[:-n_sc]
    else:
        (p_ref, gate_ref, tri_ref, seg_ref, mu_ref, w0_ref, w2_ref, a0_ref, a2_ref,
         kk_ref, ka_ref, rk_ref, lnw_ref, lnb_ref, y_ref, vfo_ref) = refs[:-n_sc]
    (state_ref, carry_ref, at_ref, rt_ref, bt_ref, kt_ref, vv_ref, bw_ref, kw_ref, tot_ref, yy_ref,
     lhs_ref, xav_ref, ay_ref, tc_ref, cc_ref, wlt_ref) = refs[-n_sc:]

    @pl.when(pl.program_id(1) == 0)
    def _():
        state_ref[...] = jnp.zeros_like(state_ref)
        carry_ref[...] = jnp.zeros_like(carry_ref)

    seg = seg_ref[...]
    pair_lanes = [slice(pr * LANE, (pr + 1) * LANE) for pr in range(RW_HEADS // 2)]

    def segsum(x):
        xb = x.astype(BF16)
        return jnp.concatenate([_dot(xb[:, ls], seg) for ls in pair_lanes], axis=1)

    p = p_ref[...]
    prev = pltpu.roll(p, 1, 0)
    top = jnp.where(lax.broadcasted_iota(jnp.int32, (SUBLANE, 1), 0) == 0, carry_ref[...], prev[0:SUBLANE])
    prev = jnp.concatenate([top, prev[SUBLANE:]], axis=0)
    carry_ref[...] = p[tb - 1:tb, :]
    yield
    xs = p + (prev - p) * mu_ref[...]
    yield
    r = xs[:, 0:RW_W]
    k = xs[:, RW_W:2 * RW_W]
    v = xs[:, 2 * RW_W:3 * RW_W]
    hwa = xs[:, 3 * RW_W:3 * RW_W + LANE]
    lw = -math.exp(-0.5) * _sigmoid(w0_ref[...] + _dot(jnp.tanh(hwa).astype(BF16), w2_ref[...]))
    yield
    a = _sigmoid(a0_ref[...] + _dot(hwa.astype(BF16), a2_ref[...]))
    yield
    if has_vres:
        hv = xs[:, RW_SHIFT_BASE:RW_SHIFT_BASE + LANE]
        v = v + (vf_ref[...] - v) * _sigmoid(v0_ref[...] + _dot(hv.astype(BF16), v2_ref[...]))
    else:
        vfo_ref[...] = v
    yield
    kk = k * kk_ref[...]
    kk = kk * lax.rsqrt(jnp.maximum(segsum(kk * kk), 1e-24))
    yield
    k = k * (1.0 + (a - 1.0) * ka_ref[...])
    yield

    lw_hi, lw_lo = _split_hi_lo(lw)
    cum = _dot(tri_ref[...], lw_hi) + _dot(tri_ref[...], lw_lo)
    tot = jnp.concatenate([jnp.broadcast_to(cum[e - 1:e], (RW_CHUNK, RW_W))
                           for e in range(RW_CHUNK, tb + 1, RW_CHUNK)], axis=0)
    yield
    w_inv = jnp.exp(-cum)
    w_end = jnp.exp(tot - cum)
    yield
    kka = kk * a
    rt_ref[...] = (r * jnp.exp(cum)).astype(BF16)
    yield
    at_ref[...] = (-kk * jnp.exp(cum - lw)).astype(BF16)
    yield
    bt_ref[...] = (kka * w_inv).astype(BF16)
    kt_ref[...] = (k * w_inv).astype(BF16)
    yield
    vv_ref[...] = v.astype(BF16)
    bw_ref[...] = kka * w_end
    kw_ref[...] = k * w_end
    tot_ref[...] = tot
    yield

    c = RW_CHUNK
    n_pairs = RW_HEADS // 2
    lane_c = lax.broadcasted_iota(jnp.int32, (c, LANE), 1)
    even_c = lane_c < RW_N
    even_2c = lax.broadcasted_iota(jnp.int32, (2 * c, LANE), 1) < RW_N
    ri = lax.broadcasted_iota(jnp.int32, (2 * c, 2 * c), 0)
    ci = lax.broadcasted_iota(jnp.int32, (2 * c, 2 * c), 1)
    cm = jnp.where(ci >= c, ci - c, ci)
    keep = ((ri >= c) & (cm <= ri - c)) | ((ri < c) & (cm < ri))
    same_head = (ri < c) == (ci < c)
    eye2 = (lax.broadcasted_iota(jnp.int32, (c, LANE), 0)
            == jnp.where(lane_c >= c, lane_c - c, lane_c)).astype(F32)

    def bdiag(m):
        zero = jnp.zeros_like(m)
        return jnp.concatenate([jnp.where(even_c, m, zero), jnp.where(even_c, zero, m)], axis=0)

    def adiag(m):
        zero = jnp.zeros_like(m)
        return jnp.concatenate([jnp.where(even_c, zero, m), jnp.where(even_c, m, zero)], axis=0)

    def phase_a(i):
        probs = [(i * cpi + cc, pr) for cc in range(cpi) for pr in range(n_pairs)]
        rows = [pl.ds(pl.multiple_of(ic * c, c), c) for ic, _ in probs]
        rows2 = [pl.ds(pl.multiple_of(ic * 2 * c, 2 * c), 2 * c) for ic, _ in probs]
        ls = [slice(pr * LANE, (pr + 1) * LANE) for _, pr in probs]
        rng = range(len(probs))
        at = [at_ref[rows[j], ls[j]] for j in rng]
        rt = [rt_ref[rows[j], ls[j]] for j in rng]
        bt = [bt_ref[rows[j], ls[j]] for j in rng]
        kt = [kt_ref[rows[j], ls[j]] for j in rng]
        vh = [vv_ref[rows[j], ls[j]] for j in rng]
        xar = [jnp.concatenate([at[j], rt[j]], axis=0) for j in rng]
        zero_b = jnp.zeros((2 * c, LANE), BF16)
        aa_e = [jnp.where(keep, _dot_nt(jnp.where(even_2c, xar[j], zero_b),
                                        jnp.concatenate([bt[j], kt[j]], axis=0)), 0.0) for j in rng]
        yield
        aa_o = [jnp.where(keep, _dot_nt(jnp.where(even_2c, zero_b, xar[j]),
                                        jnp.concatenate([kt[j], bt[j]], axis=0)), 0.0) for j in rng]
        yield
        nn = [jnp.where(even_c, aa_e[j][0:c], aa_o[j][0:c]) for j in rng]
        aak = [jnp.where(even_c, aa_o[j][0:c], aa_e[j][0:c]) for j in rng]
        for j in rng:
            pr = probs[j][1]
            ay_ref[rows[j], 2 * pr * LANE:(2 * pr + 1) * LANE] = jnp.where(
                even_c, aa_e[j][c:2 * c], aa_o[j][c:2 * c]).astype(BF16)
            ay_ref[rows[j], (2 * pr + 1) * LANE:(2 * pr + 2) * LANE] = jnp.where(
                even_c, aa_o[j][c:2 * c], aa_e[j][c:2 * c]).astype(BF16)
        yield
        av = [_dot(aak[j].astype(BF16), adiag(vh[j])) for j in rng]
        yield
        x = [eye2 + nn[j] for j in rng]
        pw = [nn[j].astype(BF16) for j in rng]
        pw = [_dot(pw[j], bdiag(pw[j])).astype(BF16) for j in rng]
        yield
        n_sq = int(math.log2(c)) - 1
        for it in range(n_sq):
            last = it == n_sq - 1
            lhs = [x[j].astype(BF16) if last else jnp.concatenate([x[j].astype(BF16), pw[j]], axis=0) for j in rng]
            out = [_dot(lhs[j], bdiag(pw[j])) for j in rng]
            x = [x[j] + out[j][0:c] for j in rng]
            if not last:
                pw = [out[j][c:2 * c].astype(BF16) for j in rng]
            yield
        fin = [_dot(x[j].astype(BF16), jnp.concatenate([bdiag(at[j]), bdiag(av[j].astype(BF16))], axis=1))
               for j in rng]
        yield
        ahat = [fin[j][:, 0:LANE].astype(BF16) for j in rng]
        xav = [fin[j][:, LANE:2 * LANE] for j in rng]
        zt = [jnp.concatenate([bw_ref[rows[j], ls[j]], kw_ref[rows[j], ls[j]]], axis=0).T.astype(BF16) for j in rng]
        zero_c = jnp.zeros((c, LANE), BF16)
        tcb = [_dot(zt[j], jnp.concatenate([ahat[j], zero_c], axis=0)) for j in rng]
        yield
        ccv = [_dot(zt[j], jnp.concatenate([xav[j].astype(BF16), vh[j]], axis=0)) for j in rng]
        yield
        for j in rng:
            lhs_ref[rows2[j], ls[j]] = jnp.concatenate([ahat[j], rt[j]], axis=0)
            xav_ref[rows[j], ls[j]] = xav[j]
            tc_ref[rows2[j], ls[j]] = jnp.where(same_head, tcb[j], 0.0).astype(BF16)
            cc_ref[rows2[j], ls[j]] = jnp.where(same_head, ccv[j], 0.0)
            tot_row = tot_ref[pl.ds(probs[j][0] * c, 1), ls[j]]
            wlt_ref[rows2[j], ls[j]] = jnp.exp(jnp.broadcast_to(tot_row, (2 * c, LANE)).T)
        yield

    for i in range(tb // (c * cpi)):
        yield from phase_a(i)

    ls = [slice(pr * LANE, (pr + 1) * LANE) for pr in range(n_pairs)]
    rng = range(n_pairs)
    mt = [state_ref[pr] for pr in rng]
    for ic in range(tb // c):
        rows = slice(ic * c, (ic + 1) * c)
        rows2 = slice(ic * 2 * c, (ic + 1) * 2 * c)
        mtb = [mt[pr].astype(BF16) for pr in rng]
        nxt = [_dot(tc_ref[rows2, ls[pr]], mtb[pr]) for pr in rng]
        mt = [mt[pr] * wlt_ref[rows2, ls[pr]] + nxt[pr] + cc_ref[rows2, ls[pr]] for pr in rng]
        out = [_dot(lhs_ref[rows2, ls[pr]], mtb[pr]) for pr in rng]
        ub = [(out[pr][0:c] + xav_ref[rows, ls[pr]]).astype(BF16) for pr in rng]
        vh = [vv_ref[rows, ls[pr]] for pr in rng]
        for pr in rng:
            uv = jnp.concatenate([bdiag(ub[pr]), adiag(vh[pr])], axis=0)
            yy_ref[rows, ls[pr]] = out[pr][c:2 * c] + _dot(ay_ref[rows, 2 * pr * LANE:(2 * pr + 2) * LANE], uv)
        yield
    for pr in rng:
        state_ref[pr] = mt[pr]

    y = yy_ref[...]
    inv_n = 1.0 / RW_N
    dlt = y - segsum(y) * inv_n
    yield
    var = segsum(dlt * dlt) * inv_n
    yield
    y = dlt * lax.rsqrt(var + RW_GN_EPS) * lnw_ref[...] + lnb_ref[...]
    yield
    y = y + segsum(r * k * rk_ref[...]) * v
    y_ref[...] = (y * _silu(gate_ref[...].astype(F32))).astype(BF16)


def _rwkv(p, vfirst, gate, consts, params, *, batch, seq, tb, cpi, n_seq):
    has_vres = vfirst is not None
    cw = p.shape[1]
    per_seq = lambda a: a.reshape(batch // n_seq, n_seq, seq, a.shape[-1])
    blk = lambda w: pl.BlockSpec((None, n_seq, tb, w), lambda b, i: (b, 0, i, 0))
    full = lambda a: pl.BlockSpec(a.shape, lambda b, i: (0,) * a.ndim)
    shared = list(consts) + list(params)
    ins = [per_seq(p)] + ([per_seq(vfirst)] if has_vres else []) + [per_seq(gate)] + shared
    in_specs = [blk(cw)] + ([blk(RW_W)] if has_vres else []) + [blk(RW_W)] + [full(a) for a in shared]
    out_specs = [blk(RW_W)]
    out_shape = [jax.ShapeDtypeStruct((batch // n_seq, n_seq, seq, RW_W), BF16)]
    if not has_vres:
        out_specs.append(blk(RW_W))
        out_shape.append(jax.ShapeDtypeStruct((batch // n_seq, n_seq, seq, RW_W), F32))
    vm = lambda rows, cols, dt: pltpu.VMEM((rows, cols), dt)
    scratch = [pltpu.VMEM((RW_HEADS // 2, LANE, LANE), F32), vm(1, cw, F32),
               vm(tb, RW_W, BF16), vm(tb, RW_W, BF16), vm(tb, RW_W, BF16), vm(tb, RW_W, BF16), vm(tb, RW_W, BF16),
               vm(tb, RW_W, F32), vm(tb, RW_W, F32), vm(tb, RW_W, F32), vm(tb, RW_W, F32),
               vm(2 * tb, RW_W, BF16), vm(tb, RW_W, F32), vm(tb, 2 * RW_W, BF16),
               vm(2 * tb, RW_W, BF16), vm(2 * tb, RW_W, F32), vm(2 * tb, RW_W, F32)]
    assert len(scratch) == RW_N_SCRATCH
    outs = pl.pallas_call(
        functools.partial(_rwkv_kernel, tb=tb, has_vres=has_vres, cpi=cpi, n_seq=n_seq, n_shared=len(shared)),
        grid=(batch // n_seq, seq // tb),
        in_specs=in_specs, out_specs=out_specs, out_shape=out_shape,
        scratch_shapes=scratch * n_seq,
        compiler_params=pltpu.CompilerParams(
            dimension_semantics=("parallel", "arbitrary"), vmem_limit_bytes=VMEM_LIMIT),
        name="rwkv7",
    )(*ins)
    outs = [o.reshape(batch * seq, RW_W) for o in outs]
    return outs if not has_vres else (outs[0], None)


def _outproj_kernel(x_ref, om_ref, od_ref, orw_ref, w_ref, fg_ref, o_ref, *, final):
    mixed = jnp.concatenate([om_ref[...], od_ref[...], orw_ref[...]], axis=1)
    y = x_ref[...] + _dot(mixed, w_ref[...])
    if final:
        y = _rms(y, fg_ref[...], NORM_EPS)
    o_ref[...] = y


def _outproj(x2, om, od, orw, w, fg, *, tm, final):
    t = x2.shape[0]
    row = lambda i: (i, 0)
    const = lambda i: (0, 0)
    return pl.pallas_call(
        functools.partial(_outproj_kernel, final=final),
        grid=(t // tm,),
        in_specs=[pl.BlockSpec((tm, D_MODEL), row), pl.BlockSpec((tm, MLA_W), row),
                  pl.BlockSpec((tm, DIFF_W), row), pl.BlockSpec((tm, RW_W), row),
                  pl.BlockSpec(w.shape, const), pl.BlockSpec(fg.shape, const)],
        out_specs=pl.BlockSpec((tm, D_MODEL), row),
        out_shape=jax.ShapeDtypeStruct((t, D_MODEL), F32),
        compiler_params=pltpu.CompilerParams(
            dimension_semantics=("parallel",), vmem_limit_bytes=VMEM_LIMIT),
        name="outproj",
    )(x2, om, od, orw, w, fg)


def _pack_inproj(w_in, w_vres):
    d = w_in.shape[0]
    o = 0
    cq = w_in[:, o:o + MLA_Q_RANK]; o += MLA_Q_RANK
    ckv = w_in[:, o:o + MLA_KV_RANK]; o += MLA_KV_RANK
    kpe = w_in[:, o:o + MLA_ROPE]; o += MLA_ROPE
    dqk = w_in[:, o:o + DIFF_G]; o += DIFF_G
    dv = w_in[:, o:o + DIFF_W]; o += DIFF_W
    gate = w_in[:, o:o + D_MIX]; o += D_MIX
    rw = w_in[:, o:]
    half = MLA_ROPE // 2
    z = lambda n: jnp.zeros((d, n), w_in.dtype)
    kp = jnp.concatenate([kpe, kpe[:, half:], kpe[:, :half], z(LANE - 2 * MLA_ROPE)], axis=1)
    g_mla, g_diff, g_rw = gate[:, :MLA_W], gate[:, MLA_W:MLA_W + DIFF_W], gate[:, MLA_W + DIFF_W:]
    cols = [cq, ckv, kp, dqk, g_rw, g_mla, g_diff, rw]
    if w_vres is not None:
        cols += [w_vres, z(LANE - RW_MV_RANK)]
    return jnp.concatenate(cols, axis=1).astype(BF16), dv.T.astype(BF16)


def _pack_mla(w_uq, w_ukv):
    rq, rkv = w_uq.shape[0], w_ukv.shape[0]
    q3 = w_uq.reshape(rq, MLA_HEADS, MLA_QK)
    zq = lambda n: jnp.zeros((rq, MLA_HEADS, n), w_uq.dtype)
    wq = jnp.concatenate([q3, zq(HP - MLA_QK)], axis=2)
    kv3 = w_ukv.reshape(rkv, MLA_HEADS, MLA_NOPE + MLA_V)
    wk = jnp.concatenate([kv3[:, :, :MLA_NOPE], jnp.zeros((rkv, MLA_HEADS, HP - MLA_NOPE), w_ukv.dtype)], axis=2)
    wv = kv3[:, :, MLA_NOPE:]
    flat = lambda a: a.reshape(a.shape[0], -1).astype(BF16)
    return flat(wq), flat(wk), flat(wv).T


def _pad_rows(w, top, total):
    return jnp.concatenate([jnp.zeros((top, w.shape[1]), w.dtype), w,
                            jnp.zeros((total - top - w.shape[0], w.shape[1]), w.dtype)], axis=0).astype(BF16)


def _rwkv_consts(tb):
    t = jnp.arange(tb)
    same = (t[:, None] // RW_CHUNK) == (t[None, :] // RW_CHUNK)
    tri = (same & (t[None, :] <= t[:, None])).astype(BF16)
    hl = jnp.arange(LANE) // RW_N
    seg = (hl[:, None] == hl[None, :]).astype(BF16)
    return tri, seg


def kernel(x, positions, pre_g, w_in, w_in_vres, w_out, mla_gq, mla_gkv, mla_wuq, mla_wukv, diff_lam, diff_gsub, rw_mu, rw_mu_vres, rw_w0, rw_w2, rw_a0, rw_a2, rw_v0, rw_v2, rw_kk, rw_ka, rw_rk, rw_lnw, rw_lnb, final_g):
    batch, seq, _ = x.shape
    depth = pre_g.shape[0]
    tm = min(512, seq)
    tq = min(512, seq)
    tkv = tq // 2
    tb = min(256, seq)
    assert seq % tm == 0 and seq % tq == 0 and tm % tkv == 0
    assert seq % tb == 0 and tb % RW_CHUNK == 0

    x2 = x.reshape(batch * seq, D_MODEL)
    tab = _rope_table(positions)
    pos_f = positions.astype(F32)
    posc = pos_f.reshape(seq, 1)
    consts = _rwkv_consts(tb)
    r1 = lambda a: a.reshape(1, -1).astype(F32)

    vfirst = None
    for layer in range(depth):
        vres = layer > 0
        w, wdvt = _pack_inproj(w_in[layer], w_in_vres[layer - 1] if vres else None)
        wq, wk, wvt = _pack_mla(mla_wuq[layer], mla_wukv[layer])
        q, k, vt, d, dvt, gate, rw = _inproj(x2, r1(pre_g[layer]), w, tab, r1(mla_gq[layer]), r1(mla_gkv[layer]),
                                             wq, wk, wvt, wdvt,
                                             seq=seq, tm=tm, tkv=tkv)
        o_mla = _mla_flash(q, k, vt, gate, batch=batch, seq=seq, tq=tq, n_h=MLA_HEADS)
        lam_init = 0.8 - 0.6 * math.exp(-0.3 * (layer + 1))
        gsub_col = diff_gsub[layer].reshape(-1, 1).astype(F32)
        o_diff = _diff_flash(d, dvt, gate, posc, diff_lam[layer].astype(F32), gsub_col,
                             batch=batch, seq=seq, tq=tq, lam_init=lam_init)
        mu = rw_mu[layer]
        if vres:
            mu = jnp.concatenate([mu, rw_mu_vres[layer - 1], jnp.zeros((LANE - RW_MV_RANK,), mu.dtype)])
        params = [r1(mu), r1(rw_w0[layer]), _pad_rows(rw_w2[layer], 0, LANE), r1(rw_a0[layer]),
                  _pad_rows(rw_a2[layer], RW_DECAY_RANK, LANE)]
        if vres:
            params += [r1(rw_v0[layer - 1]), _pad_rows(rw_v2[layer - 1], 0, LANE)]
        params += [r1(rw_kk[layer]), r1(rw_ka[layer]), r1(rw_rk[layer]), r1(rw_lnw[layer]), r1(rw_lnb[layer])]
        o_rw, vf = _rwkv(rw, vfirst, gate, consts, params, batch=batch, seq=seq, tb=tb, cpi=4,
                          n_seq=2 if batch % 2 == 0 else 1)
        if not vres:
            vfirst = vf
        x2 = _outproj(x2, o_mla, o_diff, o_rw, w_out[layer].astype(BF16), r1(final_g),
                      tm=tm, final=(layer == depth - 1))
    return x2.reshape(batch, seq, D_MODEL)
```

```python
import functools
import math

import jax
import jax.numpy as jnp
from jax import lax
from jax.experimental import pallas as pl
from jax.experimental.pallas import tpu as pltpu

F32 = jnp.float32
BF16 = jnp.bfloat16

D_MODEL = 1024
D_MIX = 1024
NORM_EPS = 1e-6
MLA_HEADS, MLA_NOPE, MLA_ROPE, MLA_V = 6, 64, 32, 64
MLA_Q_RANK, MLA_KV_RANK = 256, 128
MLA_QK = MLA_NOPE + MLA_ROPE
ROPE_THETA = 10000.0
MLA_W = MLA_HEADS * MLA_V
DIFF_HEADS, DIFF_DH = 4, 32
DIFF_W = DIFF_HEADS * 2 * DIFF_DH
DIFF_SUBLN_EPS = 1e-5
RW_HEADS, RW_N = 6, 64
RW_W = RW_HEADS * RW_N
RW_DECAY_RANK, RW_AAA_RANK, RW_MV_RANK = 64, 64, 32
RW_GN_EPS = 64e-5
RW_SHIFT_BASE = 3 * RW_W + RW_DECAY_RANK + RW_AAA_RANK

LANE = 128
SUBLANE = 8
BF16_ROWS = 16
LOG2E = math.log2(math.e)
HP = LANE
MLA_VA = MLA_V + BF16_ROWS
DIFF_VA = 2 * DIFF_DH + BF16_ROWS
RW_CHUNK = 64
RW_N_SCRATCH = 17
RW_SKEW = 14

MLA_G = MLA_Q_RANK + MLA_KV_RANK + LANE
DIFF_G = 2 * DIFF_W
GATE_G = D_MIX
RW_G0 = RW_SHIFT_BASE
RW_G1 = RW_SHIFT_BASE + LANE
OFF_DIFF = MLA_G
OFF_GATE = OFF_DIFF + DIFF_G
OFF_RW = OFF_GATE + GATE_G

VMEM_LIMIT = 52 * 1024 * 1024

_NT = (((1,), (1,)), ((), ()))
_TN = (((0,), (0,)), ((), ()))


def _dot(a, b):
    return jnp.dot(a, b, preferred_element_type=F32)


def _dot_nt(a, b):
    return lax.dot_general(a, b, _NT, preferred_element_type=F32)


def _split_hi_lo(x):
    hi = x.astype(BF16)
    return hi, (x - hi.astype(F32)).astype(BF16)


def _sigmoid(z):
    return 1.0 / (1.0 + jnp.exp(-z))


def _silu(g):
    return g * _sigmoid(g)


def _rms(x, g, eps):
    return x * lax.rsqrt(jnp.mean(x * x, axis=-1, keepdims=True) + eps) * g


def _rope_table_kernel(pos_ref, tab_ref):
    pos = pos_ref[...]
    lane = lax.broadcasted_iota(jnp.int32, (1, LANE), 1)
    half = MLA_ROPE // 2
    in_rope = (lane >= MLA_NOPE) & (lane < MLA_QK)
    first = lane < MLA_NOPE + half
    idx = jnp.where(first, lane - MLA_NOPE, lane - MLA_NOPE - half).astype(F32)
    inv = jnp.exp(idx * (-math.log(ROPE_THETA) / half))
    ang = pos * inv
    cos = jnp.where(in_rope, jnp.cos(ang), jnp.where(lane < MLA_NOPE, 1.0, 0.0))
    sin = jnp.sin(ang)
    sin = jnp.where(in_rope, jnp.where(first, -sin, sin), 0.0)
    scale = MLA_QK ** -0.5 * LOG2E
    tab_ref[:, 0 * LANE:1 * LANE] = cos * scale
    tab_ref[:, 1 * LANE:2 * LANE] = jnp.where(first, sin, 0.0) * scale
    tab_ref[:, 2 * LANE:3 * LANE] = jnp.where(first, 0.0, sin) * scale
    tab_ref[:, 3 * LANE:4 * LANE] = cos
    tab_ref[:, 4 * LANE:5 * LANE] = sin


def _rope_table(positions):
    s = positions.shape[0]
    return pl.pallas_call(
        _rope_table_kernel,
        out_shape=jax.ShapeDtypeStruct((s, 5 * LANE), F32),
        name="rope_table",
    )(positions.astype(F32).reshape(s, 1))


def _inproj_kernel(x_ref, g_ref, w_ref, tab_ref, gq_ref, gkv_ref, wq_ref, wk_ref, wvt_ref, wdvt_ref,
                   q_ref, k_ref, vt_ref, d_ref, dvt_ref, gate_ref, rw_ref, *, tkv):
    o_kpe = MLA_Q_RANK + MLA_KV_RANK
    half = MLA_ROPE // 2
    ones = jnp.ones((BF16_ROWS, tkv), BF16)
    for t in range(vt_ref.shape[0]):
        rows = slice(t * tkv, (t + 1) * tkv)
        h = _rms(x_ref[rows, :], g_ref[...], NORM_EPS).astype(BF16)

        mla = _dot(h, w_ref[:, 0:MLA_G])
        cqn = _rms(mla[:, 0:MLA_Q_RANK], gq_ref[...], NORM_EPS).astype(BF16)
        ckvn = _rms(mla[:, MLA_Q_RANK:o_kpe], gkv_ref[...], NORM_EPS).astype(BF16)
        tab = tab_ref[rows, :]
        cos_q, sin_q1, sin_q2 = tab[:, 0:LANE], tab[:, LANE:2 * LANE], tab[:, 2 * LANE:3 * LANE]
        cos_k, sin_k = tab[:, 3 * LANE:4 * LANE], tab[:, 4 * LANE:5 * LANE]
        kg = mla[:, o_kpe:o_kpe + LANE]
        kpe = pltpu.roll(kg, MLA_NOPE, 1) * cos_k + pltpu.roll(kg, MLA_NOPE - MLA_ROPE, 1) * sin_k
        qa = _dot(cqn, wq_ref[...])
        kn = _dot(ckvn, wk_ref[...])
        for hd in range(MLA_HEADS):
            sl = slice(hd * HP, (hd + 1) * HP)
            qh = qa[:, sl]
            q_ref[rows, sl] = (qh * cos_q + pltpu.roll(qh, HP - half, 1) * sin_q1
                               + pltpu.roll(qh, half, 1) * sin_q2).astype(BF16)
            k_ref[rows, sl] = (kn[:, sl] + kpe).astype(BF16)
        for src, dst, heads, width in ((_dot_nt(wvt_ref[...], ckvn), vt_ref, MLA_HEADS, MLA_V),
                                       (_dot_nt(wdvt_ref[...], h), dvt_ref, DIFF_HEADS, 2 * DIFF_DH)):
            src = src.astype(BF16)
            for hd in range(heads):
                r0 = hd * (width + BF16_ROWS)
                dst[t, r0:r0 + width, :] = src[hd * width:(hd + 1) * width, :]
                dst[t, r0 + width:r0 + width + BF16_ROWS, :] = ones

        dd = _dot(h, w_ref[:, OFF_DIFF:OFF_GATE])
        d_ref[rows, 0:DIFF_W] = (dd[:, 0:DIFF_W] * (DIFF_DH ** -0.5 * LOG2E)).astype(BF16)
        d_ref[rows, DIFF_W:DIFF_G] = dd[:, DIFF_W:DIFF_G].astype(BF16)
        gate_ref[rows, :] = _dot(h, w_ref[:, OFF_GATE:OFF_RW]).astype(BF16)
        rw_ref[rows, :] = _dot(h, w_ref[:, OFF_RW:])


def _inproj(x2, g, w, tab, gq, gkv, wq, wk, wvt, wdvt, *, seq, tm, tkv):
    t = x2.shape[0]
    ctot = w.shape[1]
    rw_g = ctot - OFF_RW
    n_seq_tiles = seq // tm
    kv_per_tile = tm // tkv
    row = lambda i: (i, 0)
    slab = lambda i: (i, 0, 0)
    const = lambda i: (0, 0)
    full = lambda a: pl.BlockSpec(a.shape, const)
    return pl.pallas_call(
        functools.partial(_inproj_kernel, tkv=tkv),
        grid=(t // tm,),
        in_specs=[
            pl.BlockSpec((tm, D_MODEL), row),
            full(g), full(w),
            pl.BlockSpec((tm, tab.shape[1]), lambda i: (i % n_seq_tiles, 0)),
            full(gq), full(gkv), full(wq), full(wk), full(wvt), full(wdvt),
        ],
        out_specs=[
            pl.BlockSpec((tm, MLA_HEADS * HP), row),
            pl.BlockSpec((tm, MLA_HEADS * HP), row),
            pl.BlockSpec((kv_per_tile, MLA_HEADS * MLA_VA, tkv), slab),
            pl.BlockSpec((tm, DIFF_G), row),
            pl.BlockSpec((kv_per_tile, DIFF_HEADS * DIFF_VA, tkv), slab),
            pl.BlockSpec((tm, GATE_G), row),
            pl.BlockSpec((tm, rw_g), row),
        ],
        out_shape=[
            jax.ShapeDtypeStruct((t, MLA_HEADS * HP), BF16),
            jax.ShapeDtypeStruct((t, MLA_HEADS * HP), BF16),
            jax.ShapeDtypeStruct((t // tkv, MLA_HEADS * MLA_VA, tkv), BF16),
            jax.ShapeDtypeStruct((t, DIFF_G), BF16),
            jax.ShapeDtypeStruct((t // tkv, DIFF_HEADS * DIFF_VA, tkv), BF16),
            jax.ShapeDtypeStruct((t, GATE_G), BF16),
            jax.ShapeDtypeStruct((t, rw_g), F32),
        ],
        compiler_params=pltpu.CompilerParams(
            dimension_semantics=("parallel",), vmem_limit_bytes=VMEM_LIMIT),
        name="inproj",
    )(x2, g, w, tab, gq, gkv, wq, wk, wvt, wdvt)


def _softmax_step(s, m):
    m_new = jnp.maximum(m, jnp.max(s, axis=0, keepdims=True))
    return m_new, jnp.exp2(m - m_new), jnp.exp2(s - m_new).astype(BF16)


def _causal_t(n_keys, n_queries):
    return (lax.broadcasted_iota(jnp.int32, (n_keys, n_queries), 0)
            <= lax.broadcasted_iota(jnp.int32, (n_keys, n_queries), 1))


def _mla_flash_kernel(q_ref, k_ref, vt_ref, gate_ref, o_ref, *, tq, tk, n_h):
    i = pl.program_id(2)
    half = tq // 2
    pw = MLA_VA
    hs = [slice(hh * HP, (hh + 1) * HP) for hh in range(n_h)]
    ps = [slice(hh * pw, (hh + 1) * pw) for hh in range(n_h)]

    def tile(j, carry, qs, mask, n_slab):
        rows = pl.ds(pl.multiple_of(j * (n_slab * tk), n_slab * tk), n_slab * tk)
        s = [_dot_nt(k_ref[rows, hs[hh]], qs[hh]) for hh in range(n_h)]
        if mask is not None:
            s = [jnp.where(mask, x, -jnp.inf) for x in s]
        st = [_softmax_step(s[hh], carry[hh][0]) for hh in range(n_h)]
        vt = [vt_ref[j * n_slab + t] for t in range(n_slab)]
        pv = [sum(_dot(vt[t][ps[hh]], st[hh][2][t * tk:(t + 1) * tk]) for t in range(n_slab))
              for hh in range(n_h)]
        return tuple((st[hh][0], st[hh][1] * carry[hh][1] + pv[hh]) for hh in range(n_h))

    qs = [q_ref[:, hs[hh]] for hh in range(n_h)]
    one = (jnp.full((1, tq), -jnp.inf, F32), jnp.zeros((pw, tq), F32))
    carry = lax.fori_loop(0, i, lambda j, c: tile(j, c, qs, None, tq // tk), (one,) * n_h)
    carry = tile(2 * i, carry, qs, _causal_t(tk, tq), 1)
    late = tile(2 * i + 1, tuple(tuple(a[:, half:] for a in c) for c in carry),
                [q[half:] for q in qs], _causal_t(tk, half), 1)
    accs = [jnp.concatenate([carry[hh][1][:, :half], late[hh][1]], axis=1) for hh in range(n_h)]
    o = jnp.concatenate([a[0:MLA_V] / a[MLA_V:MLA_V + 1] for a in accs], axis=0).T
    o_ref[...] = (o * _silu(gate_ref[...].astype(F32))).astype(BF16)


def _mla_flash(q, k, vt, gate, *, batch, seq, tq, n_h):
    nq = seq // tq
    tk = tq // 2
    nkv = seq // tk
    ow = n_h * MLA_V
    vw = n_h * MLA_VA
    gate_blk0 = RW_W // ow
    return pl.pallas_call(
        functools.partial(_mla_flash_kernel, tq=tq, tk=tk, n_h=n_h),
        grid=(batch, MLA_HEADS // n_h, nq),
        in_specs=[
            pl.BlockSpec((tq, n_h * HP), lambda b, g, i: (b * nq + i, g)),
            pl.BlockSpec((seq, n_h * HP), lambda b, g, i: (b, g)),
            pl.BlockSpec((nkv, vw, tk), lambda b, g, i: (b, g, 0)),
            pl.BlockSpec((tq, ow), lambda b, g, i: (b * nq + i, gate_blk0 + g)),
        ],
        out_specs=pl.BlockSpec((tq, ow), lambda b, g, i: (b * nq + i, g)),
        out_shape=jax.ShapeDtypeStruct((batch * seq, MLA_W), BF16),
        compiler_params=pltpu.CompilerParams(
            dimension_semantics=("parallel", "parallel", "parallel"), vmem_limit_bytes=VMEM_LIMIT),
        name="mla_flash",
    )(q, k, vt, gate)


def _diff_flash_kernel(posc_ref, lam_ref, gsub_ref, q_ref, k_ref, vt_ref, gate_ref, o_ref, qm_ref, p_ref,
                       *, tq, tk, lam_init):
    i = pl.program_id(1)
    n_maps = 2 * DIFF_HEADS
    half = tq // 2
    q = q_ref[...]
    lane_q = lax.broadcasted_iota(jnp.int32, (tq, DIFF_W), 1)
    for hm in range(n_maps):
        own = (lane_q >= hm * DIFF_DH) & (lane_q < (hm + 1) * DIFF_DH)
        qm_ref[hm * tq:(hm + 1) * tq, :] = jnp.where(own, q, jnp.zeros_like(q))
    slopes = [LOG2E * 2.0 ** (-8.0 * (h + 1) / DIFF_HEADS) for h in range(DIFF_HEADS)]
    dv = 2 * DIFF_DH
    va = DIFF_VA

    def tile(j, carry, q0, nq, mask, n_slab):
        nk = n_slab * tk
        rows = pl.ds(pl.multiple_of(j * nk, nk), nk)
        kt = k_ref[rows, :]
        pos_k = posc_ref[rows, :]
        bias = [jnp.tile(slope * pos_k, (1, nq // LANE)) for slope in slopes]
        st = []
        for hm in range(n_maps):
            cols = slice(hm * tq + q0, hm * tq + q0 + nq)
            s = _dot_nt(kt, qm_ref[cols, :]) + bias[hm // 2]
            if mask is not None:
                s = jnp.where(mask, s, -jnp.inf)
            m, alpha, p = _softmax_step(s, carry[hm][0])
            p_ref[0:nk, cols] = p
            st.append((m, alpha))
        vts = [vt_ref[j * n_slab + t] for t in range(n_slab)]
        new = []
        for hm in range(n_maps):
            cols = slice(hm * tq + q0, hm * tq + q0 + nq)
            vr = slice((hm // 2) * va, (hm // 2 + 1) * va)
            pv = sum(_dot(vts[t][vr], p_ref[t * tk:(t + 1) * tk, cols]) for t in range(n_slab))
            new.append((st[hm][0], st[hm][1] * carry[hm][1] + pv))
        return tuple(new)

    one = (jnp.full((1, tq), -jnp.inf, F32), jnp.zeros((va, tq), F32))
    carry = lax.fori_loop(0, i, lambda j, c: tile(j, c, 0, tq, None, tq // tk), (one,) * n_maps)
    carry = tile(2 * i, carry, 0, tq, _causal_t(tk, tq), 1)
    late = tile(2 * i + 1, tuple(tuple(a[:, half:] for a in c) for c in carry),
                half, half, _causal_t(tk, half), 1)
    accs = [jnp.concatenate([carry[hm][1][:, :half], late[hm][1]], axis=1) for hm in range(n_maps)]

    lam = lam_ref[...]
    lam_full = (jnp.exp(jnp.sum(lam[0:1] * lam[1:2], axis=-1, keepdims=True))
                - jnp.exp(jnp.sum(lam[2:3] * lam[3:4], axis=-1, keepdims=True)) + lam_init)
    heads = []
    for h in range(DIFF_HEADS):
        a0, a1 = accs[2 * h], accs[2 * h + 1]
        o = a0[0:dv] / a0[dv:dv + 1] - lam_full * (a1[0:dv] / a1[dv:dv + 1])
        ms = jnp.mean(o * o, axis=0, keepdims=True)
        heads.append(o * lax.rsqrt(ms + DIFF_SUBLN_EPS) * gsub_ref[...] * (1.0 - lam_init))
    o = jnp.concatenate(heads, axis=0).T
    o_ref[...] = (o * _silu(gate_ref[...].astype(F32))).astype(BF16)


def _diff_flash(d, dvt, gate, posc, lam, gsub_col, *, batch, seq, tq, lam_init):
    nq = seq // tq
    gate_blk = (RW_W + MLA_W) // DIFF_W
    n_maps = 2 * DIFF_HEADS
    tk = tq // 2
    nkv = seq // tk
    return pl.pallas_call(
        functools.partial(_diff_flash_kernel, tq=tq, tk=tk, lam_init=lam_init),
        grid=(batch, nq),
        in_specs=[
            pl.BlockSpec(posc.shape, lambda b, i: (0, 0)),
            pl.BlockSpec(lam.shape, lambda b, i: (0, 0)),
            pl.BlockSpec(gsub_col.shape, lambda b, i: (0, 0)),
            pl.BlockSpec((tq, DIFF_W), lambda b, i: (b * nq + i, 0)),
            pl.BlockSpec((seq, DIFF_W), lambda b, i: (b, 1)),
            pl.BlockSpec((nkv, DIFF_HEADS * DIFF_VA, tk), lambda b, i: (b, 0, 0)),
            pl.BlockSpec((tq, DIFF_W), lambda b, i: (b * nq + i, gate_blk)),
        ],
        out_specs=pl.BlockSpec((tq, DIFF_W), lambda b, i: (b * nq + i, 0)),
        out_shape=jax.ShapeDtypeStruct((batch * seq, DIFF_W), BF16),
        scratch_shapes=[pltpu.VMEM((n_maps * tq, DIFF_W), BF16), pltpu.VMEM((tq, n_maps * tq), BF16)],
        compiler_params=pltpu.CompilerParams(
            dimension_semantics=("parallel", "parallel"), vmem_limit_bytes=VMEM_LIMIT),
        name="diff_flash",
    )(posc, lam, gsub_col, d, d, dvt, gate)


def _rwkv_kernel(*refs, tb, has_vres, cpi, n_seq, n_shared):
    n_in = 3 if has_vres else 2
    n_io = len(refs) - n_seq * RW_N_SCRATCH
    live = []
    for e in range(n_seq):
        views = [r if n_in <= idx < n_in + n_shared else r.at[e] for idx, r in enumerate(refs[:n_io])]
        views += refs[n_io + e * RW_N_SCRATCH:n_io + (e + 1) * RW_N_SCRATCH]
        live.append(_rwkv_seq(*views, tb=tb, has_vres=has_vres, cpi=cpi, skew=e * RW_SKEW))
    while live:
        live = [g for g in live if next(g, "done") != "done"]


def _rwkv_seq(*refs, tb, has_vres, cpi, skew):
    for _ in range(skew):
        yield
    n_sc = RW_N_SCRATCH
    if has_vres:
        (p_ref, vf_ref, gate_ref, tri_ref, seg_ref, mu_ref, w0_ref, w2_ref, a0_ref, a2_ref,
         v0_ref, v2_ref, kk_ref, ka_ref, rk_ref, lnw_ref, lnb_ref, y_ref) = refs[:-n_sc]
    else:
        (p_ref, gate_ref, tri_ref, seg_ref, mu_ref, w0_ref, w2_ref, a0_ref, a2_ref,
         kk_ref, ka_ref, rk_ref, lnw_ref, lnb_ref, y_ref, vfo_ref) = refs[:-n_sc]
    (state_ref, carry_ref, at_ref, rt_ref, bt_ref, kt_ref, vv_ref, bw_ref, kw_ref, tot_ref, yy_ref,
     lhs_ref, xav_ref, ay_ref, tc_ref, cc_ref, wlt_ref) = refs[-n_sc:]

    @pl.when(pl.program_id(1) == 0)
    def _():
        state_ref[...] = jnp.zeros_like(state_ref)
        carry_ref[...] = jnp.zeros_like(carry_ref)

    seg = seg_ref[...]
    pair_lanes = [slice(pr * LANE, (pr + 1) * LANE) for pr in range(RW_HEADS // 2)]

    def segsum(x):
        xb = x.astype(BF16)
        return jnp.concatenate([_dot(xb[:, ls], seg) for ls in pair_lanes], axis=1)

    p = p_ref[...]
    prev = pltpu.roll(p, 1, 0)
    top = jnp.where(lax.broadcasted_iota(jnp.int32, (SUBLANE, 1), 0) == 0, carry_ref[...], prev[0:SUBLANE])
    prev = jnp.concatenate([top, prev[SUBLANE:]], axis=0)
    carry_ref[...] = p[tb - 1:tb, :]
    yield
    xs = p + (prev - p) * mu_ref[...]
    yield
    r = xs[:, 0:RW_W]
    k = xs[:, RW_W:2 * RW_W]
    v = xs[:, 2 * RW_W:3 * RW_W]
    hwa = xs[:, 3 * RW_W:3 * RW_W + LANE]
    lw = -math.exp(-0.5) * _sigmoid(w0_ref[...] + _dot(jnp.tanh(hwa).astype(BF16), w2_ref[...]))
    yield
    a = _sigmoid(a0_ref[...] + _dot(hwa.astype(BF16), a2_ref[...]))
    yield
    if has_vres:
        hv = xs[:, RW_SHIFT_BASE:RW_SHIFT_BASE + LANE]
        v = v + (vf_ref[...] - v) * _sigmoid(v0_ref[...] + _dot(hv.astype(BF16), v2_ref[...]))
    else:
        vfo_ref[...] = v
    yield
    kk = k * kk_ref[...]
    kk = kk * lax.rsqrt(jnp.maximum(segsum(kk * kk), 1e-24))
    yield
    k = k * (1.0 + (a - 1.0) * ka_ref[...])
    yield

    lw_hi, lw_lo = _split_hi_lo(lw)
    cum = _dot(tri_ref[...], lw_hi) + _dot(tri_ref[...], lw_lo)
    tot = jnp.concatenate([jnp.broadcast_to(cum[e - 1:e], (RW_CHUNK, RW_W))
                           for e in range(RW_CHUNK, tb + 1, RW_CHUNK)], axis=0)
    yield
    w_inv = jnp.exp(-cum)
    w_end = jnp.exp(tot - cum)
    yield
    kka = kk * a
    rt_ref[...] = (r * jnp.exp(cum)).astype(BF16)
    yield
    at_ref[...] = (-kk * jnp.exp(cum - lw)).astype(BF16)
    yield
    bt_ref[...] = (kka * w_inv).astype(BF16)
    kt_ref[...] = (k * w_inv).astype(BF16)
    yield
    vv_ref[...] = v.astype(BF16)
    bw_ref[...] = kka * w_end
    kw_ref[...] = k * w_end
    tot_ref[...] = tot
    yield

    c = RW_CHUNK
    n_pairs = RW_HEADS // 2
    lane_c = lax.broadcasted_iota(jnp.int32, (c, LANE), 1)
    even_c = lane_c < RW_N
    even_2c = lax.broadcasted_iota(jnp.int32, (2 * c, LANE), 1) < RW_N
    ri = lax.broadcasted_iota(jnp.int32, (2 * c, 2 * c), 0)
    ci = lax.broadcasted_iota(jnp.int32, (2 * c, 2 * c), 1)
    cm = jnp.where(ci >= c, ci - c, ci)
    keep = ((ri >= c) & (cm <= ri - c)) | ((ri < c) & (cm < ri))
    same_head = (ri < c) == (ci < c)
    eye2 = (lax.broadcasted_iota(jnp.int32, (c, LANE), 0)
            == jnp.where(lane_c >= c, lane_c - c, lane_c)).astype(F32)

    def bdiag(m):
        zero = jnp.zeros_like(m)
        return jnp.concatenate([jnp.where(even_c, m, zero), jnp.where(even_c, zero, m)], axis=0)

    def adiag(m):
        zero = jnp.zeros_like(m)
        return jnp.concatenate([jnp.where(even_c, zero, m), jnp.where(even_c, m, zero)], axis=0)

    def phase_a(i):
        probs = [(i * cpi + cc, pr) for cc in range(cpi) for pr in range(n_pairs)]
        rows = [pl.ds(pl.multiple_of(ic * c, c), c) for ic, _ in probs]
        rows2 = [pl.ds(pl.multiple_of(ic * 2 * c, 2 * c), 2 * c) for ic, _ in probs]
        ls = [slice(pr * LANE, (pr + 1) * LANE) for _, pr in probs]
        rng = range(len(probs))
        at = [at_ref[rows[j], ls[j]] for j in rng]
        rt = [rt_ref[rows[j], ls[j]] for j in rng]
        bt = [bt_ref[rows[j], ls[j]] for j in rng]
        kt = [kt_ref[rows[j], ls[j]] for j in rng]
        vh = [vv_ref[rows[j], ls[j]] for j in rng]
        xar = [jnp.concatenate([at[j], rt[j]], axis=0) for j in rng]
        zero_b = jnp.zeros((2 * c, LANE), BF16)
        aa_e = [jnp.where(keep, _dot_nt(jnp.where(even_2c, xar[j], zero_b),
                                        jnp.concatenate([bt[j], kt[j]], axis=0)), 0.0) for j in rng]
        yield
        aa_o = [jnp.where(keep, _dot_nt(jnp.where(even_2c, zero_b, xar[j]),
                                        jnp.concatenate([kt[j], bt[j]], axis=0)), 0.0) for j in rng]
        yield
        nn = [jnp.where(even_c, aa_e[j][0:c], aa_o[j][0:c]) for j in rng]
        aak = [jnp.where(even_c, aa_o[j][0:c], aa_e[j][0:c]) for j in rng]
        for j in rng:
            pr = probs[j][1]
            ay_ref[rows[j], 2 * pr * LANE:(2 * pr + 1) * LANE] = jnp.where(
                even_c, aa_e[j][c:2 * c], aa_o[j][c:2 * c]).astype(BF16)
            ay_ref[rows[j], (2 * pr + 1) * LANE:(2 * pr + 2) * LANE] = jnp.where(
                even_c, aa_o[j][c:2 * c], aa_e[j][c:2 * c]).astype(BF16)
        yield
        av = [_dot(aak[j].astype(BF16), adiag(vh[j])) for j in rng]
        yield
        x = [eye2 + nn[j] for j in rng]
        pw = [nn[j].astype(BF16) for j in rng]
        pw = [_dot(pw[j], bdiag(pw[j])).astype(BF16) for j in rng]
        yield
        n_sq = int(math.log2(c)) - 1
        for it in range(n_sq):
            last = it == n_sq - 1
            lhs = [x[j].astype(BF16) if last else jnp.concatenate([x[j].astype(BF16), pw[j]], axis=0) for j in rng]
            out = [_dot(lhs[j], bdiag(pw[j])) for j in rng]
            x = [x[j] + out[j][0:c] for j in rng]
            if not last:
                pw = [out[j][c:2 * c].astype(BF16) for j in rng]
            yield
        fin = [_dot(x[j].astype(BF16), jnp.concatenate([bdiag(at[j]), bdiag(av[j].astype(BF16))], axis=1))
               for j in rng]
        yield
        ahat = [fin[j][:, 0:LANE].astype(BF16) for j in rng]
        xav = [fin[j][:, LANE:2 * LANE] for j in rng]
        zt = [jnp.concatenate([bw_ref[rows[j], ls[j]], kw_ref[rows[j], ls[j]]], axis=0).T.astype(BF16) for j in rng]
        zero_c = jnp.zeros((c, LANE), BF16)
        tcb = [_dot(zt[j], jnp.concatenate([ahat[j], zero_c], axis=0)) for j in rng]
        yield
        ccv = [_dot(zt[j], jnp.concatenate([xav[j].astype(BF16), vh[j]], axis=0)) for j in rng]
        yield
        for j in rng:
            lhs_ref[rows2[j], ls[j]] = jnp.concatenate([ahat[j], rt[j]], axis=0)
            xav_ref[rows[j], ls[j]] = xav[j]
            tc_ref[rows2[j], ls[j]] = jnp.where(same_head, tcb[j], 0.0).astype(BF16)
            cc_ref[rows2[j], ls[j]] = jnp.where(same_head, ccv[j], 0.0)
            tot_row = tot_ref[pl.ds(probs[j][0] * c, 1), ls[j]]
            wlt_ref[rows2[j], ls[j]] = jnp.exp(jnp.broadcast_to(tot_row, (2 * c, LANE)).T)
        yield

    for i in range(tb // (c * cpi)):
        yield from phase_a(i)

    ls = [slice(pr * LANE, (pr + 1) * LANE) for pr in range(n_pairs)]
    rng = range(n_pairs)
    mt = [state_ref[pr] for pr in rng]
    for ic in range(tb // c):
        rows = slice(ic * c, (ic + 1) * c)
        rows2 = slice(ic * 2 * c, (ic + 1) * 2 * c)
        mtb = [mt[pr].astype(BF16) for pr in rng]
        nxt = [_dot(tc_ref[rows2, ls[pr]], mtb[pr]) for pr in rng]
        mt = [mt[pr] * wlt_ref[rows2, ls[pr]] + nxt[pr] + cc_ref[rows2, ls[pr]] for pr in rng]
        out = [_dot(lhs_ref[rows2, ls[pr]], mtb[pr]) for pr in rng]
        ub = [(out[pr][0:c] + xav_ref[rows, ls[pr]]).astype(BF16) for pr in rng]
        vh = [vv_ref[rows, ls[pr]] for pr in rng]
        for pr in rng:
            uv = jnp.concatenate([bdiag(ub[pr]), adiag(vh[pr])], axis=0)
            yy_ref[rows, ls[pr]] = out[pr][c:2 * c] + _dot(ay_ref[rows, 2 * pr * LANE:(2 * pr + 2) * LANE], uv)
        yield
    for pr in rng:
        state_ref[pr] = mt[pr]

    y = yy_ref[...]
    inv_n = 1.0 / RW_N
    dlt = y - segsum(y) * inv_n
    yield
    var = segsum(dlt * dlt) * inv_n
    yield
    y = dlt * lax.rsqrt(var + RW_GN_EPS) * lnw_ref[...] + lnb_ref[...]
    yield
    y = y + segsum(r * k * rk_ref[...]) * v
    y_ref[...] = (y * _silu(gate_ref[...].astype(F32))).astype(BF16)


def _rwkv(p, vfirst, gate, consts, params, *, batch, seq, tb, cpi, n_seq):
    has_vres = vfirst is not None
    cw = p.shape[1]
    per_seq = lambda a: a.reshape(batch // n_seq, n_seq, seq, a.shape[-1])
    blk = lambda w: pl.BlockSpec((None, n_seq, tb, w), lambda b, i: (b, 0, i, 0))
    full = lambda a: pl.BlockSpec(a.shape, lambda b, i: (0,) * a.ndim)
    shared = list(consts) + list(params)
    ins = [per_seq(p)] + ([per_seq(vfirst)] if has_vres else []) + [per_seq(gate)] + shared
    in_specs = [blk(cw)] + ([blk(RW_W)] if has_vres else []) + [blk(RW_W)] + [full(a) for a in shared]
    out_specs = [blk(RW_W)]
    out_shape = [jax.ShapeDtypeStruct((batch // n_seq, n_seq, seq, RW_W), BF16)]
    if not has_vres:
        out_specs.append(blk(RW_W))
        out_shape.append(jax.ShapeDtypeStruct((batch // n_seq, n_seq, seq, RW_W), F32))
    vm = lambda rows, cols, dt: pltpu.VMEM((rows, cols), dt)
    scratch = [pltpu.VMEM((RW_HEADS // 2, LANE, LANE), F32), vm(1, cw, F32),
               vm(tb, RW_W, BF16), vm(tb, RW_W, BF16), vm(tb, RW_W, BF16), vm(tb, RW_W, BF16), vm(tb, RW_W, BF16),
               vm(tb, RW_W, F32), vm(tb, RW_W, F32), vm(tb, RW_W, F32), vm(tb, RW_W, F32),
               vm(2 * tb, RW_W, BF16), vm(tb, RW_W, F32), vm(tb, 2 * RW_W, BF16),
               vm(2 * tb, RW_W, BF16), vm(2 * tb, RW_W, F32), vm(2 * tb, RW_W, F32)]
    assert len(scratch) == RW_N_SCRATCH
    outs = pl.pallas_call(
        functools.partial(_rwkv_kernel, tb=tb, has_vres=has_vres, cpi=cpi, n_seq=n_seq, n_shared=len(shared)),
        grid=(batch // n_seq, seq // tb),
        in_specs=in_specs, out_specs=out_specs, out_shape=out_shape,
        scratch_shapes=scratch * n_seq,
        compiler_params=pltpu.CompilerParams(
            dimension_semantics=("parallel", "arbitrary"), vmem_limit_bytes=VMEM_LIMIT),
        name="rwkv7",
    )(*ins)
    outs = [o.reshape(batch * seq, RW_W) for o in outs]
    return outs if not has_vres else (outs[0], None)


def _outproj_kernel(x_ref, om_ref, od_ref, orw_ref, w_ref, fg_ref, o_ref, *, final):
    mixed = jnp.concatenate([om_ref[...], od_ref[...], orw_ref[...]], axis=1)
    y = x_ref[...] + _dot(mixed, w_ref[...])
    if final:
        y = _rms(y, fg_ref[...], NORM_EPS)
    o_ref[...] = y


def _outproj(x2, om, od, orw, w, fg, *, tm, final):
    t = x2.shape[0]
    row = lambda i: (i, 0)
    const = lambda i: (0, 0)
    return pl.pallas_call(
        functools.partial(_outproj_kernel, final=final),
        grid=(t // tm,),
        in_specs=[pl.BlockSpec((tm, D_MODEL), row), pl.BlockSpec((tm, MLA_W), row),
                  pl.BlockSpec((tm, DIFF_W), row), pl.BlockSpec((tm, RW_W), row),
                  pl.BlockSpec(w.shape, const), pl.BlockSpec(fg.shape, const)],
        out_specs=pl.BlockSpec((tm, D_MODEL), row),
        out_shape=jax.ShapeDtypeStruct((t, D_MODEL), F32),
        compiler_params=pltpu.CompilerParams(
            dimension_semantics=("parallel",), vmem_limit_bytes=VMEM_LIMIT),
        name="outproj",
    )(x2, om, od, orw, w, fg)


def _pack_inproj(w_in, w_vres):
    d = w_in.shape[0]
    o = 0
    cq = w_in[:, o:o + MLA_Q_RANK]; o += MLA_Q_RANK
    ckv = w_in[:, o:o + MLA_KV_RANK]; o += MLA_KV_RANK
    kpe = w_in[:, o:o + MLA_ROPE]; o += MLA_ROPE
    dqk = w_in[:, o:o + DIFF_G]; o += DIFF_G
    dv = w_in[:, o:o + DIFF_W]; o += DIFF_W
    gate = w_in[:, o:o + D_MIX]; o += D_MIX
    rw = w_in[:, o:]
    half = MLA_ROPE // 2
    z = lambda n: jnp.zeros((d, n), w_in.dtype)
    kp = jnp.concatenate([kpe, kpe[:, half:], kpe[:, :half], z(LANE - 2 * MLA_ROPE)], axis=1)
    g_mla, g_diff, g_rw = gate[:, :MLA_W], gate[:, MLA_W:MLA_W + DIFF_W], gate[:, MLA_W + DIFF_W:]
    cols = [cq, ckv, kp, dqk, g_rw, g_mla, g_diff, rw]
    if w_vres is not None:
        cols += [w_vres, z(LANE - RW_MV_RANK)]
    return jnp.concatenate(cols, axis=1).astype(BF16), dv.T.astype(BF16)


def _pack_mla(w_uq, w_ukv):
    rq, rkv = w_uq.shape[0], w_ukv.shape[0]
    q3 = w_uq.reshape(rq, MLA_HEADS, MLA_QK)
    zq = lambda n: jnp.zeros((rq, MLA_HEADS, n), w_uq.dtype)
    wq = jnp.concatenate([q3, zq(HP - MLA_QK)], axis=2)
    kv3 = w_ukv.reshape(rkv, MLA_HEADS, MLA_NOPE + MLA_V)
    wk = jnp.concatenate([kv3[:, :, :MLA_NOPE], jnp.zeros((rkv, MLA_HEADS, HP - MLA_NOPE), w_ukv.dtype)], axis=2)
    wv = kv3[:, :, MLA_NOPE:]
    flat = lambda a: a.reshape(a.shape[0], -1).astype(BF16)
    return flat(wq), flat(wk), flat(wv).T


def _pad_rows(w, top, total):
    return jnp.concatenate([jnp.zeros((top, w.shape[1]), w.dtype), w,
                            jnp.zeros((total - top - w.shape[0], w.shape[1]), w.dtype)], axis=0).astype(BF16)


def _rwkv_consts(tb):
    t = jnp.arange(tb)
    same = (t[:, None] // RW_CHUNK) == (t[None, :] // RW_CHUNK)
    tri = (same & (t[None, :] <= t[:, None])).astype(BF16)
    hl = jnp.arange(LANE) // RW_N
    seg = (hl[:, None] == hl[None, :]).astype(BF16)
    return tri, seg


def kernel(x, positions, pre_g, w_in, w_in_vres, w_out, mla_gq, mla_gkv, mla_wuq, mla_wukv, diff_lam, diff_gsub, rw_mu, rw_mu_vres, rw_w0, rw_w2, rw_a0, rw_a2, rw_v0, rw_v2, rw_kk, rw_ka, rw_rk, rw_lnw, rw_lnb, final_g):
    batch, seq, _ = x.shape
    depth = pre_g.shape[0]
    tm = min(512, seq)
    tq = min(512, seq)
    tkv = tq // 2
    tb = min(256, seq)
    assert seq % tm == 0 and seq % tq == 0 and tm % tkv == 0
    assert seq % tb == 0 and tb % RW_CHUNK == 0

    x2 = x.reshape(batch * seq, D_MODEL)
    tab = _rope_table(positions)
    pos_f = positions.astype(F32)
    posc = jnp.broadcast_to(pos_f.reshape(seq, 1), (seq, LANE))
    consts = _rwkv_consts(tb)
    r1 = lambda a: a.reshape(1, -1).astype(F32)

    vfirst = None
    for layer in range(depth):
        vres = layer > 0
        w, wdvt = _pack_inproj(w_in[layer], w_in_vres[layer - 1] if vres else None)
        wq, wk, wvt = _pack_mla(mla_wuq[layer], mla_wukv[layer])
        q, k, vt, d, dvt, gate, rw = _inproj(x2, r1(pre_g[layer]), w, tab, r1(mla_gq[layer]), r1(mla_gkv[layer]),
                                             wq, wk, wvt, wdvt,
                                             seq=seq, tm=tm, tkv=tkv)
        o_mla = _mla_flash(q, k, vt, gate, batch=batch, seq=seq, tq=tq, n_h=MLA_HEADS)
        lam_init = 0.8 - 0.6 * math.exp(-0.3 * (layer + 1))
        gsub_col = diff_gsub[layer].reshape(-1, 1).astype(F32)
        o_diff = _diff_flash(d, dvt, gate, posc, diff_lam[layer].astype(F32), gsub_col,
                             batch=batch, seq=seq, tq=tq, lam_init=lam_init)
        mu = rw_mu[layer]
        if vres:
            mu = jnp.concatenate([mu, rw_mu_vres[layer - 1], jnp.zeros((LANE - RW_MV_RANK,), mu.dtype)])
        params = [r1(mu), r1(rw_w0[layer]), _pad_rows(rw_w2[layer], 0, LANE), r1(rw_a0[layer]),
                  _pad_rows(rw_a2[layer], RW_DECAY_RANK, LANE)]
        if vres:
            params += [r1(rw_v0[layer - 1]), _pad_rows(rw_v2[layer - 1], 0, LANE)]
        params += [r1(rw_kk[layer]), r1(rw_ka[layer]), r1(rw_rk[layer]), r1(rw_lnw[layer]), r1(rw_lnb[layer])]
        o_rw, vf = _rwkv(rw, vfirst, gate, consts, params, batch=batch, seq=seq, tb=tb, cpi=4,
                          n_seq=2 if batch % 2 == 0 else 1)
        if not vres:
            vfirst = vf
        x2 = _outproj(x2, o_mla, o_diff, o_rw, w_out[layer].astype(BF16), r1(final_g),
                      tm=min(2 * tm, seq), final=(layer == depth - 1))
    return x2.reshape(batch, seq, D_MODEL)
```

```python
import functools
import math

import jax
import jax.numpy as jnp
from jax import lax
from jax.experimental import pallas as pl
from jax.experimental.pallas import tpu as pltpu

F32 = jnp.float32
BF16 = jnp.bfloat16

D_MODEL = 1024
D_MIX = 1024
NORM_EPS = 1e-6
MLA_HEADS, MLA_NOPE, MLA_ROPE, MLA_V = 6, 64, 32, 64
MLA_Q_RANK, MLA_KV_RANK = 256, 128
MLA_QK = MLA_NOPE + MLA_ROPE
ROPE_THETA = 10000.0
MLA_W = MLA_HEADS * MLA_V
DIFF_HEADS, DIFF_DH = 4, 32
DIFF_W = DIFF_HEADS * 2 * DIFF_DH
DIFF_SUBLN_EPS = 1e-5
RW_HEADS, RW_N = 6, 64
RW_W = RW_HEADS * RW_N
RW_DECAY_RANK, RW_AAA_RANK, RW_MV_RANK = 64, 64, 32
RW_GN_EPS = 64e-5
RW_SHIFT_BASE = 3 * RW_W + RW_DECAY_RANK + RW_AAA_RANK

LANE = 128
SUBLANE = 8
BF16_ROWS = 16
LOG2E = math.log2(math.e)
HP = LANE
MLA_VA = MLA_V + BF16_ROWS
DIFF_VA = 2 * DIFF_DH + BF16_ROWS
RW_CHUNK = 64
RW_N_SCRATCH = 17
RW_SKEW = 14

MLA_G = MLA_Q_RANK + MLA_KV_RANK + LANE
DIFF_G = 2 * DIFF_W
GATE_G = D_MIX
RW_G0 = RW_SHIFT_BASE
RW_G1 = RW_SHIFT_BASE + LANE
OFF_DIFF = MLA_G
OFF_GATE = OFF_DIFF + DIFF_G
OFF_RW = OFF_GATE + GATE_G

VMEM_LIMIT = 52 * 1024 * 1024

_NT = (((1,), (1,)), ((), ()))
_TN = (((0,), (0,)), ((), ()))


def _dot(a, b):
    return jnp.dot(a, b, preferred_element_type=F32)


def _dot_nt(a, b):
    return lax.dot_general(a, b, _NT, preferred_element_type=F32)


def _split_hi_lo(x):
    hi = x.astype(BF16)
    return hi, (x - hi.astype(F32)).astype(BF16)


def _sigmoid(z):
    return 1.0 / (1.0 + jnp.exp(-z))


def _silu(g):
    return g * _sigmoid(g)


def _rms(x, g, eps):
    return x * lax.rsqrt(jnp.mean(x * x, axis=-1, keepdims=True) + eps) * g


def _rope_table_kernel(pos_ref, tab_ref):
    pos = pos_ref[...]
    lane = lax.broadcasted_iota(jnp.int32, (1, LANE), 1)
    half = MLA_ROPE // 2
    in_rope = (lane >= MLA_NOPE) & (lane < MLA_QK)
    first = lane < MLA_NOPE + half
    idx = jnp.where(first, lane - MLA_NOPE, lane - MLA_NOPE - half).astype(F32)
    inv = jnp.exp(idx * (-math.log(ROPE_THETA) / half))
    ang = pos * inv
    cos = jnp.where(in_rope, jnp.cos(ang), jnp.where(lane < MLA_NOPE, 1.0, 0.0))
    sin = jnp.sin(ang)
    sin = jnp.where(in_rope, jnp.where(first, -sin, sin), 0.0)
    scale = MLA_QK ** -0.5 * LOG2E
    tab_ref[:, 0 * LANE:1 * LANE] = cos * scale
    tab_ref[:, 1 * LANE:2 * LANE] = jnp.where(first, sin, 0.0) * scale
    tab_ref[:, 2 * LANE:3 * LANE] = jnp.where(first, 0.0, sin) * scale
    tab_ref[:, 3 * LANE:4 * LANE] = cos
    tab_ref[:, 4 * LANE:5 * LANE] = sin


def _rope_table(positions):
    s = positions.shape[0]
    return pl.pallas_call(
        _rope_table_kernel,
        out_shape=jax.ShapeDtypeStruct((s, 5 * LANE), F32),
        name="rope_table",
    )(positions.astype(F32).reshape(s, 1))


def _inproj_kernel(x_ref, g_ref, w_ref, tab_ref, gq_ref, gkv_ref, wq_ref, wk_ref, wvt_ref, wdvt_ref,
                   q_ref, k_ref, vt_ref, d_ref, dvt_ref, gate_ref, rw_ref, *, tkv):
    o_kpe = MLA_Q_RANK + MLA_KV_RANK
    half = MLA_ROPE // 2
    ones = jnp.ones((BF16_ROWS, tkv), BF16)
    for t in range(vt_ref.shape[0]):
        rows = slice(t * tkv, (t + 1) * tkv)
        h = _rms(x_ref[rows, :], g_ref[...], NORM_EPS).astype(BF16)

        mla = _dot(h, w_ref[:, 0:MLA_G])
        cqn = _rms(mla[:, 0:MLA_Q_RANK], gq_ref[...], NORM_EPS).astype(BF16)
        ckvn = _rms(mla[:, MLA_Q_RANK:o_kpe], gkv_ref[...], NORM_EPS).astype(BF16)
        tab = tab_ref[rows, :]
        cos_q, sin_q1, sin_q2 = tab[:, 0:LANE], tab[:, LANE:2 * LANE], tab[:, 2 * LANE:3 * LANE]
        cos_k, sin_k = tab[:, 3 * LANE:4 * LANE], tab[:, 4 * LANE:5 * LANE]
        kg = mla[:, o_kpe:o_kpe + LANE]
        kpe = pltpu.roll(kg, MLA_NOPE, 1) * cos_k + pltpu.roll(kg, MLA_NOPE - MLA_ROPE, 1) * sin_k
        qa = _dot(cqn, wq_ref[...])
        kn = _dot(ckvn, wk_ref[...])
        for hd in range(MLA_HEADS):
            sl = slice(hd * HP, (hd + 1) * HP)
            qh = qa[:, sl]
            q_ref[rows, sl] = (qh * cos_q + pltpu.roll(qh, HP - half, 1) * sin_q1
                               + pltpu.roll(qh, half, 1) * sin_q2).astype(BF16)
            k_ref[rows, sl] = (kn[:, sl] + kpe).astype(BF16)
        for src, dst, heads, width in ((_dot_nt(wvt_ref[...], ckvn), vt_ref, MLA_HEADS, MLA_V),
                                       (_dot_nt(wdvt_ref[...], h), dvt_ref, DIFF_HEADS, 2 * DIFF_DH)):
            src = src.astype(BF16)
            for hd in range(heads):
                r0 = hd * (width + BF16_ROWS)
                dst[t, r0:r0 + width, :] = src[hd * width:(hd + 1) * width, :]
                dst[t, r0 + width:r0 + width + BF16_ROWS, :] = ones

        dd = _dot(h, w_ref[:, OFF_DIFF:OFF_GATE])
        d_ref[rows, 0:DIFF_W] = (dd[:, 0:DIFF_W] * (DIFF_DH ** -0.5 * LOG2E)).astype(BF16)
        d_ref[rows, DIFF_W:DIFF_G] = dd[:, DIFF_W:DIFF_G].astype(BF16)
        gate_ref[rows, :] = _dot(h, w_ref[:, OFF_GATE:OFF_RW]).astype(BF16)
        rw_ref[rows, :] = _dot(h, w_ref[:, OFF_RW:])


def _inproj(x2, g, w, tab, gq, gkv, wq, wk, wvt, wdvt, *, seq, tm, tkv):
    t = x2.shape[0]
    ctot = w.shape[1]
    rw_g = ctot - OFF_RW
    n_seq_tiles = seq // tm
    kv_per_tile = tm // tkv
    row = lambda i: (i, 0)
    slab = lambda i: (i, 0, 0)
    const = lambda i: (0, 0)
    full = lambda a: pl.BlockSpec(a.shape, const, pipeline_mode=pl.Buffered(1))
    return pl.pallas_call(
        functools.partial(_inproj_kernel, tkv=tkv),
        grid=(t // tm,),
        in_specs=[
            pl.BlockSpec((tm, D_MODEL), row),
            full(g), full(w),
            pl.BlockSpec((tm, tab.shape[1]), lambda i: (i % n_seq_tiles, 0)),
            full(gq), full(gkv), full(wq), full(wk), full(wvt), full(wdvt),
        ],
        out_specs=[
            pl.BlockSpec((tm, MLA_HEADS * HP), row),
            pl.BlockSpec((tm, MLA_HEADS * HP), row),
            pl.BlockSpec((kv_per_tile, MLA_HEADS * MLA_VA, tkv), slab),
            pl.BlockSpec((tm, DIFF_G), row),
            pl.BlockSpec((kv_per_tile, DIFF_HEADS * DIFF_VA, tkv), slab),
            pl.BlockSpec((tm, GATE_G), row),
            pl.BlockSpec((tm, rw_g), row),
        ],
        out_shape=[
            jax.ShapeDtypeStruct((t, MLA_HEADS * HP), BF16),
            jax.ShapeDtypeStruct((t, MLA_HEADS * HP), BF16),
            jax.ShapeDtypeStruct((t // tkv, MLA_HEADS * MLA_VA, tkv), BF16),
            jax.ShapeDtypeStruct((t, DIFF_G), BF16),
            jax.ShapeDtypeStruct((t // tkv, DIFF_HEADS * DIFF_VA, tkv), BF16),
            jax.ShapeDtypeStruct((t, GATE_G), BF16),
            jax.ShapeDtypeStruct((t, rw_g), F32),
        ],
        compiler_params=pltpu.CompilerParams(
            dimension_semantics=("parallel",), vmem_limit_bytes=VMEM_LIMIT),
        name="inproj",
    )(x2, g, w, tab, gq, gkv, wq, wk, wvt, wdvt)


def _softmax_step(s, m):
    m_new = jnp.maximum(m, jnp.max(s, axis=0, keepdims=True))
    return m_new, jnp.exp2(m - m_new), jnp.exp2(s - m_new).astype(BF16)


def _causal_t(n_keys, n_queries):
    return (lax.broadcasted_iota(jnp.int32, (n_keys, n_queries), 0)
            <= lax.broadcasted_iota(jnp.int32, (n_keys, n_queries), 1))


def _run_interleaved(gens):
    out = [None] * len(gens)
    live = list(enumerate(gens))
    while live:
        still = []
        for idx, g in live:
            try:
                next(g)
                still.append((idx, g))
            except StopIteration as stop:
                out[idx] = stop.value
        live = still
    return out


def _attn_kernel(q_ref, k_ref, vt_ref, gm_ref, posc_ref, lam_ref, gsub_ref, dq_ref, dk_ref, dvt_ref, gd_ref,
                 om_ref, od_ref, qm_ref, p_ref, *, tq, tk, lam_init):
    i = pl.program_id(1)
    half = tq // 2
    n_slab_main = tq // tk

    n_h = MLA_HEADS
    hs = [slice(hh * HP, (hh + 1) * HP) for hh in range(n_h)]
    ps = [slice(hh * MLA_VA, (hh + 1) * MLA_VA) for hh in range(n_h)]

    def mla_tile(j, carry, qs, mask, n_slab):
        rows = pl.ds(pl.multiple_of(j * (n_slab * tk), n_slab * tk), n_slab * tk)
        s = [_dot_nt(k_ref[rows, hs[hh]], qs[hh]) for hh in range(n_h)]
        if mask is not None:
            s = [jnp.where(mask, x, -jnp.inf) for x in s]
        yield
        st = []
        for hh in range(n_h):
            st.append(_softmax_step(s[hh], carry[hh][0]))
            yield
        vt = [vt_ref[j * n_slab + t] for t in range(n_slab)]
        pv = [sum(_dot(vt[t][ps[hh]], st[hh][2][t * tk:(t + 1) * tk]) for t in range(n_slab))
              for hh in range(n_h)]
        yield
        return tuple((st[hh][0], st[hh][1] * carry[hh][1] + pv[hh]) for hh in range(n_h))

    qs_full = [q_ref[:, hs[hh]] for hh in range(n_h)]
    qs_late = [q[half:] for q in qs_full]
    mla_one = (jnp.full((1, tq), -jnp.inf, F32), jnp.zeros((MLA_VA, tq), F32))

    n_maps = 2 * DIFF_HEADS
    dq = dq_ref[...]
    lane_q = lax.broadcasted_iota(jnp.int32, (tq, DIFF_W), 1)
    for hm in range(n_maps):
        own = (lane_q >= hm * DIFF_DH) & (lane_q < (hm + 1) * DIFF_DH)
        qm_ref[hm * tq:(hm + 1) * tq, :] = jnp.where(own, dq, jnp.zeros_like(dq))
    slopes = [LOG2E * 2.0 ** (-8.0 * (h + 1) / DIFF_HEADS) for h in range(DIFF_HEADS)]
    dv = 2 * DIFF_DH
    va = DIFF_VA

    def diff_tile(j, carry, q0, nq, mask, n_slab):
        nk = n_slab * tk
        rows = pl.ds(pl.multiple_of(j * nk, nk), nk)
        kt = dk_ref[rows, :]
        pos_k = posc_ref[rows, :]
        bias = [jnp.tile(slope * pos_k, (1, nq // LANE)) for slope in slopes]
        st = []
        for hm in range(n_maps):
            cols = slice(hm * tq + q0, hm * tq + q0 + nq)
            s = _dot_nt(kt, qm_ref[cols, :]) + bias[hm // 2]
            if mask is not None:
                s = jnp.where(mask, s, -jnp.inf)
            m, alpha, p = _softmax_step(s, carry[hm][0])
            p_ref[0:nk, cols] = p
            st.append((m, alpha))
            yield
        vts = [dvt_ref[j * n_slab + t] for t in range(n_slab)]
        new = []
        for hm in range(n_maps):
            cols = slice(hm * tq + q0, hm * tq + q0 + nq)
            vr = slice((hm // 2) * va, (hm // 2 + 1) * va)
            pv = sum(_dot(vts[t][vr], p_ref[t * tk:(t + 1) * tk, cols]) for t in range(n_slab))
            new.append((st[hm][0], st[hm][1] * carry[hm][1] + pv))
        yield
        return tuple(new)

    diff_one = (jnp.full((1, tq), -jnp.inf, F32), jnp.zeros((va, tq), F32))

    def main(j, c):
        return tuple(_run_interleaved([mla_tile(j, c[0], qs_full, None, n_slab_main),
                                       diff_tile(j, c[1], 0, tq, None, n_slab_main)]))

    cm, cd = lax.fori_loop(0, i, main, ((mla_one,) * n_h, (diff_one,) * n_maps))
    cm, cd = _run_interleaved([mla_tile(2 * i, cm, qs_full, _causal_t(tk, tq), 1),
                               diff_tile(2 * i, cd, 0, tq, _causal_t(tk, tq), 1)])
    late = lambda c: tuple(tuple(a[:, half:] for a in x) for x in c)
    lm, ld = _run_interleaved([mla_tile(2 * i + 1, late(cm), qs_late, _causal_t(tk, half), 1),
                               diff_tile(2 * i + 1, late(cd), half, half, _causal_t(tk, half), 1)])
    merge = lambda c, l: [jnp.concatenate([c[n][1][:, :half], l[n][1]], axis=1) for n in range(len(c))]

    accs = merge(cm, lm)
    o = jnp.concatenate([a[0:MLA_V] / a[MLA_V:MLA_V + 1] for a in accs], axis=0).T
    om_ref[...] = (o * _silu(gm_ref[...].astype(F32))).astype(BF16)

    accs = merge(cd, ld)
    lam = lam_ref[...]
    lam_full = (jnp.exp(jnp.sum(lam[0:1] * lam[1:2], axis=-1, keepdims=True))
                - jnp.exp(jnp.sum(lam[2:3] * lam[3:4], axis=-1, keepdims=True)) + lam_init)
    heads = []
    for h in range(DIFF_HEADS):
        a0, a1 = accs[2 * h], accs[2 * h + 1]
        o = a0[0:dv] / a0[dv:dv + 1] - lam_full * (a1[0:dv] / a1[dv:dv + 1])
        ms = jnp.mean(o * o, axis=0, keepdims=True)
        heads.append(o * lax.rsqrt(ms + DIFF_SUBLN_EPS) * gsub_ref[...] * (1.0 - lam_init))
    o = jnp.concatenate(heads, axis=0).T
    od_ref[...] = (o * _silu(gd_ref[...].astype(F32))).astype(BF16)


def _attn(q, k, vt, d, dvt, gate, posc, lam, gsub_col, *, batch, seq, tq, lam_init):
    nq = seq // tq
    tk = tq // 2
    nkv = seq // tk
    n_maps = 2 * DIFF_HEADS
    row = lambda b, i: (b * nq + i, 0)
    const = lambda a: pl.BlockSpec(a.shape, lambda b, i: (0, 0))
    return pl.pallas_call(
        functools.partial(_attn_kernel, tq=tq, tk=tk, lam_init=lam_init),
        grid=(batch, nq),
        in_specs=[
            pl.BlockSpec((tq, MLA_HEADS * HP), row),
            pl.BlockSpec((seq, MLA_HEADS * HP), lambda b, i: (b, 0)),
            pl.BlockSpec((nkv, MLA_HEADS * MLA_VA, tk), lambda b, i: (b, 0, 0)),
            pl.BlockSpec((tq, MLA_W), lambda b, i: (b * nq + i, RW_W // MLA_W)),
            const(posc), const(lam), const(gsub_col),
            pl.BlockSpec((tq, DIFF_W), row),
            pl.BlockSpec((seq, DIFF_W), lambda b, i: (b, 1)),
            pl.BlockSpec((nkv, DIFF_HEADS * DIFF_VA, tk), lambda b, i: (b, 0, 0)),
            pl.BlockSpec((tq, DIFF_W), lambda b, i: (b * nq + i, (RW_W + MLA_W) // DIFF_W)),
        ],
        out_specs=[pl.BlockSpec((tq, MLA_W), row), pl.BlockSpec((tq, DIFF_W), row)],
        out_shape=[jax.ShapeDtypeStruct((batch * seq, MLA_W), BF16),
                   jax.ShapeDtypeStruct((batch * seq, DIFF_W), BF16)],
        scratch_shapes=[pltpu.VMEM((n_maps * tq, DIFF_W), BF16), pltpu.VMEM((tq, n_maps * tq), BF16)],
        compiler_params=pltpu.CompilerParams(
            dimension_semantics=("parallel", "parallel"), vmem_limit_bytes=VMEM_LIMIT),
        name="attn_flash",
    )(q, k, vt, gate, posc, lam, gsub_col, d, d, dvt, gate)


def _rwkv_kernel(*refs, tb, has_vres, cpi, n_seq, n_shared):
    n_in = 3 if has_vres else 2
    n_io = len(refs) - n_seq * RW_N_SCRATCH
    live = []
    for e in range(n_seq):
        views = [r if n_in <= idx < n_in + n_shared else r.at[e] for idx, r in enumerate(refs[:n_io])]
        views += refs[n_io + e * RW_N_SCRATCH:n_io + (e + 1) * RW_N_SCRATCH]
        live.append(_rwkv_seq(*views, tb=tb, has_vres=has_vres, cpi=cpi, skew=e * RW_SKEW))
    while live:
        live = [g for g in live if next(g, "done") != "done"]


def _rwkv_seq(*refs, tb, has_vres, cpi, skew):
    for _ in range(skew):
        yield
    n_sc = RW_N_SCRATCH
    if has_vres:
        (p_ref, vf_ref, gate_ref, tri_ref, seg_ref, mu_ref, w0_ref, w2_ref, a0_ref, a2_ref,
         v0_ref, v2_ref, kk_ref, ka_ref, rk_ref, lnw_ref, lnb_ref, y_ref) = refs[:-n_sc]
    else:
        (p_ref, gate_ref, tri_ref, seg_ref, mu_ref, w0_ref, w2_ref, a0_ref, a2_ref,
         kk_ref, ka_ref, rk_ref, lnw_ref, lnb_ref, y_ref, vfo_ref) = refs[:-n_sc]
    (state_ref, carry_ref, at_ref, rt_ref, bt_ref, kt_ref, vv_ref, bw_ref, kw_ref, tot_ref, yy_ref,
     lhs_ref, xav_ref, ay_ref, tc_ref, cc_ref, wlt_ref) = refs[-n_sc:]

    @pl.when(pl.program_id(1) == 0)
    def _():
        state_ref[...] = jnp.zeros_like(state_ref)
        carry_ref[...] = jnp.zeros_like(carry_ref)

    seg = seg_ref[...]
    pair_lanes = [slice(pr * LANE, (pr + 1) * LANE) for pr in range(RW_HEADS // 2)]

    def segsum(x):
        xb = x.astype(BF16)
        return jnp.concatenate([_dot(xb[:, ls], seg) for ls in pair_lanes], axis=1)

    p = p_ref[...]
    prev = pltpu.roll(p, 1, 0)
    top = jnp.where(lax.broadcasted_iota(jnp.int32, (SUBLANE, 1), 0) == 0, carry_ref[...], prev[0:SUBLANE])
    prev = jnp.concatenate([top, prev[SUBLANE:]], axis=0)
    carry_ref[...] = p[tb - 1:tb, :]
    yield
    xs = p + (prev - p) * mu_ref[...]
    yield
    r = xs[:, 0:RW_W]
    k = xs[:, RW_W:2 * RW_W]
    v = xs[:, 2 * RW_W:3 * RW_W]
    hwa = xs[:, 3 * RW_W:3 * RW_W + LANE]
    lw = -math.exp(-0.5) * _sigmoid(w0_ref[...] + _dot(jnp.tanh(hwa).astype(BF16), w2_ref[...]))
    yield
    a = _sigmoid(a0_ref[...] + _dot(hwa.astype(BF16), a2_ref[...]))
    yield
    if has_vres:
        hv = xs[:, RW_SHIFT_BASE:RW_SHIFT_BASE + LANE]
        v = v + (vf_ref[...] - v) * _sigmoid(v0_ref[...] + _dot(hv.astype(BF16), v2_ref[...]))
    else:
        vfo_ref[...] = v
    yield
    kk = k * kk_ref[...]
    kk = kk * lax.rsqrt(jnp.maximum(segsum(kk * kk), 1e-24))
    yield
    k = k * (1.0 + (a - 1.0) * ka_ref[...])
    yield

    lw_hi, lw_lo = _split_hi_lo(lw)
    cum = _dot(tri_ref[...], lw_hi) + _dot(tri_ref[...], lw_lo)
    tot = jnp.concatenate([jnp.broadcast_to(cum[e - 1:e], (RW_CHUNK, RW_W))
                           for e in range(RW_CHUNK, tb + 1, RW_CHUNK)], axis=0)
    yield
    w_inv = jnp.exp(-cum)
    w_end = jnp.exp(tot - cum)
    yield
    kka = kk * a
    rt_ref[...] = (r * jnp.exp(cum)).astype(BF16)
    yield
    at_ref[...] = (-kk * jnp.exp(cum - lw)).astype(BF16)
    yield
    bt_ref[...] = (kka * w_inv).astype(BF16)
    kt_ref[...] = (k * w_inv).astype(BF16)
    yield
    vv_ref[...] = v.astype(BF16)
    bw_ref[...] = kka * w_end
    kw_ref[...] = k * w_end
    tot_ref[...] = tot
    yield

    c = RW_CHUNK
    n_pairs = RW_HEADS // 2
    lane_c = lax.broadcasted_iota(jnp.int32, (c, LANE), 1)
    even_c = lane_c < RW_N
    even_2c = lax.broadcasted_iota(jnp.int32, (2 * c, LANE), 1) < RW_N
    ri = lax.broadcasted_iota(jnp.int32, (2 * c, 2 * c), 0)
    ci = lax.broadcasted_iota(jnp.int32, (2 * c, 2 * c), 1)
    cm = jnp.where(ci >= c, ci - c, ci)
    keep = ((ri >= c) & (cm <= ri - c)) | ((ri < c) & (cm < ri))
    same_head = (ri < c) == (ci < c)
    eye2 = (lax.broadcasted_iota(jnp.int32, (c, LANE), 0)
            == jnp.where(lane_c >= c, lane_c - c, lane_c)).astype(F32)

    def bdiag(m):
        zero = jnp.zeros_like(m)
        return jnp.concatenate([jnp.where(even_c, m, zero), jnp.where(even_c, zero, m)], axis=0)

    def adiag(m):
        zero = jnp.zeros_like(m)
        return jnp.concatenate([jnp.where(even_c, zero, m), jnp.where(even_c, m, zero)], axis=0)

    def phase_a(i):
        probs = [(i * cpi + cc, pr) for cc in range(cpi) for pr in range(n_pairs)]
        rows = [pl.ds(pl.multiple_of(ic * c, c), c) for ic, _ in probs]
        rows2 = [pl.ds(pl.multiple_of(ic * 2 * c, 2 * c), 2 * c) for ic, _ in probs]
        ls = [slice(pr * LANE, (pr + 1) * LANE) for _, pr in probs]
        rng = range(len(probs))
        at = [at_ref[rows[j], ls[j]] for j in rng]
        rt = [rt_ref[rows[j], ls[j]] for j in rng]
        bt = [bt_ref[rows[j], ls[j]] for j in rng]
        kt = [kt_ref[rows[j], ls[j]] for j in rng]
        vh = [vv_ref[rows[j], ls[j]] for j in rng]
        xar = [jnp.concatenate([at[j], rt[j]], axis=0) for j in rng]
        zero_b = jnp.zeros((2 * c, LANE), BF16)
        aa_e = [jnp.where(keep, _dot_nt(jnp.where(even_2c, xar[j], zero_b),
                                        jnp.concatenate([bt[j], kt[j]], axis=0)), 0.0) for j in rng]
        yield
        aa_o = [jnp.where(keep, _dot_nt(jnp.where(even_2c, zero_b, xar[j]),
                                        jnp.concatenate([kt[j], bt[j]], axis=0)), 0.0) for j in rng]
        yield
        nn = [jnp.where(even_c, aa_e[j][0:c], aa_o[j][0:c]) for j in rng]
        aak = [jnp.where(even_c, aa_o[j][0:c], aa_e[j][0:c]) for j in rng]
        for j in rng:
            pr = probs[j][1]
            ay_ref[rows[j], 2 * pr * LANE:(2 * pr + 1) * LANE] = jnp.where(
                even_c, aa_e[j][c:2 * c], aa_o[j][c:2 * c]).astype(BF16)
            ay_ref[rows[j], (2 * pr + 1) * LANE:(2 * pr + 2) * LANE] = jnp.where(
                even_c, aa_o[j][c:2 * c], aa_e[j][c:2 * c]).astype(BF16)
        yield
        av = [_dot(aak[j].astype(BF16), adiag(vh[j])) for j in rng]
        yield
        x = [eye2 + nn[j] for j in rng]
        pw = [nn[j].astype(BF16) for j in rng]
        pw = [_dot(pw[j], bdiag(pw[j])).astype(BF16) for j in rng]
        yield
        n_sq = int(math.log2(c)) - 1
        for it in range(n_sq):
            last = it == n_sq - 1
            lhs = [x[j].astype(BF16) if last else jnp.concatenate([x[j].astype(BF16), pw[j]], axis=0) for j in rng]
            out = [_dot(lhs[j], bdiag(pw[j])) for j in rng]
            x = [x[j] + out[j][0:c] for j in rng]
            if not last:
                pw = [out[j][c:2 * c].astype(BF16) for j in rng]
            yield
        fin = [_dot(x[j].astype(BF16), jnp.concatenate([bdiag(at[j]), bdiag(av[j].astype(BF16))], axis=1))
               for j in rng]
        yield
        ahat = [fin[j][:, 0:LANE].astype(BF16) for j in rng]
        xav = [fin[j][:, LANE:2 * LANE] for j in rng]
        zt = [jnp.concatenate([bw_ref[rows[j], ls[j]], kw_ref[rows[j], ls[j]]], axis=0).T.astype(BF16) for j in rng]
        zero_c = jnp.zeros((c, LANE), BF16)
        tcb = [_dot(zt[j], jnp.concatenate([ahat[j], zero_c], axis=0)) for j in rng]
        yield
        ccv = [_dot(zt[j], jnp.concatenate([xav[j].astype(BF16), vh[j]], axis=0)) for j in rng]
        yield
        for j in rng:
            lhs_ref[rows2[j], ls[j]] = jnp.concatenate([ahat[j], rt[j]], axis=0)
            xav_ref[rows[j], ls[j]] = xav[j]
            tc_ref[rows2[j], ls[j]] = jnp.where(same_head, tcb[j], 0.0).astype(BF16)
            cc_ref[rows2[j], ls[j]] = jnp.where(same_head, ccv[j], 0.0)
            tot_row = tot_ref[pl.ds(probs[j][0] * c, 1), ls[j]]
            wlt_ref[rows2[j], ls[j]] = jnp.exp(jnp.broadcast_to(tot_row, (2 * c, LANE)).T)
        yield

    for i in range(tb // (c * cpi)):
        yield from phase_a(i)

    ls = [slice(pr * LANE, (pr + 1) * LANE) for pr in range(n_pairs)]
    rng = range(n_pairs)
    mt = [state_ref[pr] for pr in rng]
    for ic in range(tb // c):
        rows = slice(ic * c, (ic + 1) * c)
        rows2 = slice(ic * 2 * c, (ic + 1) * 2 * c)
        mtb = [mt[pr].astype(BF16) for pr in rng]
        nxt = [_dot(tc_ref[rows2, ls[pr]], mtb[pr]) for pr in rng]
        mt = [mt[pr] * wlt_ref[rows2, ls[pr]] + nxt[pr] + cc_ref[rows2, ls[pr]] for pr in rng]
        out = [_dot(lhs_ref[rows2, ls[pr]], mtb[pr]) for pr in rng]
        ub = [(out[pr][0:c] + xav_ref[rows, ls[pr]]).astype(BF16) for pr in rng]
        vh = [vv_ref[rows, ls[pr]] for pr in rng]
        for pr in rng:
            uv = jnp.concatenate([bdiag(ub[pr]), adiag(vh[pr])], axis=0)
            yy_ref[rows, ls[pr]] = out[pr][c:2 * c] + _dot(ay_ref[rows, 2 * pr * LANE:(2 * pr + 2) * LANE], uv)
        yield
    for pr in rng:
        state_ref[pr] = mt[pr]

    y = yy_ref[...]
    inv_n = 1.0 / RW_N
    dlt = y - segsum(y) * inv_n
    yield
    var = segsum(dlt * dlt) * inv_n
    yield
    y = dlt * lax.rsqrt(var + RW_GN_EPS) * lnw_ref[...] + lnb_ref[...]
    yield
    y = y + segsum(r * k * rk_ref[...]) * v
    y_ref[...] = (y * _silu(gate_ref[...].astype(F32))).astype(BF16)


def _rwkv(p, vfirst, gate, consts, params, *, batch, seq, tb, cpi, n_seq):
    has_vres = vfirst is not None
    cw = p.shape[1]
    per_seq = lambda a: a.reshape(batch // n_seq, n_seq, seq, a.shape[-1])
    blk = lambda w: pl.BlockSpec((None, n_seq, tb, w), lambda b, i: (b, 0, i, 0))
    full = lambda a: pl.BlockSpec(a.shape, lambda b, i: (0,) * a.ndim)
    shared = list(consts) + list(params)
    ins = [per_seq(p)] + ([per_seq(vfirst)] if has_vres else []) + [per_seq(gate)] + shared
    in_specs = [blk(cw)] + ([blk(RW_W)] if has_vres else []) + [blk(RW_W)] + [full(a) for a in shared]
    out_specs = [blk(RW_W)]
    out_shape = [jax.ShapeDtypeStruct((batch // n_seq, n_seq, seq, RW_W), BF16)]
    if not has_vres:
        out_specs.append(blk(RW_W))
        out_shape.append(jax.ShapeDtypeStruct((batch // n_seq, n_seq, seq, RW_W), F32))
    vm = lambda rows, cols, dt: pltpu.VMEM((rows, cols), dt)
    scratch = [pltpu.VMEM((RW_HEADS // 2, LANE, LANE), F32), vm(1, cw, F32),
               vm(tb, RW_W, BF16), vm(tb, RW_W, BF16), vm(tb, RW_W, BF16), vm(tb, RW_W, BF16), vm(tb, RW_W, BF16),
               vm(tb, RW_W, F32), vm(tb, RW_W, F32), vm(tb, RW_W, F32), vm(tb, RW_W, F32),
               vm(2 * tb, RW_W, BF16), vm(tb, RW_W, F32), vm(tb, 2 * RW_W, BF16),
               vm(2 * tb, RW_W, BF16), vm(2 * tb, RW_W, F32), vm(2 * tb, RW_W, F32)]
    assert len(scratch) == RW_N_SCRATCH
    outs = pl.pallas_call(
        functools.partial(_rwkv_kernel, tb=tb, has_vres=has_vres, cpi=cpi, n_seq=n_seq, n_shared=len(shared)),
        grid=(batch // n_seq, seq // tb),
        in_specs=in_specs, out_specs=out_specs, out_shape=out_shape,
        scratch_shapes=scratch * n_seq,
        compiler_params=pltpu.CompilerParams(
            dimension_semantics=("parallel", "arbitrary"), vmem_limit_bytes=VMEM_LIMIT),
        name="rwkv7",
    )(*ins)
    outs = [o.reshape(batch * seq, RW_W) for o in outs]
    return outs if not has_vres else (outs[0], None)


def _outproj_kernel(x_ref, om_ref, od_ref, orw_ref, w_ref, fg_ref, o_ref, *, final):
    mixed = jnp.concatenate([om_ref[...], od_ref[...], orw_ref[...]], axis=1)
    y = x_ref[...] + _dot(mixed, w_ref[...])
    if final:
        y = _rms(y, fg_ref[...], NORM_EPS)
    o_ref[...] = y


def _outproj(x2, om, od, orw, w, fg, *, tm, final):
    t = x2.shape[0]
    row = lambda i: (i, 0)
    const = lambda i: (0, 0)
    return pl.pallas_call(
        functools.partial(_outproj_kernel, final=final),
        grid=(t // tm,),
        in_specs=[pl.BlockSpec((tm, D_MODEL), row), pl.BlockSpec((tm, MLA_W), row),
                  pl.BlockSpec((tm, DIFF_W), row), pl.BlockSpec((tm, RW_W), row),
                  pl.BlockSpec(w.shape, const), pl.BlockSpec(fg.shape, const)],
        out_specs=pl.BlockSpec((tm, D_MODEL), row),
        out_shape=jax.ShapeDtypeStruct((t, D_MODEL), F32),
        compiler_params=pltpu.CompilerParams(
            dimension_semantics=("parallel",), vmem_limit_bytes=VMEM_LIMIT),
        name="outproj",
    )(x2, om, od, orw, w, fg)


def _pack_inproj(w_in, w_vres):
    d = w_in.shape[0]
    o = 0
    cq = w_in[:, o:o + MLA_Q_RANK]; o += MLA_Q_RANK
    ckv = w_in[:, o:o + MLA_KV_RANK]; o += MLA_KV_RANK
    kpe = w_in[:, o:o + MLA_ROPE]; o += MLA_ROPE
    dqk = w_in[:, o:o + DIFF_G]; o += DIFF_G
    dv = w_in[:, o:o + DIFF_W]; o += DIFF_W
    gate = w_in[:, o:o + D_MIX]; o += D_MIX
    rw = w_in[:, o:]
    half = MLA_ROPE // 2
    z = lambda n: jnp.zeros((d, n), w_in.dtype)
    kp = jnp.concatenate([kpe, kpe[:, half:], kpe[:, :half], z(LANE - 2 * MLA_ROPE)], axis=1)
    g_mla, g_diff, g_rw = gate[:, :MLA_W], gate[:, MLA_W:MLA_W + DIFF_W], gate[:, MLA_W + DIFF_W:]
    cols = [cq, ckv, kp, dqk, g_rw, g_mla, g_diff, rw]
    if w_vres is not None:
        cols += [w_vres, z(LANE - RW_MV_RANK)]
    return jnp.concatenate(cols, axis=1).astype(BF16), dv.T.astype(BF16)


def _pack_mla(w_uq, w_ukv):
    rq, rkv = w_uq.shape[0], w_ukv.shape[0]
    q3 = w_uq.reshape(rq, MLA_HEADS, MLA_QK)
    zq = lambda n: jnp.zeros((rq, MLA_HEADS, n), w_uq.dtype)
    wq = jnp.concatenate([q3, zq(HP - MLA_QK)], axis=2)
    kv3 = w_ukv.reshape(rkv, MLA_HEADS, MLA_NOPE + MLA_V)
    wk = jnp.concatenate([kv3[:, :, :MLA_NOPE], jnp.zeros((rkv, MLA_HEADS, HP - MLA_NOPE), w_ukv.dtype)], axis=2)
    wv = kv3[:, :, MLA_NOPE:]
    flat = lambda a: a.reshape(a.shape[0], -1).astype(BF16)
    return flat(wq), flat(wk), flat(wv).T


def _pad_rows(w, top, total):
    return jnp.concatenate([jnp.zeros((top, w.shape[1]), w.dtype), w,
                            jnp.zeros((total - top - w.shape[0], w.shape[1]), w.dtype)], axis=0).astype(BF16)


def _rwkv_consts(tb):
    t = jnp.arange(tb)
    same = (t[:, None] // RW_CHUNK) == (t[None, :] // RW_CHUNK)
    tri = (same & (t[None, :] <= t[:, None])).astype(BF16)
    hl = jnp.arange(LANE) // RW_N
    seg = (hl[:, None] == hl[None, :]).astype(BF16)
    return tri, seg


def kernel(x, positions, pre_g, w_in, w_in_vres, w_out, mla_gq, mla_gkv, mla_wuq, mla_wukv, diff_lam, diff_gsub, rw_mu, rw_mu_vres, rw_w0, rw_w2, rw_a0, rw_a2, rw_v0, rw_v2, rw_kk, rw_ka, rw_rk, rw_lnw, rw_lnb, final_g):
    batch, seq, _ = x.shape
    depth = pre_g.shape[0]
    tm = min(512, seq)
    tq = min(512, seq)
    tkv = tq // 2
    tb = min(256, seq)
    assert seq % tm == 0 and seq % tq == 0 and tm % tkv == 0
    assert seq % tb == 0 and tb % RW_CHUNK == 0

    x2 = x.reshape(batch * seq, D_MODEL)
    tab = _rope_table(positions)
    pos_f = positions.astype(F32)
    posc = jnp.broadcast_to(pos_f.reshape(seq, 1), (seq, LANE))
    consts = _rwkv_consts(tb)
    r1 = lambda a: a.reshape(1, -1).astype(F32)

    vfirst = None
    for layer in range(depth):
        vres = layer > 0
        w, wdvt = _pack_inproj(w_in[layer], w_in_vres[layer - 1] if vres else None)
        wq, wk, wvt = _pack_mla(mla_wuq[layer], mla_wukv[layer])
        q, k, vt, d, dvt, gate, rw = _inproj(x2, r1(pre_g[layer]), w, tab, r1(mla_gq[layer]), r1(mla_gkv[layer]),
                                             wq, wk, wvt, wdvt,
                                             seq=seq, tm=min(2 * tm, seq), tkv=tkv)
        lam_init = 0.8 - 0.6 * math.exp(-0.3 * (layer + 1))
        gsub_col = diff_gsub[layer].reshape(-1, 1).astype(F32)
        o_mla, o_diff = _attn(q, k, vt, d, dvt, gate, posc, diff_lam[layer].astype(F32), gsub_col,
                              batch=batch, seq=seq, tq=tq, lam_init=lam_init)
        mu = rw_mu[layer]
        if vres:
            mu = jnp.concatenate([mu, rw_mu_vres[layer - 1], jnp.zeros((LANE - RW_MV_RANK,), mu.dtype)])
        params = [r1(mu), r1(rw_w0[layer]), _pad_rows(rw_w2[layer], 0, LANE), r1(rw_a0[layer]),
                  _pad_rows(rw_a2[layer], RW_DECAY_RANK, LANE)]
        if vres:
            params += [r1(rw_v0[layer - 1]), _pad_rows(rw_v2[layer - 1], 0, LANE)]
        params += [r1(rw_kk[layer]), r1(rw_ka[layer]), r1(rw_rk[layer]), r1(rw_lnw[layer]), r1(rw_lnb[layer])]
        o_rw, vf = _rwkv(rw, vfirst, gate, consts, params, batch=batch, seq=seq, tb=tb, cpi=4,
                          n_seq=2 if batch % 2 == 0 else 1)
        if not vres:
            vfirst = vf
        x2 = _outproj(x2, o_mla, o_diff, o_rw, w_out[layer].astype(BF16), r1(final_g),
                      tm=min(2 * tm, seq), final=(layer == depth - 1))
    return x2.reshape(batch, seq, D_MODEL)
```

```python
import functools
import math

import jax
import jax.numpy as jnp
from jax import lax
from jax.experimental import pallas as pl
from jax.experimental.pallas import tpu as pltpu

F32 = jnp.float32
BF16 = jnp.bfloat16

D_MODEL = 1024
D_MIX = 1024
NORM_EPS = 1e-6
MLA_HEADS, MLA_NOPE, MLA_ROPE, MLA_V = 6, 64, 32, 64
MLA_Q_RANK, MLA_KV_RANK = 256, 128
MLA_QK = MLA_NOPE + MLA_ROPE
ROPE_THETA = 10000.0
MLA_W = MLA_HEADS * MLA_V
DIFF_HEADS, DIFF_DH = 4, 32
DIFF_W = DIFF_HEADS * 2 * DIFF_DH
DIFF_SUBLN_EPS = 1e-5
RW_HEADS, RW_N = 6, 64
RW_W = RW_HEADS * RW_N
RW_DECAY_RANK, RW_AAA_RANK, RW_MV_RANK = 64, 64, 32
RW_GN_EPS = 64e-5
RW_SHIFT_BASE = 3 * RW_W + RW_DECAY_RANK + RW_AAA_RANK

LANE = 128
SUBLANE = 8
BF16_ROWS = 16
LOG2E = math.log2(math.e)
HP = LANE
MLA_VA = MLA_V + BF16_ROWS
DIFF_VA = 2 * DIFF_DH + BF16_ROWS
RW_CHUNK = 64
RW_N_SCRATCH = 17
RW_SKEW = 14

MLA_G = MLA_Q_RANK + MLA_KV_RANK + LANE
DIFF_G = 2 * DIFF_W
GATE_G = D_MIX
OFF_DIFF = MLA_G
OFF_GATE = OFF_DIFF + DIFF_G
OFF_RW = OFF_GATE + GATE_G

VMEM_LIMIT = 52 * 1024 * 1024

_NT = (((1,), (1,)), ((), ()))


def _dot(a, b):
    return jnp.dot(a, b, preferred_element_type=F32)


def _dot_nt(a, b):
    return lax.dot_general(a, b, _NT, preferred_element_type=F32)


def _split_hi_lo(x):
    hi = x.astype(BF16)
    return hi, (x - hi.astype(F32)).astype(BF16)


def _sigmoid(z):
    return 1.0 / (1.0 + jnp.exp(-z))


def _silu(g):
    return g * _sigmoid(g)


def _rms(x, g, eps):
    return x * lax.rsqrt(jnp.mean(x * x, axis=-1, keepdims=True) + eps) * g


def _rope_table_kernel(pos_ref, tab_ref):
    pos = pos_ref[...]
    lane = lax.broadcasted_iota(jnp.int32, (1, LANE), 1)
    half = MLA_ROPE // 2
    in_rope = (lane >= MLA_NOPE) & (lane < MLA_QK)
    first = lane < MLA_NOPE + half
    idx = jnp.where(first, lane - MLA_NOPE, lane - MLA_NOPE - half).astype(F32)
    inv = jnp.exp(idx * (-math.log(ROPE_THETA) / half))
    ang = pos * inv
    cos = jnp.where(in_rope, jnp.cos(ang), jnp.where(lane < MLA_NOPE, 1.0, 0.0))
    sin = jnp.sin(ang)
    sin = jnp.where(in_rope, jnp.where(first, -sin, sin), 0.0)
    scale = MLA_QK ** -0.5 * LOG2E
    tab_ref[:, 0 * LANE:1 * LANE] = cos * scale
    tab_ref[:, 1 * LANE:2 * LANE] = jnp.where(first, sin, 0.0) * scale
    tab_ref[:, 2 * LANE:3 * LANE] = jnp.where(first, 0.0, sin) * scale
    tab_ref[:, 3 * LANE:4 * LANE] = cos
    tab_ref[:, 4 * LANE:5 * LANE] = sin


def _rope_table(positions):
    s = positions.shape[0]
    return pl.pallas_call(
        _rope_table_kernel,
        out_shape=jax.ShapeDtypeStruct((s, 5 * LANE), F32),
        name="rope_table",
    )(positions.astype(F32).reshape(s, 1))


def _inproj_kernel(x_ref, g_ref, w_ref, tab_ref, gq_ref, gkv_ref, wq_ref, wk_ref, wvt_ref, wdvt_ref,
                   q_ref, k_ref, vt_ref, d_ref, dvt_ref, gate_ref, rw_ref, *, tkv):
    o_kpe = MLA_Q_RANK + MLA_KV_RANK
    half = MLA_ROPE // 2
    ones = jnp.ones((BF16_ROWS, tkv), BF16)
    for t in range(vt_ref.shape[0]):
        rows = slice(t * tkv, (t + 1) * tkv)
        h = _rms(x_ref[rows, :], g_ref[...], NORM_EPS).astype(BF16)

        mla = _dot(h, w_ref[:, 0:MLA_G])
        cqn = _rms(mla[:, 0:MLA_Q_RANK], gq_ref[...], NORM_EPS).astype(BF16)
        ckvn = _rms(mla[:, MLA_Q_RANK:o_kpe], gkv_ref[...], NORM_EPS).astype(BF16)
        tab = tab_ref[rows, :]
        cos_q, sin_q1, sin_q2 = tab[:, 0:LANE], tab[:, LANE:2 * LANE], tab[:, 2 * LANE:3 * LANE]
        cos_k, sin_k = tab[:, 3 * LANE:4 * LANE], tab[:, 4 * LANE:5 * LANE]
        kg = mla[:, o_kpe:o_kpe + LANE]
        kpe = pltpu.roll(kg, MLA_NOPE, 1) * cos_k + pltpu.roll(kg, MLA_NOPE - MLA_ROPE, 1) * sin_k
        qa = _dot(cqn, wq_ref[...])
        kn = _dot(ckvn, wk_ref[...])
        for hd in range(MLA_HEADS):
            sl = slice(hd * HP, (hd + 1) * HP)
            qh = qa[:, sl]
            q_ref[rows, sl] = (qh * cos_q + pltpu.roll(qh, HP - half, 1) * sin_q1
                               + pltpu.roll(qh, half, 1) * sin_q2).astype(BF16)
            k_ref[rows, sl] = (kn[:, sl] + kpe).astype(BF16)
        for src, dst, heads, width in ((_dot_nt(wvt_ref[...], ckvn), vt_ref, MLA_HEADS, MLA_V),
                                       (_dot_nt(wdvt_ref[...], h), dvt_ref, DIFF_HEADS, 2 * DIFF_DH)):
            src = src.astype(BF16)
            for hd in range(heads):
                r0 = hd * (width + BF16_ROWS)
                dst[t, r0:r0 + width, :] = src[hd * width:(hd + 1) * width, :]
                dst[t, r0 + width:r0 + width + BF16_ROWS, :] = ones

        dd = _dot(h, w_ref[:, OFF_DIFF:OFF_GATE])
        d_ref[rows, 0:DIFF_W] = (dd[:, 0:DIFF_W] * (DIFF_DH ** -0.5 * LOG2E)).astype(BF16)
        d_ref[rows, DIFF_W:DIFF_G] = dd[:, DIFF_W:DIFF_G].astype(BF16)
        gate_ref[rows, :] = _dot(h, w_ref[:, OFF_GATE:OFF_RW]).astype(BF16)
        rw_ref[rows, :] = _dot(h, w_ref[:, OFF_RW:])


def _inproj(x2, g, w, tab, gq, gkv, wq, wk, wvt, wdvt, *, seq, tm, tkv):
    t = x2.shape[0]
    ctot = w.shape[1]
    rw_g = ctot - OFF_RW
    n_seq_tiles = seq // tm
    kv_per_tile = tm // tkv
    row = lambda i: (i, 0)
    slab = lambda i: (i, 0, 0)
    const = lambda i: (0, 0)
    full = lambda a: pl.BlockSpec(a.shape, const, pipeline_mode=pl.Buffered(1))
    return pl.pallas_call(
        functools.partial(_inproj_kernel, tkv=tkv),
        grid=(t // tm,),
        in_specs=[
            pl.BlockSpec((tm, D_MODEL), row),
            full(g), full(w),
            pl.BlockSpec((tm, tab.shape[1]), lambda i: (i % n_seq_tiles, 0)),
            full(gq), full(gkv), full(wq), full(wk), full(wvt), full(wdvt),
        ],
        out_specs=[
            pl.BlockSpec((tm, MLA_HEADS * HP), row),
            pl.BlockSpec((tm, MLA_HEADS * HP), row),
            pl.BlockSpec((kv_per_tile, MLA_HEADS * MLA_VA, tkv), slab),
            pl.BlockSpec((tm, DIFF_G), row),
            pl.BlockSpec((kv_per_tile, DIFF_HEADS * DIFF_VA, tkv), slab),
            pl.BlockSpec((tm, GATE_G), row),
            pl.BlockSpec((tm, rw_g), row),
        ],
        out_shape=[
            jax.ShapeDtypeStruct((t, MLA_HEADS * HP), BF16),
            jax.ShapeDtypeStruct((t, MLA_HEADS * HP), BF16),
            jax.ShapeDtypeStruct((t // tkv, MLA_HEADS * MLA_VA, tkv), BF16),
            jax.ShapeDtypeStruct((t, DIFF_G), BF16),
            jax.ShapeDtypeStruct((t // tkv, DIFF_HEADS * DIFF_VA, tkv), BF16),
            jax.ShapeDtypeStruct((t, GATE_G), BF16),
            jax.ShapeDtypeStruct((t, rw_g), F32),
        ],
        compiler_params=pltpu.CompilerParams(
            dimension_semantics=("parallel",), vmem_limit_bytes=VMEM_LIMIT),
        name="inproj",
    )(x2, g, w, tab, gq, gkv, wq, wk, wvt, wdvt)


def _softmax_step(s, m):
    m_new = jnp.maximum(m, jnp.max(s, axis=0, keepdims=True))
    return m_new, jnp.exp2(m - m_new), jnp.exp2(s - m_new).astype(BF16)


def _causal_t(n_keys, n_queries):
    return (lax.broadcasted_iota(jnp.int32, (n_keys, n_queries), 0)
            <= lax.broadcasted_iota(jnp.int32, (n_keys, n_queries), 1))


def _run_interleaved(gens):
    out = [None] * len(gens)
    live = list(enumerate(gens))
    while live:
        still = []
        for idx, g in live:
            try:
                next(g)
                still.append((idx, g))
            except StopIteration as stop:
                out[idx] = stop.value
        live = still
    return out


def _attn_kernel(q_ref, k_ref, vt_ref, gm_ref, posc_ref, lam_ref, gsub_ref, dq_ref, dk_ref, dvt_ref, gd_ref,
                 om_ref, od_ref, qm_ref, p_ref, *, tq, tk, lam_init):
    i = pl.program_id(1)
    half = tq // 2
    n_slab_main = tq // tk

    n_h = MLA_HEADS
    hs = [slice(hh * HP, (hh + 1) * HP) for hh in range(n_h)]
    ps = [slice(hh * MLA_VA, (hh + 1) * MLA_VA) for hh in range(n_h)]

    def mla_tile(j, carry, qs, mask, n_slab):
        rows = pl.ds(pl.multiple_of(j * (n_slab * tk), n_slab * tk), n_slab * tk)
        s = [_dot_nt(k_ref[rows, hs[hh]], qs[hh]) for hh in range(n_h)]
        if mask is not None:
            s = [jnp.where(mask, x, -jnp.inf) for x in s]
        yield
        st = []
        for hh in range(n_h):
            st.append(_softmax_step(s[hh], carry[hh][0]))
            yield
        vt = [vt_ref[j * n_slab + t] for t in range(n_slab)]
        pv = [sum(_dot(vt[t][ps[hh]], st[hh][2][t * tk:(t + 1) * tk]) for t in range(n_slab))
              for hh in range(n_h)]
        yield
        return tuple((st[hh][0], st[hh][1] * carry[hh][1] + pv[hh]) for hh in range(n_h))

    qs_full = [q_ref[:, hs[hh]] for hh in range(n_h)]
    qs_late = [q[half:] for q in qs_full]
    mla_one = (jnp.full((1, tq), -jnp.inf, F32), jnp.zeros((MLA_VA, tq), F32))

    n_maps = 2 * DIFF_HEADS
    dq = dq_ref[...]
    lane_q = lax.broadcasted_iota(jnp.int32, (tq, DIFF_W), 1)
    for hm in range(n_maps):
        own = (lane_q >= hm * DIFF_DH) & (lane_q < (hm + 1) * DIFF_DH)
        qm_ref[hm * tq:(hm + 1) * tq, :] = jnp.where(own, dq, jnp.zeros_like(dq))
    slopes = [LOG2E * 2.0 ** (-8.0 * (h + 1) / DIFF_HEADS) for h in range(DIFF_HEADS)]
    dv = 2 * DIFF_DH
    va = DIFF_VA

    def diff_tile(j, carry, q0, nq, mask, n_slab):
        nk = n_slab * tk
        rows = pl.ds(pl.multiple_of(j * nk, nk), nk)
        kt = dk_ref[rows, :]
        pos_k = posc_ref[rows, :]
        bias = [jnp.tile(slope * pos_k, (1, nq // LANE)) for slope in slopes]
        st = []
        for hm in range(n_maps):
            cols = slice(hm * tq + q0, hm * tq + q0 + nq)
            s = _dot_nt(kt, qm_ref[cols, :]) + bias[hm // 2]
            if mask is not None:
                s = jnp.where(mask, s, -jnp.inf)
            m, alpha, p = _softmax_step(s, carry[hm][0])
            p_ref[0:nk, cols] = p
            st.append((m, alpha))
            yield
        vts = [dvt_ref[j * n_slab + t] for t in range(n_slab)]
        new = []
        for hm in range(n_maps):
            cols = slice(hm * tq + q0, hm * tq + q0 + nq)
            vr = slice((hm // 2) * va, (hm // 2 + 1) * va)
            pv = sum(_dot(vts[t][vr], p_ref[t * tk:(t + 1) * tk, cols]) for t in range(n_slab))
            new.append((st[hm][0], st[hm][1] * carry[hm][1] + pv))
        yield
        return tuple(new)

    diff_one = (jnp.full((1, tq), -jnp.inf, F32), jnp.zeros((va, tq), F32))

    def main(j, c):
        return tuple(_run_interleaved([mla_tile(j, c[0], qs_full, None, n_slab_main),
                                       diff_tile(j, c[1], 0, tq, None, n_slab_main)]))

    cm, cd = lax.fori_loop(0, i, main, ((mla_one,) * n_h, (diff_one,) * n_maps))
    cm, cd = _run_interleaved([mla_tile(2 * i, cm, qs_full, _causal_t(tk, tq), 1),
                               diff_tile(2 * i, cd, 0, tq, _causal_t(tk, tq), 1)])
    late = lambda c: tuple(tuple(a[:, half:] for a in x) for x in c)
    lm, ld = _run_interleaved([mla_tile(2 * i + 1, late(cm), qs_late, _causal_t(tk, half), 1),
                               diff_tile(2 * i + 1, late(cd), half, half, _causal_t(tk, half), 1)])
    merge = lambda c, l: [jnp.concatenate([c[n][1][:, :half], l[n][1]], axis=1) for n in range(len(c))]

    accs = merge(cm, lm)
    o = jnp.concatenate([a[0:MLA_V] / a[MLA_V:MLA_V + 1] for a in accs], axis=0).T
    om_ref[...] = (o * _silu(gm_ref[...].astype(F32))).astype(BF16)

    accs = merge(cd, ld)
    lam = lam_ref[...]
    lam_full = (jnp.exp(jnp.sum(lam[0:1] * lam[1:2], axis=-1, keepdims=True))
                - jnp.exp(jnp.sum(lam[2:3] * lam[3:4], axis=-1, keepdims=True)) + lam_init)
    heads = []
    for h in range(DIFF_HEADS):
        a0, a1 = accs[2 * h], accs[2 * h + 1]
        o = a0[0:dv] / a0[dv:dv + 1] - lam_full * (a1[0:dv] / a1[dv:dv + 1])
        ms = jnp.mean(o * o, axis=0, keepdims=True)
        heads.append(o * lax.rsqrt(ms + DIFF_SUBLN_EPS) * gsub_ref[...] * (1.0 - lam_init))
    o = jnp.concatenate(heads, axis=0).T
    od_ref[...] = (o * _silu(gd_ref[...].astype(F32))).astype(BF16)


def _attn(q, k, vt, d, dvt, gate, posc, lam, gsub_col, *, batch, seq, tq, lam_init):
    nq = seq // tq
    tk = tq // 2
    nkv = seq // tk
    n_maps = 2 * DIFF_HEADS
    row = lambda b, i: (b * nq + i, 0)
    const = lambda a: pl.BlockSpec(a.shape, lambda b, i: (0, 0))
    return pl.pallas_call(
        functools.partial(_attn_kernel, tq=tq, tk=tk, lam_init=lam_init),
        grid=(batch, nq),
        in_specs=[
            pl.BlockSpec((tq, MLA_HEADS * HP), row),
            pl.BlockSpec((seq, MLA_HEADS * HP), lambda b, i: (b, 0)),
            pl.BlockSpec((nkv, MLA_HEADS * MLA_VA, tk), lambda b, i: (b, 0, 0)),
            pl.BlockSpec((tq, MLA_W), lambda b, i: (b * nq + i, RW_W // MLA_W)),
            const(posc), const(lam), const(gsub_col),
            pl.BlockSpec((tq, DIFF_W), row),
            pl.BlockSpec((seq, DIFF_W), lambda b, i: (b, 1)),
            pl.BlockSpec((nkv, DIFF_HEADS * DIFF_VA, tk), lambda b, i: (b, 0, 0)),
            pl.BlockSpec((tq, DIFF_W), lambda b, i: (b * nq + i, (RW_W + MLA_W) // DIFF_W)),
        ],
        out_specs=[pl.BlockSpec((tq, MLA_W), row), pl.BlockSpec((tq, DIFF_W), row)],
        out_shape=[jax.ShapeDtypeStruct((batch * seq, MLA_W), BF16),
                   jax.ShapeDtypeStruct((batch * seq, DIFF_W), BF16)],
        scratch_shapes=[pltpu.VMEM((n_maps * tq, DIFF_W), BF16), pltpu.VMEM((tq, n_maps * tq), BF16)],
        compiler_params=pltpu.CompilerParams(
            dimension_semantics=("parallel", "parallel"), vmem_limit_bytes=VMEM_LIMIT),
        name="attn_flash",
    )(q, k, vt, gate, posc, lam, gsub_col, d, d, dvt, gate)


def _rwkv_kernel(*refs, tb, has_vres, cpi, n_seq, n_shared):
    n_in = 3 if has_vres else 2
    n_io = len(refs) - n_seq * RW_N_SCRATCH
    live = []
    for e in range(n_seq):
        views = [r if n_in <= idx < n_in + n_shared else r.at[e] for idx, r in enumerate(refs[:n_io])]
        views += refs[n_io + e * RW_N_SCRATCH:n_io + (e + 1) * RW_N_SCRATCH]
        live.append(_rwkv_seq(*views, tb=tb, has_vres=has_vres, cpi=cpi, skew=e * RW_SKEW))
    while live:
        live = [g for g in live if next(g, "done") != "done"]


def _rwkv_seq(*refs, tb, has_vres, cpi, skew):
    for _ in range(skew):
        yield
    n_sc = RW_N_SCRATCH
    if has_vres:
        (p_ref, vf_ref, gate_ref, tri_ref, seg_ref, mu_ref, w0_ref, w2_ref, a0_ref, a2_ref,
         v0_ref, v2_ref, kk_ref, ka_ref, rk_ref, lnw_ref, lnb_ref, y_ref) = refs[:-n_sc]
    else:
        (p_ref, gate_ref, tri_ref, seg_ref, mu_ref, w0_ref, w2_ref, a0_ref, a2_ref,
         kk_ref, ka_ref, rk_ref, lnw_ref, lnb_ref, y_ref, vfo_ref) = refs[:-n_sc]
    (state_ref, carry_ref, at_ref, rt_ref, bt_ref, kt_ref, vv_ref, bw_ref, kw_ref, tot_ref, yy_ref,
     lhs_ref, xav_ref, ay_ref, tc_ref, cc_ref, wlt_ref) = refs[-n_sc:]

    @pl.when(pl.program_id(1) == 0)
    def _():
        state_ref[...] = jnp.zeros_like(state_ref)
        carry_ref[...] = jnp.zeros_like(carry_ref)

    seg = seg_ref[...]
    pair_lanes = [slice(pr * LANE, (pr + 1) * LANE) for pr in range(RW_HEADS // 2)]

    def segsum(x):
        xb = x.astype(BF16)
        return jnp.concatenate([_dot(xb[:, ls], seg) for ls in pair_lanes], axis=1)

    p = p_ref[...]
    prev = pltpu.roll(p, 1, 0)
    top = jnp.where(lax.broadcasted_iota(jnp.int32, (SUBLANE, 1), 0) == 0, carry_ref[...], prev[0:SUBLANE])
    prev = jnp.concatenate([top, prev[SUBLANE:]], axis=0)
    carry_ref[...] = p[tb - 1:tb, :]
    yield
    xs = p + (prev - p) * mu_ref[...]
    yield
    r = xs[:, 0:RW_W]
    k = xs[:, RW_W:2 * RW_W]
    v = xs[:, 2 * RW_W:3 * RW_W]
    hwa = xs[:, 3 * RW_W:3 * RW_W + LANE]
    lw = -math.exp(-0.5) * _sigmoid(w0_ref[...] + _dot(jnp.tanh(hwa).astype(BF16), w2_ref[...]))
    yield
    a = _sigmoid(a0_ref[...] + _dot(hwa.astype(BF16), a2_ref[...]))
    yield
    if has_vres:
        hv = xs[:, RW_SHIFT_BASE:RW_SHIFT_BASE + LANE]
        v = v + (vf_ref[...] - v) * _sigmoid(v0_ref[...] + _dot(hv.astype(BF16), v2_ref[...]))
    else:
        vfo_ref[...] = v
    yield
    kk = k * kk_ref[...]
    kk = kk * lax.rsqrt(jnp.maximum(segsum(kk * kk), 1e-24))
    yield
    k = k * (1.0 + (a - 1.0) * ka_ref[...])
    yield

    lw_hi, lw_lo = _split_hi_lo(lw)
    cum = _dot(tri_ref[...], lw_hi) + _dot(tri_ref[...], lw_lo)
    tot = jnp.concatenate([jnp.broadcast_to(cum[e - 1:e], (RW_CHUNK, RW_W))
                           for e in range(RW_CHUNK, tb + 1, RW_CHUNK)], axis=0)
    yield
    w_inv = jnp.exp(-cum)
    w_end = jnp.exp(tot - cum)
    yield
    kka = kk * a
    rt_ref[...] = (r * jnp.exp(cum)).astype(BF16)
    yield
    at_ref[...] = (-kk * jnp.exp(cum - lw)).astype(BF16)
    yield
    bt_ref[...] = (kka * w_inv).astype(BF16)
    kt_ref[...] = (k * w_inv).astype(BF16)
    yield
    vv_ref[...] = v.astype(BF16)
    bw_ref[...] = kka * w_end
    kw_ref[...] = k * w_end
    tot_ref[...] = tot
    yield

    c = RW_CHUNK
    n_pairs = RW_HEADS // 2
    lane_c = lax.broadcasted_iota(jnp.int32, (c, LANE), 1)
    even_c = lane_c < RW_N
    even_2c = lax.broadcasted_iota(jnp.int32, (2 * c, LANE), 1) < RW_N
    ri = lax.broadcasted_iota(jnp.int32, (2 * c, 2 * c), 0)
    ci = lax.broadcasted_iota(jnp.int32, (2 * c, 2 * c), 1)
    cm = jnp.where(ci >= c, ci - c, ci)
    keep = ((ri >= c) & (cm <= ri - c)) | ((ri < c) & (cm < ri))
    same_head = (ri < c) == (ci < c)
    eye2 = (lax.broadcasted_iota(jnp.int32, (c, LANE), 0)
            == jnp.where(lane_c >= c, lane_c - c, lane_c)).astype(F32)

    def bdiag(m):
        zero = jnp.zeros_like(m)
        return jnp.concatenate([jnp.where(even_c, m, zero), jnp.where(even_c, zero, m)], axis=0)

    def adiag(m):
        zero = jnp.zeros_like(m)
        return jnp.concatenate([jnp.where(even_c, zero, m), jnp.where(even_c, m, zero)], axis=0)

    def phase_a(i):
        probs = [(i * cpi + cc, pr) for cc in range(cpi) for pr in range(n_pairs)]
        rows = [pl.ds(pl.multiple_of(ic * c, c), c) for ic, _ in probs]
        rows2 = [pl.ds(pl.multiple_of(ic * 2 * c, 2 * c), 2 * c) for ic, _ in probs]
        ls = [slice(pr * LANE, (pr + 1) * LANE) for _, pr in probs]
        rng = range(len(probs))
        at = [at_ref[rows[j], ls[j]] for j in rng]
        rt = [rt_ref[rows[j], ls[j]] for j in rng]
        bt = [bt_ref[rows[j], ls[j]] for j in rng]
        kt = [kt_ref[rows[j], ls[j]] for j in rng]
        vh = [vv_ref[rows[j], ls[j]] for j in rng]
        xar = [jnp.concatenate([at[j], rt[j]], axis=0) for j in rng]
        zero_b = jnp.zeros((2 * c, LANE), BF16)
        aa_e = [jnp.where(keep, _dot_nt(jnp.where(even_2c, xar[j], zero_b),
                                        jnp.concatenate([bt[j], kt[j]], axis=0)), 0.0) for j in rng]
        yield
        aa_o = [jnp.where(keep, _dot_nt(jnp.where(even_2c, zero_b, xar[j]),
                                        jnp.concatenate([kt[j], bt[j]], axis=0)), 0.0) for j in rng]
        yield
        nn = [jnp.where(even_c, aa_e[j][0:c], aa_o[j][0:c]) for j in rng]
        aak = [jnp.where(even_c, aa_o[j][0:c], aa_e[j][0:c]) for j in rng]
        for j in rng:
            pr = probs[j][1]
            ay_ref[rows[j], 2 * pr * LANE:(2 * pr + 1) * LANE] = jnp.where(
                even_c, aa_e[j][c:2 * c], aa_o[j][c:2 * c]).astype(BF16)
            ay_ref[rows[j], (2 * pr + 1) * LANE:(2 * pr + 2) * LANE] = jnp.where(
                even_c, aa_o[j][c:2 * c], aa_e[j][c:2 * c]).astype(BF16)
        yield
        av = [_dot(aak[j].astype(BF16), adiag(vh[j])) for j in rng]
        yield
        x = [eye2 + nn[j] for j in rng]
        pw = [nn[j].astype(BF16) for j in rng]
        pw = [_dot(pw[j], bdiag(pw[j])).astype(BF16) for j in rng]
        yield
        n_sq = int(math.log2(c)) - 1
        for it in range(n_sq):
            last = it == n_sq - 1
            lhs = [x[j].astype(BF16) if last else jnp.concatenate([x[j].astype(BF16), pw[j]], axis=0) for j in rng]
            out = [_dot(lhs[j], bdiag(pw[j])) for j in rng]
            x = [x[j] + out[j][0:c] for j in rng]
            if not last:
                pw = [out[j][c:2 * c].astype(BF16) for j in rng]
            yield
        fin = [_dot(x[j].astype(BF16), jnp.concatenate([bdiag(at[j]), bdiag(av[j].astype(BF16))], axis=1))
               for j in rng]
        yield
        ahat = [fin[j][:, 0:LANE].astype(BF16) for j in rng]
        xav = [fin[j][:, LANE:2 * LANE] for j in rng]
        zt = [jnp.concatenate([bw_ref[rows[j], ls[j]], kw_ref[rows[j], ls[j]]], axis=0).T.astype(BF16) for j in rng]
        zero_c = jnp.zeros((c, LANE), BF16)
        tcb = [_dot(zt[j], jnp.concatenate([ahat[j], zero_c], axis=0)) for j in rng]
        yield
        ccv = [_dot(zt[j], jnp.concatenate([xav[j].astype(BF16), vh[j]], axis=0)) for j in rng]
        yield
        for j in rng:
            lhs_ref[rows2[j], ls[j]] = jnp.concatenate([ahat[j], rt[j]], axis=0)
            xav_ref[rows[j], ls[j]] = xav[j]
            tc_ref[rows2[j], ls[j]] = jnp.where(same_head, tcb[j], 0.0).astype(BF16)
            cc_ref[rows2[j], ls[j]] = jnp.where(same_head, ccv[j], 0.0)
            tot_row = tot_ref[pl.ds(probs[j][0] * c, 1), ls[j]]
            wlt_ref[rows2[j], ls[j]] = jnp.exp(jnp.broadcast_to(tot_row, (2 * c, LANE)).T)
        yield

    for i in range(tb // (c * cpi)):
        yield from phase_a(i)

    ls = [slice(pr * LANE, (pr + 1) * LANE) for pr in range(n_pairs)]
    rng = range(n_pairs)
    mt = [state_ref[pr] for pr in rng]
    for ic in range(tb // c):
        rows = slice(ic * c, (ic + 1) * c)
        rows2 = slice(ic * 2 * c, (ic + 1) * 2 * c)
        mtb = [mt[pr].astype(BF16) for pr in rng]
        nxt = [_dot(tc_ref[rows2, ls[pr]], mtb[pr]) for pr in rng]
        mt = [mt[pr] * wlt_ref[rows2, ls[pr]] + nxt[pr] + cc_ref[rows2, ls[pr]] for pr in rng]
        out = [_dot(lhs_ref[rows2, ls[pr]], mtb[pr]) for pr in rng]
        ub = [(out[pr][0:c] + xav_ref[rows, ls[pr]]).astype(BF16) for pr in rng]
        vh = [vv_ref[rows, ls[pr]] for pr in rng]
        for pr in rng:
            uv = jnp.concatenate([bdiag(ub[pr]), adiag(vh[pr])], axis=0)
            yy_ref[rows, ls[pr]] = out[pr][c:2 * c] + _dot(ay_ref[rows, 2 * pr * LANE:(2 * pr + 2) * LANE], uv)
        yield
    for pr in rng:
        state_ref[pr] = mt[pr]

    y = yy_ref[...]
    inv_n = 1.0 / RW_N
    dlt = y - segsum(y) * inv_n
    yield
    var = segsum(dlt * dlt) * inv_n
    yield
    y = dlt * lax.rsqrt(var + RW_GN_EPS) * lnw_ref[...] + lnb_ref[...]
    yield
    y = y + segsum(r * k * rk_ref[...]) * v
    y_ref[...] = (y * _silu(gate_ref[...].astype(F32))).astype(BF16)


def _rwkv(p, vfirst, gate, consts, params, *, batch, seq, tb, cpi, n_seq):
    has_vres = vfirst is not None
    cw = p.shape[1]
    per_seq = lambda a: a.reshape(batch // n_seq, n_seq, seq, a.shape[-1])
    blk = lambda w: pl.BlockSpec((None, n_seq, tb, w), lambda b, i: (b, 0, i, 0))
    full = lambda a: pl.BlockSpec(a.shape, lambda b, i: (0,) * a.ndim)
    shared = list(consts) + list(params)
    ins = [per_seq(p)] + ([per_seq(vfirst)] if has_vres else []) + [per_seq(gate)] + shared
    in_specs = [blk(cw)] + ([blk(RW_W)] if has_vres else []) + [blk(RW_W)] + [full(a) for a in shared]
    out_specs = [blk(RW_W)]
    out_shape = [jax.ShapeDtypeStruct((batch // n_seq, n_seq, seq, RW_W), BF16)]
    if not has_vres:
        out_specs.append(blk(RW_W))
        out_shape.append(jax.ShapeDtypeStruct((batch // n_seq, n_seq, seq, RW_W), F32))
    vm = lambda rows, cols, dt: pltpu.VMEM((rows, cols), dt)
    scratch = [pltpu.VMEM((RW_HEADS // 2, LANE, LANE), F32), vm(1, cw, F32),
               vm(tb, RW_W, BF16), vm(tb, RW_W, BF16), vm(tb, RW_W, BF16), vm(tb, RW_W, BF16), vm(tb, RW_W, BF16),
               vm(tb, RW_W, F32), vm(tb, RW_W, F32), vm(tb, RW_W, F32), vm(tb, RW_W, F32),
               vm(2 * tb, RW_W, BF16), vm(tb, RW_W, F32), vm(tb, 2 * RW_W, BF16),
               vm(2 * tb, RW_W, BF16), vm(2 * tb, RW_W, F32), vm(2 * tb, RW_W, F32)]
    assert len(scratch) == RW_N_SCRATCH
    outs = pl.pallas_call(
        functools.partial(_rwkv_kernel, tb=tb, has_vres=has_vres, cpi=cpi, n_seq=n_seq, n_shared=len(shared)),
        grid=(batch // n_seq, seq // tb),
        in_specs=in_specs, out_specs=out_specs, out_shape=out_shape,
        scratch_shapes=scratch * n_seq,
        compiler_params=pltpu.CompilerParams(
            dimension_semantics=("parallel", "arbitrary"), vmem_limit_bytes=VMEM_LIMIT),
        name="rwkv7",
    )(*ins)
    outs = [o.reshape(batch * seq, RW_W) for o in outs]
    return outs if not has_vres else (outs[0], None)


def _outproj_kernel(x_ref, om_ref, od_ref, orw_ref, w_ref, fg_ref, o_ref, *, final):
    mixed = jnp.concatenate([om_ref[...], od_ref[...], orw_ref[...]], axis=1)
    y = x_ref[...] + _dot(mixed, w_ref[...])
    if final:
        y = _rms(y, fg_ref[...], NORM_EPS)
    o_ref[...] = y


def _outproj(x2, om, od, orw, w, fg, *, tm, final):
    t = x2.shape[0]
    row = lambda i: (i, 0)
    const = lambda i: (0, 0)
    return pl.pallas_call(
        functools.partial(_outproj_kernel, final=final),
        grid=(t // tm,),
        in_specs=[pl.BlockSpec((tm, D_MODEL), row), pl.BlockSpec((tm, MLA_W), row),
                  pl.BlockSpec((tm, DIFF_W), row), pl.BlockSpec((tm, RW_W), row),
                  pl.BlockSpec(w.shape, const), pl.BlockSpec(fg.shape, const)],
        out_specs=pl.BlockSpec((tm, D_MODEL), row),
        out_shape=jax.ShapeDtypeStruct((t, D_MODEL), F32),
        compiler_params=pltpu.CompilerParams(
            dimension_semantics=("parallel",), vmem_limit_bytes=VMEM_LIMIT),
        name="outproj",
    )(x2, om, od, orw, w, fg)


def _pack_inproj(w_in, w_vres):
    d = w_in.shape[0]
    o = 0
    cq = w_in[:, o:o + MLA_Q_RANK]; o += MLA_Q_RANK
    ckv = w_in[:, o:o + MLA_KV_RANK]; o += MLA_KV_RANK
    kpe = w_in[:, o:o + MLA_ROPE]; o += MLA_ROPE
    dqk = w_in[:, o:o + DIFF_G]; o += DIFF_G
    dv = w_in[:, o:o + DIFF_W]; o += DIFF_W
    gate = w_in[:, o:o + D_MIX]; o += D_MIX
    rw = w_in[:, o:]
    half = MLA_ROPE // 2
    z = lambda n: jnp.zeros((d, n), w_in.dtype)
    kp = jnp.concatenate([kpe, kpe[:, half:], kpe[:, :half], z(LANE - 2 * MLA_ROPE)], axis=1)
    g_mla, g_diff, g_rw = gate[:, :MLA_W], gate[:, MLA_W:MLA_W + DIFF_W], gate[:, MLA_W + DIFF_W:]
    cols = [cq, ckv, kp, dqk, g_rw, g_mla, g_diff, rw]
    if w_vres is not None:
        cols += [w_vres, z(LANE - RW_MV_RANK)]
    return jnp.concatenate(cols, axis=1).astype(BF16), dv.T.astype(BF16)


def _pack_mla(w_uq, w_ukv):
    rq, rkv = w_uq.shape[0], w_ukv.shape[0]
    q3 = w_uq.reshape(rq, MLA_HEADS, MLA_QK)
    zq = lambda n: jnp.zeros((rq, MLA_HEADS, n), w_uq.dtype)
    wq = jnp.concatenate([q3, zq(HP - MLA_QK)], axis=2)
    kv3 = w_ukv.reshape(rkv, MLA_HEADS, MLA_NOPE + MLA_V)
    wk = jnp.concatenate([kv3[:, :, :MLA_NOPE], jnp.zeros((rkv, MLA_HEADS, HP - MLA_NOPE), w_ukv.dtype)], axis=2)
    wv = kv3[:, :, MLA_NOPE:]
    flat = lambda a: a.reshape(a.shape[0], -1).astype(BF16)
    return flat(wq), flat(wk), flat(wv).T


def _pad_rows(w, top, total):
    return jnp.concatenate([jnp.zeros((top, w.shape[1]), w.dtype), w,
                            jnp.zeros((total - top - w.shape[0], w.shape[1]), w.dtype)], axis=0).astype(BF16)


def _rwkv_consts(tb):
    t = jnp.arange(tb)
    same = (t[:, None] // RW_CHUNK) == (t[None, :] // RW_CHUNK)
    tri = (same & (t[None, :] <= t[:, None])).astype(BF16)
    hl = jnp.arange(LANE) // RW_N
    seg = (hl[:, None] == hl[None, :]).astype(BF16)
    return tri, seg


def kernel(x, positions, pre_g, w_in, w_in_vres, w_out, mla_gq, mla_gkv, mla_wuq, mla_wukv, diff_lam, diff_gsub, rw_mu, rw_mu_vres, rw_w0, rw_w2, rw_a0, rw_a2, rw_v0, rw_v2, rw_kk, rw_ka, rw_rk, rw_lnw, rw_lnb, final_g):
    batch, seq, _ = x.shape
    depth = pre_g.shape[0]
    tm = min(512, seq)
    tq = min(512, seq)
    tkv = tq // 2
    tb = min(256, seq)
    assert seq % tm == 0 and seq % tq == 0 and tm % tkv == 0
    assert seq % tb == 0 and tb % RW_CHUNK == 0

    x2 = x.reshape(batch * seq, D_MODEL)
    tab = _rope_table(positions)
    pos_f = positions.astype(F32)
    posc = jnp.broadcast_to(pos_f.reshape(seq, 1), (seq, LANE))
    consts = _rwkv_consts(tb)
    r1 = lambda a: a.reshape(1, -1).astype(F32)

    vfirst = None
    for layer in range(depth):
        vres = layer > 0
        w, wdvt = _pack_inproj(w_in[layer], w_in_vres[layer - 1] if vres else None)
        wq, wk, wvt = _pack_mla(mla_wuq[layer], mla_wukv[layer])
        q, k, vt, d, dvt, gate, rw = _inproj(x2, r1(pre_g[layer]), w, tab, r1(mla_gq[layer]), r1(mla_gkv[layer]),
                                             wq, wk, wvt, wdvt,
                                             seq=seq, tm=min(2 * tm, seq), tkv=tkv)
        lam_init = 0.8 - 0.6 * math.exp(-0.3 * (layer + 1))
        gsub_col = diff_gsub[layer].reshape(-1, 1).astype(F32)
        o_mla, o_diff = _attn(q, k, vt, d, dvt, gate, posc, diff_lam[layer].astype(F32), gsub_col,
                              batch=batch, seq=seq, tq=tq, lam_init=lam_init)
        mu = rw_mu[layer]
        if vres:
            mu = jnp.concatenate([mu, rw_mu_vres[layer - 1], jnp.zeros((LANE - RW_MV_RANK,), mu.dtype)])
        params = [r1(mu), r1(rw_w0[layer]), _pad_rows(rw_w2[layer], 0, LANE), r1(rw_a0[layer]),
                  _pad_rows(rw_a2[layer], RW_DECAY_RANK, LANE)]
        if vres:
            params += [r1(rw_v0[layer - 1]), _pad_rows(rw_v2[layer - 1], 0, LANE)]
        params += [r1(rw_kk[layer]), r1(rw_ka[layer]), r1(rw_rk[layer]), r1(rw_lnw[layer]), r1(rw_lnb[layer])]
        o_rw, vf = _rwkv(rw, vfirst, gate, consts, params, batch=batch, seq=seq, tb=tb, cpi=4,
                          n_seq=2 if batch % 2 == 0 else 1)
        if not vres:
            vfirst = vf
        x2 = _outproj(x2, o_mla, o_diff, o_rw, w_out[layer].astype(BF16), r1(final_g),
                      tm=min(2 * tm, seq), final=(layer == depth - 1))
    return x2.reshape(batch, seq, D_MODEL)
```
